```python
import jax, jax.numpy as jnp
from jax import lax
import numpy as np

D_MODEL = 2048
BATCH = 8
SEQ = 4096
DEPTH = 1
DEC_BATCH = 8
DEC_SEQ = 16
PAST_LEN = 2048

CHUNK = 64
HEAD_DIM = 64
ATTN_WIDTH = D_MODEL // 2
CONV_CH = D_MODEL - ATTN_WIDTH
N_HEADS = ATTN_WIDTH // HEAD_DIM
N_KV_HEADS = N_HEADS // 4
GQA_GROUP = N_HEADS // N_KV_HEADS
ROT_DIM = HEAD_DIM // 4
ROPE_THETA = 500000.0
WINDOW = 128
WINDOW_CHUNKS = WINDOW // CHUNK
CONV_WIDTH = 31
N_EXPERTS = 32
TOP_K = 4
D_FF = D_MODEL
SWIGLU_LIMIT = 7.0
SWIGLU_ALPHA = 1.702
MOE_BLOCK = 256
EPS = 1e-5
NEG_INF = -1e30
Q_COLS = N_HEADS * HEAD_DIM
KV_COLS = N_KV_HEADS * HEAD_DIM
IN_COLS = Q_COLS + 2 * KV_COLS + 2 * CONV_CH

kernel_name = 'hybrid_swa_sink_conformer_moe_stream_step'


def _rmsnorm(x, g):
    xf = x.astype(jnp.float32)
    y = xf * lax.rsqrt(jnp.mean(xf * xf, axis=-1, keepdims=True) + EPS)
    return (y * g.astype(jnp.float32)).astype(x.dtype)


def _layernorm(x, g, b):
    xf = x.astype(jnp.float32)
    mu = jnp.mean(xf, axis=-1, keepdims=True)
    var = jnp.mean(jnp.square(xf - mu), axis=-1, keepdims=True)
    y = (xf - mu) * lax.rsqrt(var + EPS)
    return (y * g.astype(jnp.float32) + b.astype(jnp.float32)).astype(x.dtype)


def _rotary(x, pos):
    half = ROT_DIM // 2
    inv_freq = ROPE_THETA ** (-jnp.arange(0, ROT_DIM, 2, dtype=jnp.float32) / ROT_DIM)
    ang = pos.astype(jnp.float32)[:, None] * inv_freq[None, :]
    cos = jnp.cos(ang)[None, :, None, :]
    sin = jnp.sin(ang)[None, :, None, :]
    xr = x[..., :ROT_DIM].astype(jnp.float32)
    x1, x2 = xr[..., :half], xr[..., half:]
    rot = jnp.concatenate([x1 * cos - x2 * sin, x2 * cos + x1 * sin], axis=-1)
    return jnp.concatenate([rot.astype(x.dtype), x[..., ROT_DIM:]], axis=-1)


def _modulation(c, w_ada, b_ada):
    m = jax.nn.silu(c) @ w_ada + b_ada
    return m.reshape(c.shape[0], 6, 1, D_MODEL)


def _mixer_inputs(x, shift, scale, g_mix, w_in, pos):
    b, s, _ = x.shape
    h = _rmsnorm(x, g_mix) * (1 + scale) + shift
    z = h @ w_in
    q = z[..., :Q_COLS].reshape(b, s, N_HEADS, HEAD_DIM)
    k = z[..., Q_COLS:Q_COLS + KV_COLS].reshape(b, s, N_KV_HEADS, HEAD_DIM)
    v = z[..., Q_COLS + KV_COLS:Q_COLS + 2 * KV_COLS].reshape(b, s, N_KV_HEADS, HEAD_DIM)
    o = Q_COLS + 2 * KV_COLS
    u = z[..., o:o + CONV_CH] * jax.nn.sigmoid(z[..., o + CONV_CH:])
    return _rotary(q, pos), _rotary(k, pos), v, u


def _sink_softmax(sc, sink):
    sk = jnp.broadcast_to(sink.astype(jnp.float32).reshape(N_KV_HEADS, GQA_GROUP, 1, 1), sc.shape[:-1] + (1,))
    p = jax.nn.softmax(jnp.concatenate([sc, sk], axis=-1), axis=-1)
    return p[..., :-1]


def _band_attention(q, k, v, sink):
    b, s = q.shape[0], q.shape[1]
    n_c = s // CHUNK
    qb = q.reshape(b, n_c, CHUNK, N_KV_HEADS, GQA_GROUP, HEAD_DIM)
    pad = WINDOW_CHUNKS * CHUNK

    def band(t):
        tp = jnp.pad(t, ((0, 0), (pad, 0), (0, 0), (0, 0))).reshape(b, n_c + WINDOW_CHUNKS, CHUNK, N_KV_HEADS, HEAD_DIM)
        return jnp.concatenate([tp[:, j:j + n_c] for j in range(WINDOW_CHUNKS + 1)], axis=2)

    kb, vb = band(k), band(v)
    key_chunk = jnp.arange(n_c)[:, None] + jnp.arange(WINDOW_CHUNKS + 1)[None, :] - WINDOW_CHUNKS
    valid = jnp.repeat(key_chunk >= 0, CHUNK, axis=1)
    sc = jnp.einsum('bnqkgd,bnskd->bnkgqs', qb, kb, preferred_element_type=jnp.float32) * (HEAD_DIM ** -0.5)
    sc = jnp.where(valid[None, :, None, None, None, :], sc, NEG_INF)
    p = _sink_softmax(sc, sink).astype(v.dtype)
    o = jnp.einsum('bnkgqs,bnskd->bnqkgd', p, vb)
    return o.reshape(b, s, Q_COLS)


def _cached_attention(q, k, v, cache_k, cache_v, sink):
    b, s = q.shape[0], q.shape[1]
    qg = q.reshape(b, s, N_KV_HEADS, GQA_GROUP, HEAD_DIM)
    k_all = jnp.concatenate([cache_k, k], axis=1)
    v_all = jnp.concatenate([cache_v, v], axis=1)
    sc = jnp.einsum('bqkgd,bskd->bkgqs', qg, k_all, preferred_element_type=jnp.float32) * (HEAD_DIM ** -0.5)
    p = _sink_softmax(sc, sink).astype(v.dtype)
    o = jnp.einsum('bkgqs,bskd->bqkgd', p, v_all)
    return o.reshape(b, s, Q_COLS)


def _conv_tail(u_hist, conv_w, conv_b, ln_g, ln_b):
    y = lax.conv_general_dilated(u_hist, conv_w, window_strides=(1,), padding='VALID',
                                 dimension_numbers=('NWC', 'WIO', 'NWC'),
                                 feature_group_count=CONV_CH) + conv_b
    return jax.nn.silu(_layernorm(y, ln_g, ln_b))


def _moe(h, w_router, b_router, w_gu, b_gu, w_down, b_down):
    b, s, d = h.shape
    t = b * s
    xf = h.reshape(t, d)
    logits = (xf @ w_router + b_router).astype(jnp.float32)
    top_v, top_i = lax.top_k(logits, TOP_K)
    gates = jax.nn.softmax(top_v, axis=-1).astype(h.dtype)
    tk = t * TOP_K
    blk = max(8, min(MOE_BLOCK, tk // N_EXPERTS))
    n_blk = -(-tk // blk) + N_EXPERTS
    flat_e = top_i.reshape(tk)
    flat_tok = jnp.arange(tk, dtype=jnp.int32) // TOP_K
    flat_g = gates.reshape(tk)
    order = jnp.argsort(flat_e)
    se = flat_e[order]
    counts = jnp.bincount(flat_e, length=N_EXPERTS)
    starts = jnp.cumsum(counts) - counts
    padded = (counts + blk - 1) // blk * blk
    pends = jnp.cumsum(padded)
    pstarts = pends - padded
    dest = pstarts[se] + jnp.arange(tk, dtype=jnp.int32) - starts[se]
    rows = n_blk * blk
    row_tok = jnp.full((rows,), t, dtype=jnp.int32).at[dest].set(flat_tok[order])
    row_gate = jnp.zeros((rows,), h.dtype).at[dest].set(flat_g[order])
    block_e = jnp.minimum(jnp.searchsorted(pends, jnp.arange(n_blk, dtype=jnp.int32) * blk, side='right'), N_EXPERTS - 1)
    xpad = jnp.concatenate([xf, jnp.zeros((1, d), xf.dtype)], axis=0)
    xin = xpad[row_tok].reshape(n_blk, blk, d)

    def expert(args):
        xb, e = args
        gu = xb @ w_gu[e] + b_gu[e]
        g, u = gu[:, :D_FF], gu[:, D_FF:]
        g = jnp.minimum(g, SWIGLU_LIMIT)
        u = jnp.clip(u, -SWIGLU_LIMIT, SWIGLU_LIMIT)
        a = g * jax.nn.sigmoid(SWIGLU_ALPHA * g) * (u + 1)
        return a @ w_down[e] + b_down[e]

    out = lax.map(expert, (xin, block_e)).reshape(rows, d)
    y = jnp.zeros((t + 1, d), h.dtype).at[row_tok].add(out * row_gate[:, None])
    return y[:t].reshape(b, s, d)


def _finish(x, attn, conv, gate1, shift2, scale2, gate2, w_out, g_ffn,
            w_router, b_router, w_gu, b_gu, w_down, b_down):
    x = x + gate1 * (jnp.concatenate([attn, conv], axis=-1) @ w_out)
    h = _rmsnorm(x, g_ffn) * (1 + scale2) + shift2
    return x + gate2 * _moe(h, w_router, b_router, w_gu, b_gu, w_down, b_down)


def setup_inputs(seed: int = 0) -> dict:
    key = jax.random.key(seed)
    ks = jax.random.split(key, 32)
    f32 = jnp.float32

    def nrm(k, shape, std):
        return jax.random.normal(k, shape, f32) * std

    L = DEPTH
    D = D_MODEL
    win = min(WINDOW, PAST_LEN)
    return {
        'x_prompt': nrm(ks[0], (BATCH, SEQ, D), 1.0),
        'x_sample': nrm(ks[1], (DEC_BATCH, DEC_SEQ, D), 1.0),
        'c_prompt': nrm(ks[2], (BATCH, D), 1.0),
        'c_sample': nrm(ks[3], (DEC_BATCH, D), 1.0),
        'cache_k': nrm(ks[4], (L, DEC_BATCH, win, N_KV_HEADS, HEAD_DIM), 1.0),
        'cache_v': nrm(ks[5], (L, DEC_BATCH, win, N_KV_HEADS, HEAD_DIM), 1.0),
        'state_conv': nrm(ks[6], (L, DEC_BATCH, CONV_WIDTH - 1, CONV_CH), 0.5),
        'w_ada': nrm(ks[7], (L, D, 6 * D), 0.5 * D ** -0.5),
        'b_ada': nrm(ks[8], (L, 6 * D), 0.02),
        'g_mix': 1.0 + nrm(ks[9], (L, D), 0.02),
        'w_in': nrm(ks[10], (L, D, IN_COLS), D ** -0.5),
        'attn_sink': nrm(ks[11], (L, N_HEADS), 0.5),
        'conv_w': nrm(ks[12], (L, CONV_WIDTH, 1, CONV_CH), CONV_WIDTH ** -0.5),
        'conv_b': nrm(ks[13], (L, CONV_CH), 0.02),
        'conv_ln_g': 1.0 + nrm(ks[14], (L, CONV_CH), 0.02),
        'conv_ln_b': nrm(ks[15], (L, CONV_CH), 0.02),
        'w_out': nrm(ks[16], (L, ATTN_WIDTH + CONV_CH, D), (ATTN_WIDTH + CONV_CH) ** -0.5),
        'g_ffn': 1.0 + nrm(ks[17], (L, D), 0.02),
        'w_router': nrm(ks[18], (L, D, N_EXPERTS), D ** -0.5),
        'b_router': nrm(ks[19], (L, N_EXPERTS), 0.01),
        'w_gu': nrm(ks[20], (L, N_EXPERTS, D, 2 * D_FF), D ** -0.5),
        'b_gu': nrm(ks[21], (L, N_EXPERTS, 2 * D_FF), 0.02),
        'w_down': nrm(ks[22], (L, N_EXPERTS, D_FF, D), D_FF ** -0.5),
        'b_down': nrm(ks[23], (L, N_EXPERTS, D), 0.02),
        'g_final': 1.0 + nrm(ks[24], (D,), 0.02),
    }


def reference(x_prompt, x_sample, c_prompt, c_sample, cache_k, cache_v, state_conv,
              w_ada, b_ada, g_mix, w_in, attn_sink, conv_w, conv_b, conv_ln_g, conv_ln_b,
              w_out, g_ffn, w_router, b_router, w_gu, b_gu, w_down, b_down, g_final):
    hp, hs = x_prompt, x_sample
    pos_p = jnp.arange(x_prompt.shape[1], dtype=jnp.int32)
    pos_s = PAST_LEN + jnp.arange(x_sample.shape[1], dtype=jnp.int32)
    new_kp, new_vp, new_cp = [], [], []
    new_ks, new_vs, new_cs = [], [], []
    for l in range(DEPTH):
        conv_p = (conv_w[l], conv_b[l], conv_ln_g[l], conv_ln_b[l])
        moe_p = (w_router[l], b_router[l], w_gu[l], b_gu[l], w_down[l], b_down[l])
        m = _modulation(c_prompt, w_ada[l], b_ada[l])
        q, k, v, u = _mixer_inputs(hp, m[:, 0], m[:, 1], g_mix[l], w_in[l], pos_p)
        att = _band_attention(q, k, v, attn_sink[l])
        cv = _conv_tail(jnp.pad(u, ((0, 0), (CONV_WIDTH - 1, 0), (0, 0))), *conv_p)
        hp = _finish(hp, att, cv, m[:, 2], m[:, 3], m[:, 4], m[:, 5], w_out[l], g_ffn[l], *moe_p)
        keep = min(WINDOW, k.shape[1])
        new_kp.append(k[:, -keep:])
        new_vp.append(v[:, -keep:])
        new_cp.append(u[:, -(CONV_WIDTH - 1):])
        m = _modulation(c_sample, w_ada[l], b_ada[l])
        q, k, v, u = _mixer_inputs(hs, m[:, 0], m[:, 1], g_mix[l], w_in[l], pos_s)
        att = _cached_attention(q, k, v, cache_k[l], cache_v[l], attn_sink[l])
        u_hist = jnp.concatenate([state_conv[l], u], axis=1)
        cv = _conv_tail(u_hist, *conv_p)
        hs = _finish(hs, att, cv, m[:, 2], m[:, 3], m[:, 4], m[:, 5], w_out[l], g_ffn[l], *moe_p)
        new_ks.append(k)
        new_vs.append(v)
        new_cs.append(u_hist[:, -(CONV_WIDTH - 1):])
    y_prompt = _rmsnorm(hp, g_final)
    y_sample = _rmsnorm(hs, g_final)
    return (y_prompt, y_sample, jnp.stack(new_kp), jnp.stack(new_vp), jnp.stack(new_cp),
            jnp.stack(new_ks), jnp.stack(new_vs), jnp.stack(new_cs))
```

```python
import functools

import jax
import jax.numpy as jnp
from jax import lax
from jax.experimental import pallas as pl
from jax.experimental.pallas import tpu as pltpu

D_MODEL = 2048
CHUNK = 64
HEAD_DIM = 64
ATTN_WIDTH = D_MODEL // 2
CONV_CH = D_MODEL - ATTN_WIDTH
N_HEADS = ATTN_WIDTH // HEAD_DIM
N_KV_HEADS = N_HEADS // 4
GQA_GROUP = N_HEADS // N_KV_HEADS
ROT_DIM = HEAD_DIM // 4
ROPE_THETA = 500000.0
WINDOW = 128
WINDOW_CHUNKS = WINDOW // CHUNK
CONV_WIDTH = 31
TOP_K = 4
SWIGLU_LIMIT = 7.0
SWIGLU_ALPHA = 1.702
EPS = 1e-5
NEG_INF = -1e30
PAST_LEN = 2048
Q_COLS = N_HEADS * HEAD_DIM
KV_COLS = N_KV_HEADS * HEAD_DIM
IN_COLS = Q_COLS + 2 * KV_COLS + 2 * CONV_CH

LANES = 128
CONV_HALO = 32
MIB = 1024 * 1024

_BF16 = jnp.bfloat16
_F32 = jnp.float32
_I32 = jnp.int32


def _params(semantics, vmem_mib):
    return pltpu.CompilerParams(dimension_semantics=semantics, vmem_limit_bytes=vmem_mib * MIB)


def _dot(a, b):
    return jnp.dot(a, b, preferred_element_type=_F32)


def _rmsnorm_mod(x, g, scale, shift):
    ms = jnp.mean(x * x, axis=-1, keepdims=True)
    return (x * lax.rsqrt(ms + EPS) * g) * (1.0 + scale) + shift


def _ada_body(c_ref, w_ref, b_ref, o_ref):
    c = c_ref[...]
    s = (c * jax.nn.sigmoid(c)).astype(_BF16)
    o_ref[...] = _dot(s, w_ref[...].astype(_BF16)) + b_ref[...]


def _modulation(c, w_ada, b_ada):
    n, d = c.shape
    cols = w_ada.shape[1]
    tn = 1024
    return pl.pallas_call(
        _ada_body,
        grid=(cols // tn,),
        in_specs=[
            pl.BlockSpec((n, d), lambda j: (0, 0)),
            pl.BlockSpec((d, tn), lambda j: (0, j)),
            pl.BlockSpec((1, tn), lambda j: (0, j)),
        ],
        out_specs=pl.BlockSpec((n, tn), lambda j: (0, j)),
        out_shape=jax.ShapeDtypeStruct((n, cols), _F32),
        compiler_params=_params(("arbitrary",), 40),
        name="modulation",
    )(c, w_ada, b_ada.reshape(1, cols))


def _rope_tables(pos):
    half = ROT_DIM // 2
    inv_freq = ROPE_THETA ** (-jnp.arange(0, ROT_DIM, 2, dtype=_F32) / ROT_DIM)
    ang = pos.astype(_F32)[:, None] * inv_freq[None, :]
    cos, sin = jnp.cos(ang), jnp.sin(ang)
    d = jnp.arange(LANES) % HEAD_DIM
    cos_l = jnp.where(d < ROT_DIM, cos[:, d % half], 1.0)
    sa = jnp.where(d < half, -sin[:, d % half], 0.0)
    sb = jnp.where((d >= half) & (d < ROT_DIM), sin[:, d % half], 0.0)
    return cos_l, sa, sb


def _rotate(z, cos, sa, sb):
    half = ROT_DIM // 2
    parts = []
    for j in range(z.shape[1] // LANES):
        zj = z[:, j * LANES:(j + 1) * LANES]
        parts.append(zj * cos + pltpu.roll(zj, LANES - half, 1) * sa + pltpu.roll(zj, half, 1) * sb)
    return parts[0] if len(parts) == 1 else jnp.concatenate(parts, axis=1)


def _in_proj_body(x_ref, mod_ref, g_ref, w_ref, cos_ref, sa_ref, sb_ref, q_ref, k_ref, v_ref, u_ref):
    h = _rmsnorm_mod(x_ref[...], g_ref[...], mod_ref[1, 0], mod_ref[0, 0]).astype(_BF16)
    cos, sa, sb = cos_ref[...], sa_ref[...], sb_ref[...]
    cw = 512
    for c in range(Q_COLS // cw):
        z = _dot(h, w_ref[:, c * cw:(c + 1) * cw])
        q_ref[:, c * cw:(c + 1) * cw] = _rotate(z, cos, sa, sb).astype(_BF16)
    z = _dot(h, w_ref[:, Q_COLS:Q_COLS + 2 * KV_COLS])
    k_ref[...] = _rotate(z[:, :KV_COLS], cos, sa, sb)
    v_ref[...] = z[:, KV_COLS:]
    o = Q_COLS + 2 * KV_COLS
    for c in range(CONV_CH // cw):
        zv = _dot(h, w_ref[:, o + c * cw:o + (c + 1) * cw])
        zg = _dot(h, w_ref[:, o + CONV_CH + c * cw:o + CONV_CH + (c + 1) * cw])
        u_ref[:, c * cw:(c + 1) * cw] = zv * jax.nn.sigmoid(zg)


def _mod_spec(mods4, tiles_per_group):
    _, _, r, d = mods4.shape
    return pl.BlockSpec((6, 1, r, d), lambda i: (0, i // tiles_per_group, 0, 0))


def _in_proj(x2d, mods4, g_mix, w_in_b, tabs, tm):
    t, d = x2d.shape
    n_tiles = t // tm
    tiles_per_group = n_tiles // mods4.shape[1]
    pos_tiles = tabs[0].shape[0] // tm
    tab_spec = pl.BlockSpec((tm, LANES), lambda i: (i % pos_tiles, 0))
    row = lambda w: pl.BlockSpec((tm, w), lambda i: (i, 0))
    return pl.pallas_call(
        _in_proj_body,
        grid=(n_tiles,),
        in_specs=[
            row(d),
            _mod_spec(mods4, tiles_per_group),
            pl.BlockSpec((1, d), lambda i: (0, 0)),
            pl.BlockSpec((d, IN_COLS), lambda i: (0, 0)),
            tab_spec, tab_spec, tab_spec,
        ],
        out_specs=[row(Q_COLS), row(KV_COLS), row(KV_COLS), row(CONV_CH)],
        out_shape=[
            jax.ShapeDtypeStruct((t, Q_COLS), _BF16),
            jax.ShapeDtypeStruct((t, KV_COLS), _F32),
            jax.ShapeDtypeStruct((t, KV_COLS), _F32),
            jax.ShapeDtypeStruct((t, CONV_CH), _F32),
        ],
        compiler_params=_params(("arbitrary",), 56),
        name="in_proj",
    )(x2d, mods4, g_mix.reshape(1, d), w_in_b, *tabs)


def _attn_group(qg, kk, vv, sink_col, valid):
    s = lax.dot_general(qg, kk, (((1,), (1,)), ((), ())), preferred_element_type=_F32) * (HEAD_DIM ** -0.5)
    if valid is not None:
        s = jnp.where(valid, s, NEG_INF)
    m = jnp.maximum(jnp.max(s, axis=-1, keepdims=True), sink_col)
    p = jnp.exp(s - m)
    den = jnp.sum(p, axis=-1, keepdims=True) + jnp.exp(sink_col - m)
    return _dot(p.astype(_BF16), vv) / den


def _sink_column(sink_ref, kh, rows_per_head):
    row = lax.broadcasted_iota(_I32, (GQA_GROUP * rows_per_head, 1), 0)
    col = jnp.full(row.shape, sink_ref[kh * GQA_GROUP + GQA_GROUP - 1], _F32)
    for g in range(GQA_GROUP - 2, -1, -1):
        col = jnp.where(row < (g + 1) * rows_per_head, sink_ref[kh * GQA_GROUP + g], col)
    return col


def _store_heads(o_ref, r0, rows, kh, o):
    for pair in range(GQA_GROUP // 2):
        both = jnp.concatenate([o[(2 * pair) * rows:(2 * pair + 1) * rows],
                                o[(2 * pair + 1) * rows:(2 * pair + 2) * rows]], axis=1)
        c0 = (kh * GQA_GROUP + 2 * pair) * HEAD_DIM
        o_ref[r0:r0 + rows, c0:c0 + 2 * HEAD_DIM] = both.astype(o_ref.dtype)


def _stack_heads(q, r0, rows, kh):
    return jnp.concatenate(
        [q[r0:r0 + rows, (kh * GQA_GROUP + g) * HEAD_DIM:(kh * GQA_GROUP + g + 1) * HEAD_DIM] for g in range(GQA_GROUP)],
        axis=0)


def _band_attn_body(sink_ref, q_ref, kc_ref, kh_ref, vc_ref, vh_ref, o_ref, *, tq):
    j = pl.program_id(1)
    q = q_ref[...]
    kall = jnp.concatenate([kh_ref[...], kc_ref[...]], axis=0)
    vall = jnp.concatenate([vh_ref[...], vc_ref[...]], axis=0)
    band = (WINDOW_CHUNKS + 1) * CHUNK
    key_chunk = lax.broadcasted_iota(_I32, (1, band), 1) // CHUNK
    for kh in range(N_KV_HEADS):
        sink_col = _sink_column(sink_ref, kh, CHUNK)
        kk = kall[:, kh * HEAD_DIM:(kh + 1) * HEAD_DIM].astype(_BF16)
        vv = vall[:, kh * HEAD_DIM:(kh + 1) * HEAD_DIM].astype(_BF16)
        for ci in range(tq // CHUNK):
            qg = _stack_heads(q, ci * CHUNK, CHUNK, kh)
            valid = None if ci >= WINDOW_CHUNKS else (j * (tq // CHUNK) + ci - WINDOW_CHUNKS + key_chunk) >= 0
            o = _attn_group(qg, kk[ci * CHUNK:ci * CHUNK + band], vv[ci * CHUNK:ci * CHUNK + band], sink_col, valid)
            _store_heads(o_ref, ci * CHUNK, CHUNK, kh, o)


def _band_attention(q, k, v, sink, batch, seq, tq):
    nq = seq // tq
    r = tq // WINDOW
    cur = lambda w: pl.BlockSpec((tq, w), lambda b, j: (b * nq + j, 0))
    halo = pl.BlockSpec((WINDOW, KV_COLS), lambda b, j: (b * nq * r + jnp.maximum(j * r - 1, 0), 0))
    return pl.pallas_call(
        functools.partial(_band_attn_body, tq=tq),
        grid=(batch, nq),
        in_specs=[pl.BlockSpec(memory_space=pltpu.SMEM), cur(Q_COLS), cur(KV_COLS), halo, cur(KV_COLS), halo],
        out_specs=cur(Q_COLS),
        out_shape=jax.ShapeDtypeStruct((batch * seq, Q_COLS), _BF16),
        compiler_params=_params(("arbitrary", "arbitrary"), 40),
        name="band_attention",
    )(sink, q, k, k, v, v)


def _cached_attn_body(sink_ref, q_ref, kn_ref, kc_ref, vn_ref, vc_ref, o_ref, *, rows):
    q = q_ref[...]
    kall = jnp.concatenate([kc_ref[0], kn_ref[...]], axis=0)
    vall = jnp.concatenate([vc_ref[0], vn_ref[...]], axis=0)
    for kh in range(N_KV_HEADS):
        o = _attn_group(_stack_heads(q, 0, rows, kh), kall[:, kh * HEAD_DIM:(kh + 1) * HEAD_DIM].astype(_BF16),
                        vall[:, kh * HEAD_DIM:(kh + 1) * HEAD_DIM].astype(_BF16), _sink_column(sink_ref, kh, rows), None)
        _store_heads(o_ref, 0, rows, kh, o)


def _cached_attention(q, k, v, cache_k, cache_v, sink, batch, rows):
    win = cache_k.shape[1]
    new = lambda w: pl.BlockSpec((rows, w), lambda b: (b, 0))
    cache = pl.BlockSpec((1, win, KV_COLS), lambda b: (b, 0, 0))
    return pl.pallas_call(
        functools.partial(_cached_attn_body, rows=rows),
        grid=(batch,),
        in_specs=[pl.BlockSpec(memory_space=pltpu.SMEM), new(Q_COLS), new(KV_COLS), cache, new(KV_COLS), cache],
        out_specs=new(Q_COLS),
        out_shape=jax.ShapeDtypeStruct((batch * rows, Q_COLS), _BF16),
        compiler_params=_params(("arbitrary",), 40),
        name="cached_attention",
    )(sink, q, k, cache_k, v, cache_v)


def _conv_body(uc_ref, uh_ref, w_ref, b_ref, g_ref, beta_ref, o_ref, win_ref, y_ref, *, tt, zero_first_halo):
    halo = uh_ref[...]
    if zero_first_halo:
        halo = jnp.where(pl.program_id(1) == 0, 0.0, halo)
    win_ref[0:CONV_HALO, :] = halo
    win_ref[CONV_HALO:, :] = uc_ref[...]
    lead = CONV_HALO - (CONV_WIDTH - 1)
    for s in range(CONV_CH // LANES):
        cs = slice(s * LANES, (s + 1) * LANES)
        acc = w_ref[0:1, cs] * win_ref[lead:lead + tt, cs]
        for tap in range(1, CONV_WIDTH):
            acc = acc + w_ref[tap:tap + 1, cs] * win_ref[lead + tap:lead + tap + tt, cs]
        y_ref[:, cs] = acc + b_ref[:, cs]
    y = y_ref[...]
    mu = jnp.mean(y, axis=-1, keepdims=True)
    var = jnp.mean(jnp.square(y - mu), axis=-1, keepdims=True)
    z = (y - mu) * lax.rsqrt(var + EPS) * g_ref[...] + beta_ref[...]
    o_ref[...] = (z * jax.nn.sigmoid(z)).astype(o_ref.dtype)


def _conv_module(u, hist, conv_w, conv_b, ln_g, ln_b, batch, seq, tt):
    nt = seq // tt
    r = tt // CONV_HALO
    cur = pl.BlockSpec((tt, CONV_CH), lambda b, j: (b * nt + j, 0))
    if hist is None:
        hist_arr = u
        halo = pl.BlockSpec((CONV_HALO, CONV_CH), lambda b, j: (b * nt * r + jnp.maximum(j * r - 1, 0), 0))
    else:
        assert nt == 1
        hist_arr = hist
        halo = pl.BlockSpec((CONV_HALO, CONV_CH), lambda b, j: (b, 0))
    vec = pl.BlockSpec((1, CONV_CH), lambda b, j: (0, 0))
    wpad = jnp.pad(conv_w.reshape(CONV_WIDTH, CONV_CH), ((0, 1), (0, 0)))
    return pl.pallas_call(
        functools.partial(_conv_body, tt=tt, zero_first_halo=hist is None),
        grid=(batch, nt),
        in_specs=[cur, halo, pl.BlockSpec((CONV_WIDTH + 1, CONV_CH), lambda b, j: (0, 0)), vec, vec, vec],
        out_specs=cur,
        out_shape=jax.ShapeDtypeStruct((batch * seq, CONV_CH), _BF16),
        scratch_shapes=[pltpu.VMEM((tt + CONV_HALO, CONV_CH), _F32), pltpu.VMEM((tt, CONV_CH), _F32)],
        compiler_params=_params(("arbitrary", "arbitrary"), 40),
        name="conv_module",
    )(u, hist_arr, wpad, conv_b.reshape(1, -1), ln_g.reshape(1, -1), ln_b.reshape(1, -1))


REC_IDX, REC_GATE, REC_RANK = 0, TOP_K, 2 * TOP_K


def _out_proj_body(x_ref, a_ref, c_ref, mod_ref, g_ref, wa_ref, wc_ref, wr_ref, br_ref, cnt_ref,
                   x1_ref, h2_ref, rec_ref, cnt_out_ref, run_ref, *, n_experts):
    tm = x_ref.shape[0]

    @pl.when(pl.program_id(0) == 0)
    def _():
        run_ref[...] = cnt_ref[...]

    o = _dot(a_ref[...], wa_ref[...]) + _dot(c_ref[...], wc_ref[...])
    x1 = x_ref[...] + mod_ref[2, 0] * o
    x1_ref[...] = x1
    h2 = _rmsnorm_mod(x1, g_ref[...], mod_ref[4, 0], mod_ref[3, 0])
    h2_ref[...] = h2

    logits = jnp.dot(h2, wr_ref[...], precision=lax.Precision.HIGHEST, preferred_element_type=_F32) + br_ref[...]
    lane = lax.broadcasted_iota(_I32, (tm, LANES), 1)
    work = jnp.where(lane < n_experts, logits, -jnp.inf)
    vals, hots = [], []
    rec = jnp.zeros((tm, LANES), _F32)
    for k in range(TOP_K):
        m = jnp.max(work, axis=-1, keepdims=True)
        idx = jnp.min(jnp.where(work == m, lane, LANES), axis=-1, keepdims=True)
        hot = lane == idx
        work = jnp.where(hot, -jnp.inf, work)
        vals.append(m)
        hots.append(hot)
        rec = jnp.where(lane == REC_IDX + k, idx.astype(_F32), rec)
    exps = [jnp.exp(v - vals[0]) for v in vals]
    den = exps[0]
    for e in exps[1:]:
        den = den + e
    for k in range(TOP_K):
        rec = jnp.where(lane == REC_GATE + k, exps[k] / den, rec)

    chosen = jnp.zeros((tm, LANES), _F32)
    for hot in hots:
        chosen = jnp.where(hot, 1.0, chosen)
    r_i = lax.broadcasted_iota(_I32, (tm, tm), 0)
    c_i = lax.broadcasted_iota(_I32, (tm, tm), 1)
    before = _dot(jnp.where(c_i < r_i, 1.0, 0.0).astype(_BF16), chosen.astype(_BF16)) + run_ref[...]
    for k in range(TOP_K):
        rank = jnp.sum(jnp.where(hots[k], before, 0.0), axis=-1, keepdims=True)
        rec = jnp.where(lane == REC_RANK + k, rank, rec)
    rec_ref[...] = rec
    run_ref[...] = run_ref[...] + jnp.sum(chosen, axis=0, keepdims=True)
    cnt_out_ref[...] = run_ref[...]


def _out_proj(x2d, attn, conv, mods4, g_ffn, wo_a, wo_c, wr_pad, br_pad, counts, tm, n_experts):
    t, d = x2d.shape
    n_tiles = t // tm
    tiles_per_group = n_tiles // mods4.shape[1]
    row = lambda w: pl.BlockSpec((tm, w), lambda i: (i, 0))
    const = lambda a, b: pl.BlockSpec((a, b), lambda i: (0, 0))
    return pl.pallas_call(
        functools.partial(_out_proj_body, n_experts=n_experts),
        grid=(n_tiles,),
        in_specs=[row(d), row(ATTN_WIDTH), row(CONV_CH), _mod_spec(mods4, tiles_per_group), const(1, d),
                  const(ATTN_WIDTH, d), const(CONV_CH, d), const(d, LANES), const(1, LANES), const(1, LANES)],
        out_specs=[row(d), row(d), row(LANES), const(1, LANES)],
        out_shape=[
            jax.ShapeDtypeStruct((t, d), _F32),
            jax.ShapeDtypeStruct((t, d), _F32),
            jax.ShapeDtypeStruct((t, LANES), _F32),
            jax.ShapeDtypeStruct((1, LANES), _F32),
        ],
        scratch_shapes=[pltpu.VMEM((1, LANES), _F32)],
        compiler_params=_params(("arbitrary",), 56),
        name="out_proj_router",
    )(x2d, attn, conv, mods4, g_ffn.reshape(1, d), wo_a, wo_c, wr_pad, br_pad, counts)


def _row_copy(src, dst, sem):
    return pltpu.make_async_copy(src, dst, sem)


def _drain_one_step_late(i, n, sems, wait_one, copies_per_step):
    def drain(slot):
        def body(r, carry):
            wait_one(sems.at[slot])
            return carry
        lax.fori_loop(0, copies_per_step, body, 0)

    @pl.when(i > 0)
    def _():
        drain((i - 1) % 2)

    @pl.when(i == n - 1)
    def _():
        drain(i % 2)


def _dispatch_body(pends_ref, padded_ref, nu_ref, dest_ref, hp_ref, hs_ref, zeros_ref, xs_ref, sems, zsem, *,
                   tt, bm, n_experts):
    i = pl.program_id(0)
    n = pl.num_programs(0)
    prompt_steps = hp_ref.shape[0] // tt
    n_blk = xs_ref.shape[0] // bm

    def zero_block(row0):
        return _row_copy(zeros_ref, xs_ref.at[pl.ds(pl.multiple_of(row0, bm), bm)], zsem)

    @pl.when(i == 0)
    def _():
        for e in range(n_experts):
            @pl.when(padded_ref[e] > 0)
            def _():
                zero_block(pends_ref[e] - bm).start()

        def tail_start(b, carry):
            zero_block(b * bm).start()
            return carry

        def tail_wait(b, carry):
            zero_block(b * bm).wait()
            return carry

        lax.fori_loop(nu_ref[0], n_blk, tail_start, 0)
        for e in range(n_experts):
            @pl.when(padded_ref[e] > 0)
            def _():
                zero_block(pends_ref[e] - bm).wait()
        lax.fori_loop(nu_ref[0], n_blk, tail_wait, 0)

    def issue_from(src_ref, row0):
        def issue(r, carry):
            src = src_ref.at[pl.ds(row0 + r, 1)]
            for k in range(TOP_K):
                _row_copy(src, xs_ref.at[pl.ds(dest_ref[0, 0, r * TOP_K + k], 1)], sems.at[i % 2]).start()
            return carry
        lax.fori_loop(0, tt, issue, 0)

    @pl.when(i < prompt_steps)
    def _():
        issue_from(hp_ref, i * tt)

    @pl.when(i >= prompt_steps)
    def _():
        issue_from(hs_ref, (i - prompt_steps) * tt)

    _drain_one_step_late(i, n, sems, lambda s: _row_copy(hp_ref.at[pl.ds(0, 1)], xs_ref.at[pl.ds(0, 1)], s).wait(),
                         tt * TOP_K)


def _dispatch(h2_p, h2_s, dest, pends, padded, n_used, rows, tt, bm):
    d = h2_p.shape[1]
    t_all = h2_p.shape[0] + h2_s.shape[0]
    n_experts = pends.shape[0]
    hbm = pl.BlockSpec(memory_space=pl.ANY)
    grid_spec = pltpu.PrefetchScalarGridSpec(
        num_scalar_prefetch=3,
        grid=(t_all // tt,),
        in_specs=[pl.BlockSpec((1, 1, tt * TOP_K), lambda i, *_: (i, 0, 0), memory_space=pltpu.SMEM), hbm, hbm, hbm],
        out_specs=hbm,
        scratch_shapes=[pltpu.SemaphoreType.DMA((2,)), pltpu.SemaphoreType.DMA(())],
    )
    return pl.pallas_call(
        functools.partial(_dispatch_body, tt=tt, bm=bm, n_experts=n_experts),
        grid_spec=grid_spec,
        out_shape=jax.ShapeDtypeStruct((rows, d), _F32),
        compiler_params=_params(("arbitrary",), 16),
        name="moe_dispatch",
    )(pends, padded, n_used, dest.reshape(t_all // tt, 1, tt * TOP_K), h2_p, h2_s, jnp.zeros((bm, d), _F32))


def _return_body(dest_ref, out_ref, og_ref, sems, *, tt):
    i = pl.program_id(0)
    n = pl.num_programs(0)

    def issue(r, carry):
        for k in range(TOP_K):
            _row_copy(out_ref.at[pl.ds(dest_ref[0, 0, r * TOP_K + k], 1)], og_ref.at[k, pl.ds(i * tt + r, 1)],
                      sems.at[i % 2]).start()
        return carry

    lax.fori_loop(0, tt, issue, 0)
    _drain_one_step_late(i, n, sems, lambda s: _row_copy(out_ref.at[pl.ds(0, 1)], og_ref.at[0, pl.ds(0, 1)], s).wait(),
                         tt * TOP_K)


def _moe_return(out_sorted, dest, t_all, tt):
    d = out_sorted.shape[1]
    return pl.pallas_call(
        functools.partial(_return_body, tt=tt),
        grid=(t_all // tt,),
        in_specs=[
            pl.BlockSpec((1, 1, tt * TOP_K), lambda i: (i, 0, 0), memory_space=pltpu.SMEM),
            pl.BlockSpec(memory_space=pl.ANY),
        ],
        out_specs=pl.BlockSpec(memory_space=pl.ANY),
        out_shape=jax.ShapeDtypeStruct((TOP_K, t_all, d), _F32),
        scratch_shapes=[pltpu.SemaphoreType.DMA((2,))],
        compiler_params=_params(("arbitrary",), 16),
        name="moe_return",
    )(dest.reshape(t_all // tt, 1, tt * TOP_K), out_sorted)


def _ffn_body(be_ref, nu_ref, xs_ref, wg_ref, wu_ref, bg_ref, bu_ref, wd_ref, bd_ref, o_ref, xb_ref, acc_ref):
    del be_ref
    f = pl.program_id(1)
    nf = pl.num_programs(1)

    @pl.when(pl.program_id(0) < nu_ref[0])
    def _():
        @pl.when(f == 0)
        def _():
            xb_ref[...] = xs_ref[...].astype(_BF16)

        xb = xb_ref[...]
        g = jnp.minimum(_dot(xb, wg_ref[...]) + bg_ref[...], SWIGLU_LIMIT)
        u = jnp.clip(_dot(xb, wu_ref[...]) + bu_ref[...], -SWIGLU_LIMIT, SWIGLU_LIMIT)
        a = g * jax.nn.sigmoid(SWIGLU_ALPHA * g) * (u + 1.0)
        part = _dot(a.astype(_BF16), wd_ref[...])

        @pl.when(f == 0)
        def _():
            acc_ref[...] = part

        @pl.when(f > 0)
        def _():
            acc_ref[...] += part

        @pl.when(f == nf - 1)
        def _():
            o_ref[...] = acc_ref[...] + bd_ref[...]

    @pl.when((pl.program_id(0) >= nu_ref[0]) & (f == nf - 1))
    def _():
        o_ref[...] = jnp.zeros(o_ref.shape, o_ref.dtype)


def _expert_ffn(xs, block_e, n_used, w_gu_b, b_gu, w_down_b, b_down, bm, tf):
    rows, d = xs.shape
    n_experts, _, two_ff = w_gu_b.shape
    d_ff = two_ff // 2
    nf = d_ff // tf
    n_blk = rows // bm

    def blk(i, nu):
        return jnp.minimum(i, nu[0] - 1)

    def fidx(i, f, nu):
        return jnp.where(i < nu[0], f, nf - 1)

    grid_spec = pltpu.PrefetchScalarGridSpec(
        num_scalar_prefetch=2,
        grid=(n_blk, nf),
        in_specs=[
            pl.BlockSpec((bm, d), lambda i, f, be, nu: (blk(i, nu), 0)),
            pl.BlockSpec((None, d, tf), lambda i, f, be, nu: (be[blk(i, nu)], 0, fidx(i, f, nu))),
            pl.BlockSpec((None, d, tf), lambda i, f, be, nu: (be[blk(i, nu)], 0, nf + fidx(i, f, nu))),
            pl.BlockSpec((None, 1, tf), lambda i, f, be, nu: (be[blk(i, nu)], 0, fidx(i, f, nu))),
            pl.BlockSpec((None, 1, tf), lambda i, f, be, nu: (be[blk(i, nu)], 0, nf + fidx(i, f, nu))),
            pl.BlockSpec((None, tf, d), lambda i, f, be, nu: (be[blk(i, nu)], fidx(i, f, nu), 0)),
            pl.BlockSpec((None, 1, d), lambda i, f, be, nu: (be[blk(i, nu)], 0, 0)),
        ],
        out_specs=pl.BlockSpec((bm, d), lambda i, f, be, nu: (i, 0)),
        scratch_shapes=[pltpu.VMEM((bm, d), _BF16), pltpu.VMEM((bm, d), _F32)],
    )
    return pl.pallas_call(
        _ffn_body,
        grid_spec=grid_spec,
        out_shape=jax.ShapeDtypeStruct((rows, d), _F32),
        compiler_params=_params(("arbitrary", "arbitrary"), 56),
        name="expert_ffn",
    )(block_e, n_used, xs, w_gu_b, w_gu_b, b_gu.reshape(n_experts, 1, two_ff), b_gu.reshape(n_experts, 1, two_ff),
      w_down_b, b_down.reshape(n_experts, 1, d))


def _combine_body(x1_ref, og_ref, rec_ref, mod_ref, gf_ref, y_ref, *, final_norm):
    rec = rec_ref[...]
    moe = og_ref[0] * rec[:, REC_GATE:REC_GATE + 1]
    for k in range(1, TOP_K):
        moe = moe + og_ref[k] * rec[:, REC_GATE + k:REC_GATE + k + 1]
    x2 = x1_ref[...] + mod_ref[5, 0] * moe
    if final_norm:
        ms = jnp.mean(x2 * x2, axis=-1, keepdims=True)
        x2 = x2 * lax.rsqrt(ms + EPS) * gf_ref[...]
    y_ref[...] = x2


def _combine(x1, og, rec, mods4, g_final, row0, tm, final_norm):
    t, d = x1.shape
    n_tiles = t // tm
    tiles_per_group = n_tiles // mods4.shape[1]
    blk0 = row0 // tm
    return pl.pallas_call(
        functools.partial(_combine_body, final_norm=final_norm),
        grid=(n_tiles,),
        in_specs=[
            pl.BlockSpec((tm, d), lambda i: (i, 0)),
            pl.BlockSpec((TOP_K, tm, d), lambda i: (0, blk0 + i, 0)),
            pl.BlockSpec((tm, LANES), lambda i: (i, 0)),
            _mod_spec(mods4, tiles_per_group),
            pl.BlockSpec((1, d), lambda i: (0, 0)),
        ],
        out_specs=pl.BlockSpec((tm, d), lambda i: (i, 0)),
        out_shape=jax.ShapeDtypeStruct((t, d), _F32),
        compiler_params=_params(("arbitrary",), 40),
        name="combine_final_norm",
    )(x1, og, rec, mods4, g_final.reshape(1, d))


def _tile(n, pref):
    t = pref
    while n % t:
        t //= 2
    return t


def _moe(h2_p, h2_s, rec_all, counts, w_gu_b, b_gu, w_down_b, b_down, tt, bm, tf):
    t_all = rec_all.shape[0]
    n_experts = w_gu_b.shape[0]
    n_blk = -(-(t_all * TOP_K) // bm) + n_experts
    cnt = counts[0, :n_experts].astype(_I32)
    padded = (cnt + bm - 1) // bm * bm
    pends = jnp.cumsum(padded).astype(_I32)
    pstarts = pends - padded
    idx = rec_all[:, REC_IDX:REC_IDX + TOP_K].astype(_I32)
    rank = rec_all[:, REC_RANK:REC_RANK + TOP_K].astype(_I32)
    dest = pstarts[idx] + rank
    block_e = jnp.minimum(jnp.searchsorted(pends, jnp.arange(n_blk, dtype=_I32) * bm, side="right"),
                          n_experts - 1).astype(_I32)
    n_used = (pends[-1:] // bm).astype(_I32)
    xs = _dispatch(h2_p, h2_s, dest, pends, padded, n_used, n_blk * bm, tt, bm)
    out_sorted = _expert_ffn(xs, block_e, n_used, w_gu_b, b_gu, w_down_b, b_down, bm, tf)
    return _moe_return(out_sorted, dest, t_all, tt)


def kernel(x_prompt, x_sample, c_prompt, c_sample, cache_k, cache_v, state_conv, w_ada, b_ada, g_mix, w_in, attn_sink, conv_w, conv_b, conv_ln_g, conv_ln_b, w_out, g_ffn, w_router, b_router, w_gu, b_gu, w_down, b_down, g_final):
    bp, sp, d = x_prompt.shape
    bs, ss, _ = x_sample.shape
    depth = w_ada.shape[0]
    n_experts = w_router.shape[-1]
    tp, ts = bp * sp, bs * ss
    t_all = tp + ts

    tm = _tile(sp, 256)
    tq = _tile(sp, 256)
    tc = _tile(sp, 256)
    tt = _tile(ts, 128)
    bm = 512 if tp >= 8192 else 64
    tf = 512
    assert tp % ts == 0 and tp % tt == 0 and sp % CHUNK == 0 and tq % WINDOW == 0 and tc % CONV_HALO == 0

    xp = x_prompt.reshape(tp, d)
    xsm = x_sample.reshape(ts, d)
    tabs_p = _rope_tables(jnp.arange(sp, dtype=_I32))
    tabs_s = _rope_tables(jnp.tile(PAST_LEN + jnp.arange(ss, dtype=_I32), bs))

    new_kp, new_vp, new_cp, new_ks, new_vs, new_cs = [], [], [], [], [], []
    for l in range(depth):
        w_in_b = w_in[l].astype(_BF16)
        wo_a = w_out[l, :ATTN_WIDTH].astype(_BF16)
        wo_c = w_out[l, ATTN_WIDTH:].astype(_BF16)
        w_gu_b = w_gu[l].astype(_BF16)
        w_down_b = w_down[l].astype(_BF16)
        wr_pad = jnp.pad(w_router[l], ((0, 0), (0, LANES - n_experts)))
        br_pad = jnp.pad(b_router[l], (0, LANES - n_experts)).reshape(1, LANES)

        mods = _modulation(jnp.concatenate([c_prompt, c_sample], axis=0), w_ada[l], b_ada[l])
        mods_p = mods[:bp].reshape(bp, 6, 1, d).transpose(1, 0, 2, 3)
        mods_s = jnp.repeat(mods[bp:].reshape(bs, 6, d), ss, axis=0).transpose(1, 0, 2).reshape(6, 1, ts, d)

        qp, kp, vp, up = _in_proj(xp, mods_p, g_mix[l], w_in_b, tabs_p, tm)
        qs, ks, vs, us = _in_proj(xsm, mods_s, g_mix[l], w_in_b, tabs_s, ts)

        att_p = _band_attention(qp, kp, vp, attn_sink[l], bp, sp, tq)
        win = cache_k.shape[2]
        att_s = _cached_attention(qs, ks, vs, cache_k[l].reshape(bs, win, KV_COLS), cache_v[l].reshape(bs, win, KV_COLS),
                                  attn_sink[l], bs, ss)

        cv_p = _conv_module(up, None, conv_w[l], conv_b[l], conv_ln_g[l], conv_ln_b[l], bp, sp, tc)
        hist = jnp.pad(state_conv[l], ((0, 0), (CONV_HALO - (CONV_WIDTH - 1), 0), (0, 0))).reshape(bs * CONV_HALO, CONV_CH)
        cv_s = _conv_module(us, hist, conv_w[l], conv_b[l], conv_ln_g[l], conv_ln_b[l], bs, ss, ss)

        zero_counts = jnp.zeros((1, LANES), _F32)
        x1p, h2p, rec_p, counts = _out_proj(xp, att_p, cv_p, mods_p, g_ffn[l], wo_a, wo_c, wr_pad, br_pad,
                                            zero_counts, tm, n_experts)
        x1s, h2s, rec_s, counts = _out_proj(xsm, att_s, cv_s, mods_s, g_ffn[l], wo_a, wo_c, wr_pad, br_pad,
                                            counts, ts, n_experts)

        og = _moe(h2p, h2s, jnp.concatenate([rec_p, rec_s], axis=0), counts, w_gu_b, b_gu[l], w_down_b, b_down[l],
                  tt, bm, tf)

        last = l == depth - 1
        xp = _combine(x1p, og, rec_p, mods_p, g_final, 0, tm, last)
        xsm = _combine(x1s, og, rec_s, mods_s, g_final, tp, ts, last)

        keep = min(WINDOW, sp)
        new_kp.append(kp.reshape(bp, sp, N_KV_HEADS, HEAD_DIM)[:, -keep:])
        new_vp.append(vp.reshape(bp, sp, N_KV_HEADS, HEAD_DIM)[:, -keep:])
        new_cp.append(up.reshape(bp, sp, CONV_CH)[:, -(CONV_WIDTH - 1):])
        new_ks.append(ks.reshape(bs, ss, N_KV_HEADS, HEAD_DIM))
        new_vs.append(vs.reshape(bs, ss, N_KV_HEADS, HEAD_DIM))
        new_cs.append(jnp.concatenate([state_conv[l], us.reshape(bs, ss, CONV_CH)], axis=1)[:, -(CONV_WIDTH - 1):])

    return (xp.reshape(bp, sp, d), xsm.reshape(bs, ss, d), jnp.stack(new_kp), jnp.stack(new_vp), jnp.stack(new_cp),
            jnp.stack(new_ks), jnp.stack(new_vs), jnp.stack(new_cs))
```

```python
import functools

import jax
import jax.numpy as jnp
from jax import lax
from jax.experimental import pallas as pl
from jax.experimental.pallas import tpu as pltpu

D_MODEL = 2048
CHUNK = 64
HEAD_DIM = 64
ATTN_WIDTH = D_MODEL // 2
CONV_CH = D_MODEL - ATTN_WIDTH
N_HEADS = ATTN_WIDTH // HEAD_DIM
N_KV_HEADS = N_HEADS // 4
GQA_GROUP = N_HEADS // N_KV_HEADS
ROT_DIM = HEAD_DIM // 4
ROPE_THETA = 500000.0
WINDOW = 128
WINDOW_CHUNKS = WINDOW // CHUNK
CONV_WIDTH = 31
TOP_K = 4
SWIGLU_LIMIT = 7.0
SWIGLU_ALPHA = 1.702
EPS = 1e-5
NEG_INF = -1e30
PAST_LEN = 2048
Q_COLS = N_HEADS * HEAD_DIM
KV_COLS = N_KV_HEADS * HEAD_DIM
IN_COLS = Q_COLS + 2 * KV_COLS + 2 * CONV_CH

LANES = 128
CONV_HALO = 32
MIB = 1024 * 1024

_BF16 = jnp.bfloat16
_F32 = jnp.float32
_I32 = jnp.int32
_U32 = jnp.uint32


def _params(semantics, vmem_mib):
    return pltpu.CompilerParams(dimension_semantics=semantics, vmem_limit_bytes=vmem_mib * MIB)


def _dot(a, b):
    return jnp.dot(a, b, preferred_element_type=_F32)


def _pack_halves(x):
    w = x.shape[1] // 2
    lo = lax.bitcast_convert_type(x[:, :w].astype(_BF16).astype(_F32), _U32)
    hi = lax.bitcast_convert_type(x[:, w:].astype(_BF16).astype(_F32), _U32)
    return (lo >> 16) | (hi & jnp.uint32(0xFFFF0000))


def _unpack_halves(p):
    return (lax.bitcast_convert_type(p << 16, _F32), lax.bitcast_convert_type(p & jnp.uint32(0xFFFF0000), _F32))


def _rmsnorm_mod(x, g, scale, shift):
    ms = jnp.mean(x * x, axis=-1, keepdims=True)
    return (x * lax.rsqrt(ms + EPS) * g) * (1.0 + scale) + shift


def _ada_body(c_ref, w_ref, b_ref, o_ref):
    c = c_ref[...]
    s = (c * jax.nn.sigmoid(c)).astype(_BF16)
    o_ref[...] = _dot(s, w_ref[...].astype(_BF16)) + b_ref[...]


def _modulation(c, w_ada, b_ada):
    n, d = c.shape
    cols = w_ada.shape[1]
    tn = 1024
    return pl.pallas_call(
        _ada_body,
        grid=(cols // tn,),
        in_specs=[
            pl.BlockSpec((n, d), lambda j: (0, 0)),
            pl.BlockSpec((d, tn), lambda j: (0, j)),
            pl.BlockSpec((1, tn), lambda j: (0, j)),
        ],
        out_specs=pl.BlockSpec((n, tn), lambda j: (0, j)),
        out_shape=jax.ShapeDtypeStruct((n, cols), _F32),
        compiler_params=_params(("arbitrary",), 40),
        name="modulation",
    )(c, w_ada, b_ada.reshape(1, cols))


def _rope_tables(pos):
    half = ROT_DIM // 2
    inv_freq = ROPE_THETA ** (-jnp.arange(0, ROT_DIM, 2, dtype=_F32) / ROT_DIM)
    ang = pos.astype(_F32)[:, None] * inv_freq[None, :]
    cos, sin = jnp.cos(ang), jnp.sin(ang)
    d = jnp.arange(LANES) % HEAD_DIM
    cos_l = jnp.where(d < ROT_DIM, cos[:, d % half], 1.0)
    sa = jnp.where(d < half, -sin[:, d % half], 0.0)
    sb = jnp.where((d >= half) & (d < ROT_DIM), sin[:, d % half], 0.0)
    return cos_l, sa, sb


def _rotate(z, cos, sa, sb):
    half = ROT_DIM // 2
    parts = []
    for j in range(z.shape[1] // LANES):
        zj = z[:, j * LANES:(j + 1) * LANES]
        parts.append(zj * cos + pltpu.roll(zj, LANES - half, 1) * sa + pltpu.roll(zj, half, 1) * sb)
    return parts[0] if len(parts) == 1 else jnp.concatenate(parts, axis=1)


def _in_proj_body(x_ref, mod_ref, g_ref, w_ref, cos_ref, sa_ref, sb_ref, q_ref, k_ref, v_ref, u_ref):
    h = _rmsnorm_mod(x_ref[...], g_ref[...], mod_ref[1, 0], mod_ref[0, 0]).astype(_BF16)
    cos, sa, sb = cos_ref[...], sa_ref[...], sb_ref[...]
    cw = 512
    for c in range(Q_COLS // cw):
        z = _dot(h, w_ref[:, c * cw:(c + 1) * cw])
        q_ref[:, c * cw:(c + 1) * cw] = _rotate(z, cos, sa, sb).astype(_BF16)
    z = _dot(h, w_ref[:, Q_COLS:Q_COLS + 2 * KV_COLS])
    k_ref[...] = _rotate(z[:, :KV_COLS], cos, sa, sb)
    v_ref[...] = z[:, KV_COLS:]
    o = Q_COLS + 2 * KV_COLS
    for c in range(CONV_CH // cw):
        zv = _dot(h, w_ref[:, o + c * cw:o + (c + 1) * cw])
        zg = _dot(h, w_ref[:, o + CONV_CH + c * cw:o + CONV_CH + (c + 1) * cw])
        u_ref[:, c * cw:(c + 1) * cw] = zv * jax.nn.sigmoid(zg)


def _mod_spec(mods4, tiles_per_group):
    _, _, r, d = mods4.shape
    return pl.BlockSpec((6, 1, r, d), lambda i: (0, i // tiles_per_group, 0, 0))


def _in_proj(x2d, mods4, g_mix, w_in_b, tabs, tm):
    t, d = x2d.shape
    n_tiles = t // tm
    tiles_per_group = n_tiles // mods4.shape[1]
    pos_tiles = tabs[0].shape[0] // tm
    tab_spec = pl.BlockSpec((tm, LANES), lambda i: (i % pos_tiles, 0))
    row = lambda w: pl.BlockSpec((tm, w), lambda i: (i, 0))
    return pl.pallas_call(
        _in_proj_body,
        grid=(n_tiles,),
        in_specs=[
            row(d),
            _mod_spec(mods4, tiles_per_group),
            pl.BlockSpec((1, d), lambda i: (0, 0)),
            pl.BlockSpec((d, IN_COLS), lambda i: (0, 0)),
            tab_spec, tab_spec, tab_spec,
        ],
        out_specs=[row(Q_COLS), row(KV_COLS), row(KV_COLS), row(CONV_CH)],
        out_shape=[
            jax.ShapeDtypeStruct((t, Q_COLS), _BF16),
            jax.ShapeDtypeStruct((t, KV_COLS), _F32),
            jax.ShapeDtypeStruct((t, KV_COLS), _F32),
            jax.ShapeDtypeStruct((t, CONV_CH), _F32),
        ],
        compiler_params=_params(("arbitrary",), 56),
        name="in_proj",
    )(x2d, mods4, g_mix.reshape(1, d), w_in_b, *tabs)


def _attn_group(qg, kk, vv, sink_col, valid):
    s = lax.dot_general(qg, kk, (((1,), (1,)), ((), ())), preferred_element_type=_F32) * (HEAD_DIM ** -0.5)
    if valid is not None:
        s = jnp.where(valid, s, NEG_INF)
    m = jnp.maximum(jnp.max(s, axis=-1, keepdims=True), sink_col)
    p = jnp.exp(s - m)
    den = jnp.sum(p, axis=-1, keepdims=True) + jnp.exp(sink_col - m)
    return _dot(p.astype(_BF16), vv) / den


def _sink_column(sink_ref, kh, rows_per_head):
    row = lax.broadcasted_iota(_I32, (GQA_GROUP * rows_per_head, 1), 0)
    col = jnp.full(row.shape, sink_ref[kh * GQA_GROUP + GQA_GROUP - 1], _F32)
    for g in range(GQA_GROUP - 2, -1, -1):
        col = jnp.where(row < (g + 1) * rows_per_head, sink_ref[kh * GQA_GROUP + g], col)
    return col


def _store_heads(o_ref, r0, rows, kh, o):
    for pair in range(GQA_GROUP // 2):
        both = jnp.concatenate([o[(2 * pair) * rows:(2 * pair + 1) * rows],
                                o[(2 * pair + 1) * rows:(2 * pair + 2) * rows]], axis=1)
        c0 = (kh * GQA_GROUP + 2 * pair) * HEAD_DIM
        o_ref[r0:r0 + rows, c0:c0 + 2 * HEAD_DIM] = both.astype(o_ref.dtype)


def _stack_heads(q, r0, rows, kh):
    return jnp.concatenate(
        [q[r0:r0 + rows, (kh * GQA_GROUP + g) * HEAD_DIM:(kh * GQA_GROUP + g + 1) * HEAD_DIM] for g in range(GQA_GROUP)],
        axis=0)


def _band_attn_body(sink_ref, q_ref, kc_ref, kh_ref, vc_ref, vh_ref, o_ref, *, tq):
    j = pl.program_id(1)
    q = q_ref[...]
    kall = jnp.concatenate([kh_ref[...], kc_ref[...]], axis=0)
    vall = jnp.concatenate([vh_ref[...], vc_ref[...]], axis=0)
    band = (WINDOW_CHUNKS + 1) * CHUNK
    key_chunk = lax.broadcasted_iota(_I32, (1, band), 1) // CHUNK
    for kh in range(N_KV_HEADS):
        sink_col = _sink_column(sink_ref, kh, CHUNK)
        kk = kall[:, kh * HEAD_DIM:(kh + 1) * HEAD_DIM].astype(_BF16)
        vv = vall[:, kh * HEAD_DIM:(kh + 1) * HEAD_DIM].astype(_BF16)
        for ci in range(tq // CHUNK):
            qg = _stack_heads(q, ci * CHUNK, CHUNK, kh)
            valid = None if ci >= WINDOW_CHUNKS else (j * (tq // CHUNK) + ci - WINDOW_CHUNKS + key_chunk) >= 0
            o = _attn_group(qg, kk[ci * CHUNK:ci * CHUNK + band], vv[ci * CHUNK:ci * CHUNK + band], sink_col, valid)
            _store_heads(o_ref, ci * CHUNK, CHUNK, kh, o)


def _band_attention(q, k, v, sink, batch, seq, tq):
    nq = seq // tq
    r = tq // WINDOW
    cur = lambda w: pl.BlockSpec((tq, w), lambda b, j: (b * nq + j, 0))
    halo = pl.BlockSpec((WINDOW, KV_COLS), lambda b, j: (b * nq * r + jnp.maximum(j * r - 1, 0), 0))
    return pl.pallas_call(
        functools.partial(_band_attn_body, tq=tq),
        grid=(batch, nq),
        in_specs=[pl.BlockSpec(memory_space=pltpu.SMEM), cur(Q_COLS), cur(KV_COLS), halo, cur(KV_COLS), halo],
        out_specs=cur(Q_COLS),
        out_shape=jax.ShapeDtypeStruct((batch * seq, Q_COLS), _BF16),
        compiler_params=_params(("arbitrary", "arbitrary"), 40),
        name="band_attention",
    )(sink, q, k, k, v, v)


def _cached_attn_body(sink_ref, q_ref, kn_ref, kc_ref, vn_ref, vc_ref, o_ref, *, rows):
    q = q_ref[...]
    kall = jnp.concatenate([kc_ref[0], kn_ref[...]], axis=0)
    vall = jnp.concatenate([vc_ref[0], vn_ref[...]], axis=0)
    for kh in range(N_KV_HEADS):
        o = _attn_group(_stack_heads(q, 0, rows, kh), kall[:, kh * HEAD_DIM:(kh + 1) * HEAD_DIM].astype(_BF16),
                        vall[:, kh * HEAD_DIM:(kh + 1) * HEAD_DIM].astype(_BF16), _sink_column(sink_ref, kh, rows), None)
        _store_heads(o_ref, 0, rows, kh, o)


def _cached_attention(q, k, v, cache_k, cache_v, sink, batch, rows):
    win = cache_k.shape[1]
    new = lambda w: pl.BlockSpec((rows, w), lambda b: (b, 0))
    cache = pl.BlockSpec((1, win, KV_COLS), lambda b: (b, 0, 0))
    return pl.pallas_call(
        functools.partial(_cached_attn_body, rows=rows),
        grid=(batch,),
        in_specs=[pl.BlockSpec(memory_space=pltpu.SMEM), new(Q_COLS), new(KV_COLS), cache, new(KV_COLS), cache],
        out_specs=new(Q_COLS),
        out_shape=jax.ShapeDtypeStruct((batch * rows, Q_COLS), _BF16),
        compiler_params=_params(("arbitrary",), 40),
        name="cached_attention",
    )(sink, q, k, cache_k, v, cache_v)


def _conv_body(uc_ref, uh_ref, w_ref, b_ref, g_ref, beta_ref, o_ref, win_ref, y_ref, *, tt, zero_first_halo):
    halo = uh_ref[...]
    if zero_first_halo:
        halo = jnp.where(pl.program_id(1) == 0, 0.0, halo)
    win_ref[0:CONV_HALO, :] = halo
    win_ref[CONV_HALO:, :] = uc_ref[...]
    lead = CONV_HALO - (CONV_WIDTH - 1)
    for s in range(CONV_CH // LANES):
        cs = slice(s * LANES, (s + 1) * LANES)
        acc = w_ref[0:1, cs] * win_ref[lead:lead + tt, cs]
        for tap in range(1, CONV_WIDTH):
            acc = acc + w_ref[tap:tap + 1, cs] * win_ref[lead + tap:lead + tap + tt, cs]
        y_ref[:, cs] = acc + b_ref[:, cs]
    y = y_ref[...]
    mu = jnp.mean(y, axis=-1, keepdims=True)
    var = jnp.mean(jnp.square(y - mu), axis=-1, keepdims=True)
    z = (y - mu) * lax.rsqrt(var + EPS) * g_ref[...] + beta_ref[...]
    o_ref[...] = (z * jax.nn.sigmoid(z)).astype(o_ref.dtype)


def _conv_module(u, hist, conv_w, conv_b, ln_g, ln_b, batch, seq, tt):
    nt = seq // tt
    r = tt // CONV_HALO
    cur = pl.BlockSpec((tt, CONV_CH), lambda b, j: (b * nt + j, 0))
    if hist is None:
        hist_arr = u
        halo = pl.BlockSpec((CONV_HALO, CONV_CH), lambda b, j: (b * nt * r + jnp.maximum(j * r - 1, 0), 0))
    else:
        assert nt == 1
        hist_arr = hist
        halo = pl.BlockSpec((CONV_HALO, CONV_CH), lambda b, j: (b, 0))
    vec = pl.BlockSpec((1, CONV_CH), lambda b, j: (0, 0))
    wpad = jnp.pad(conv_w.reshape(CONV_WIDTH, CONV_CH), ((0, 1), (0, 0)))
    return pl.pallas_call(
        functools.partial(_conv_body, tt=tt, zero_first_halo=hist is None),
        grid=(batch, nt),
        in_specs=[cur, halo, pl.BlockSpec((CONV_WIDTH + 1, CONV_CH), lambda b, j: (0, 0)), vec, vec, vec],
        out_specs=cur,
        out_shape=jax.ShapeDtypeStruct((batch * seq, CONV_CH), _BF16),
        scratch_shapes=[pltpu.VMEM((tt + CONV_HALO, CONV_CH), _F32), pltpu.VMEM((tt, CONV_CH), _F32)],
        compiler_params=_params(("arbitrary", "arbitrary"), 40),
        name="conv_module",
    )(u, hist_arr, wpad, conv_b.reshape(1, -1), ln_g.reshape(1, -1), ln_b.reshape(1, -1))


REC_IDX, REC_GATE, REC_RANK = 0, TOP_K, 2 * TOP_K


def _out_proj_body(x_ref, a_ref, c_ref, mod_ref, g_ref, wa_ref, wc_ref, wrh_ref, wrl_ref, br_ref, cnt_ref,
                   x1_ref, h2_ref, rec_ref, cnt_out_ref, run_ref, *, n_experts):
    tm = x_ref.shape[0]

    @pl.when(pl.program_id(0) == 0)
    def _():
        run_ref[...] = cnt_ref[...]

    o = _dot(a_ref[...], wa_ref[...]) + _dot(c_ref[...], wc_ref[...])
    x1 = x_ref[...] + mod_ref[2, 0] * o
    x1_ref[...] = x1
    h2 = _rmsnorm_mod(x1, g_ref[...], mod_ref[4, 0], mod_ref[3, 0])
    h2_ref[...] = _pack_halves(h2)

    h2_hi = h2.astype(_BF16)
    h2_lo = (h2 - h2_hi.astype(_F32)).astype(_BF16)
    logits = _dot(h2_hi, wrh_ref[...]) + (_dot(h2_lo, wrh_ref[...]) + _dot(h2_hi, wrl_ref[...])) + br_ref[...]
    lane = lax.broadcasted_iota(_I32, (tm, LANES), 1)
    work = jnp.where(lane < n_experts, logits, -jnp.inf)
    vals, hots = [], []
    rec = jnp.zeros((tm, LANES), _F32)
    for k in range(TOP_K):
        m = jnp.max(work, axis=-1, keepdims=True)
        idx = jnp.min(jnp.where(work == m, lane, LANES), axis=-1, keepdims=True)
        hot = lane == idx
        work = jnp.where(hot, -jnp.inf, work)
        vals.append(m)
        hots.append(hot)
        rec = jnp.where(lane == REC_IDX + k, idx.astype(_F32), rec)
    exps = [jnp.exp(v - vals[0]) for v in vals]
    den = exps[0]
    for e in exps[1:]:
        den = den + e
    for k in range(TOP_K):
        rec = jnp.where(lane == REC_GATE + k, exps[k] / den, rec)

    chosen = jnp.zeros((tm, LANES), _F32)
    for hot in hots:
        chosen = jnp.where(hot, 1.0, chosen)
    r_i = lax.broadcasted_iota(_I32, (tm, tm), 0)
    c_i = lax.broadcasted_iota(_I32, (tm, tm), 1)
    before = _dot(jnp.where(c_i < r_i, 1.0, 0.0).astype(_BF16), chosen.astype(_BF16)) + run_ref[...]
    for k in range(TOP_K):
        rank = jnp.sum(jnp.where(hots[k], before, 0.0), axis=-1, keepdims=True)
        rec = jnp.where(lane == REC_RANK + k, rank, rec)
    rec_ref[...] = rec
    run_ref[...] = run_ref[...] + jnp.sum(chosen, axis=0, keepdims=True)
    cnt_out_ref[...] = run_ref[...]


def _out_proj(x2d, attn, conv, mods4, g_ffn, wo_a, wo_c, wr_hi, wr_lo, br_pad, counts, tm, n_experts):
    t, d = x2d.shape
    n_tiles = t // tm
    tiles_per_group = n_tiles // mods4.shape[1]
    row = lambda w: pl.BlockSpec((tm, w), lambda i: (i, 0))
    const = lambda a, b: pl.BlockSpec((a, b), lambda i: (0, 0))
    return pl.pallas_call(
        functools.partial(_out_proj_body, n_experts=n_experts),
        grid=(n_tiles,),
        in_specs=[row(d), row(ATTN_WIDTH), row(CONV_CH), _mod_spec(mods4, tiles_per_group), const(1, d),
                  const(ATTN_WIDTH, d), const(CONV_CH, d), const(d, LANES), const(d, LANES), const(1, LANES),
                  const(1, LANES)],
        out_specs=[row(d), row(d // 2), row(LANES), const(1, LANES)],
        out_shape=[
            jax.ShapeDtypeStruct((t, d), _F32),
            jax.ShapeDtypeStruct((t, d // 2), _U32),
            jax.ShapeDtypeStruct((t, LANES), _F32),
            jax.ShapeDtypeStruct((1, LANES), _F32),
        ],
        scratch_shapes=[pltpu.VMEM((1, LANES), _F32)],
        compiler_params=_params(("arbitrary",), 56),
        name="out_proj_router",
    )(x2d, attn, conv, mods4, g_ffn.reshape(1, d), wo_a, wo_c, wr_hi, wr_lo, br_pad, counts)


def _row_copy(src, dst, sem):
    return pltpu.make_async_copy(src, dst, sem)


def _dispatch_body(pends_ref, padded_ref, nu_ref, dest_ref, hp_ref, hs_ref, xs_ref, stage_ref, zeros_ref, sems, zsem, *,
                   tt, bm, n_experts, prompt_steps):
    i = pl.program_id(0)
    n = pl.num_programs(0)
    n_blk = xs_ref.shape[0] // bm
    slot = i % 2

    def zero_block(row0):
        return _row_copy(zeros_ref, xs_ref.at[pl.ds(pl.multiple_of(row0, bm), bm)], zsem)

    @pl.when(i == 0)
    def _():
        zeros_ref[...] = jnp.zeros(zeros_ref.shape, zeros_ref.dtype)
        for e in range(n_experts):
            @pl.when(padded_ref[e] > 0)
            def _():
                zero_block(pends_ref[e] - bm).start()

        def tail_start(b, carry):
            zero_block(b * bm).start()
            return carry

        def tail_wait(b, carry):
            zero_block(b * bm).wait()
            return carry

        lax.fori_loop(nu_ref[0], n_blk, tail_start, 0)
        for e in range(n_experts):
            @pl.when(padded_ref[e] > 0)
            def _():
                zero_block(pends_ref[e] - bm).wait()
        lax.fori_loop(nu_ref[0], n_blk, tail_wait, 0)

    @pl.when(i < prompt_steps)
    def _():
        stage_ref[slot] = hp_ref[...]

    @pl.when(i >= prompt_steps)
    def _():
        stage_ref[slot] = hs_ref[...]

    def issue(r, carry):
        src = stage_ref.at[slot, pl.ds(r, 1)]
        for k in range(TOP_K):
            _row_copy(src, xs_ref.at[pl.ds(dest_ref[0, 0, r * TOP_K + k], 1)], sems.at[slot]).start()
        return carry

    lax.fori_loop(0, tt, issue, 0, unroll=8)

    def drain(s):
        for _ in range(TOP_K):
            _row_copy(stage_ref.at[s], xs_ref.at[pl.ds(0, tt)], sems.at[s]).wait()

    @pl.when(i > 0)
    def _():
        drain(1 - slot)

    @pl.when(i == n - 1)
    def _():
        drain(slot)


def _dispatch(h2_p, h2_s, dest, pends, padded, n_used, rows, tt, bm):
    w = h2_p.shape[1]
    tp, ts = h2_p.shape[0], h2_s.shape[0]
    n_experts = pends.shape[0]
    prompt_steps = tp // tt
    grid_spec = pltpu.PrefetchScalarGridSpec(
        num_scalar_prefetch=3,
        grid=((tp + ts) // tt,),
        in_specs=[
            pl.BlockSpec((1, 1, tt * TOP_K), lambda i, *_: (i, 0, 0), memory_space=pltpu.SMEM),
            pl.BlockSpec((tt, w), lambda i, *_: (jnp.minimum(i, prompt_steps - 1), 0)),
            pl.BlockSpec((tt, w), lambda i, *_: (jnp.maximum(i - prompt_steps, 0), 0)),
        ],
        out_specs=pl.BlockSpec(memory_space=pl.ANY),
        scratch_shapes=[pltpu.VMEM((2, tt, w), _U32), pltpu.VMEM((bm, w), _U32),
                        pltpu.SemaphoreType.DMA((2,)), pltpu.SemaphoreType.DMA(())],
    )
    return pl.pallas_call(
        functools.partial(_dispatch_body, tt=tt, bm=bm, n_experts=n_experts, prompt_steps=prompt_steps),
        grid_spec=grid_spec,
        out_shape=jax.ShapeDtypeStruct((rows, w), _U32),
        compiler_params=_params(("arbitrary",), 24),
        name="moe_dispatch",
    )(pends, padded, n_used, dest.reshape((tp + ts) // tt, 1, tt * TOP_K), h2_p, h2_s)


def _ffn_body(be_ref, nu_ref, xs_ref, wg_ref, wu_ref, bg_ref, bu_ref, wd_ref, bd_ref, o_ref, xb_ref, acc_ref):
    del be_ref
    f = pl.program_id(1)
    nf = pl.num_programs(1)

    @pl.when(pl.program_id(0) < nu_ref[0])
    def _():
        @pl.when(f == 0)
        def _():
            lo, hi = _unpack_halves(xs_ref[...])
            w = lo.shape[1]
            xb_ref[:, :w] = lo.astype(_BF16)
            xb_ref[:, w:] = hi.astype(_BF16)

        xb = xb_ref[...]
        g = jnp.minimum(_dot(xb, wg_ref[...]) + bg_ref[...], SWIGLU_LIMIT)
        u = jnp.clip(_dot(xb, wu_ref[...]) + bu_ref[...], -SWIGLU_LIMIT, SWIGLU_LIMIT)
        a = g * jax.nn.sigmoid(SWIGLU_ALPHA * g) * (u + 1.0)
        part = _dot(a.astype(_BF16), wd_ref[...])

        @pl.when(f == 0)
        def _():
            acc_ref[...] = part

        @pl.when(f > 0)
        def _():
            acc_ref[...] += part

        @pl.when(f == nf - 1)
        def _():
            o_ref[...] = _pack_halves(acc_ref[...] + bd_ref[...])

    @pl.when((pl.program_id(0) >= nu_ref[0]) & (f == nf - 1))
    def _():
        o_ref[...] = jnp.zeros(o_ref.shape, o_ref.dtype)


def _expert_ffn(xs, block_e, n_used, w_gu_b, b_gu, w_down_b, b_down, bm, tf):
    rows, half = xs.shape
    d = 2 * half
    n_experts, _, two_ff = w_gu_b.shape
    d_ff = two_ff // 2
    nf = d_ff // tf
    n_blk = rows // bm

    def blk(i, nu):
        return jnp.minimum(i, nu[0] - 1)

    def fidx(i, f, nu):
        return jnp.where(i < nu[0], f, nf - 1)

    grid_spec = pltpu.PrefetchScalarGridSpec(
        num_scalar_prefetch=2,
        grid=(n_blk, nf),
        in_specs=[
            pl.BlockSpec((bm, half), lambda i, f, be, nu: (blk(i, nu), 0)),
            pl.BlockSpec((None, d, tf), lambda i, f, be, nu: (be[blk(i, nu)], 0, fidx(i, f, nu))),
            pl.BlockSpec((None, d, tf), lambda i, f, be, nu: (be[blk(i, nu)], 0, nf + fidx(i, f, nu))),
            pl.BlockSpec((None, 1, tf), lambda i, f, be, nu: (be[blk(i, nu)], 0, fidx(i, f, nu))),
            pl.BlockSpec((None, 1, tf), lambda i, f, be, nu: (be[blk(i, nu)], 0, nf + fidx(i, f, nu))),
            pl.BlockSpec((None, tf, d), lambda i, f, be, nu: (be[blk(i, nu)], fidx(i, f, nu), 0)),
            pl.BlockSpec((None, 1, d), lambda i, f, be, nu: (be[blk(i, nu)], 0, 0)),
        ],
        out_specs=pl.BlockSpec((bm, half), lambda i, f, be, nu: (i, 0)),
        scratch_shapes=[pltpu.VMEM((bm, d), _BF16), pltpu.VMEM((bm, d), _F32)],
    )
    return pl.pallas_call(
        _ffn_body,
        grid_spec=grid_spec,
        out_shape=jax.ShapeDtypeStruct((rows, half), _U32),
        compiler_params=_params(("arbitrary", "arbitrary"), 56),
        name="expert_ffn",
    )(block_e, n_used, xs, w_gu_b, w_gu_b, b_gu.reshape(n_experts, 1, two_ff), b_gu.reshape(n_experts, 1, two_ff),
      w_down_b, b_down.reshape(n_experts, 1, d))


def _combine_body(dcur_ref, dnext_ref, x1_ref, rec_ref, mod_ref, gf_ref, out_ref, y_ref, gbuf_ref, sems, *, final_norm):
    i = pl.program_id(0)
    n = pl.num_programs(0)
    tm, d = x1_ref.shape
    w = d // 2
    slot = i % 2

    def gather(dest_ref, s):
        def issue(r, carry):
            for k in range(TOP_K):
                _row_copy(out_ref.at[pl.ds(dest_ref[0, 0, r * TOP_K + k], 1)], gbuf_ref.at[s, k, pl.ds(r, 1)],
                          sems.at[s]).start()
            return carry
        lax.fori_loop(0, tm, issue, 0, unroll=8)

    @pl.when(i == 0)
    def _():
        gather(dcur_ref, 0)

    @pl.when(i + 1 < n)
    def _():
        gather(dnext_ref, 1 - slot)

    for k in range(TOP_K):
        _row_copy(out_ref.at[pl.ds(0, tm)], gbuf_ref.at[slot, k], sems.at[slot]).wait()

    rec = rec_ref[...]
    lo = hi = None
    for k in range(TOP_K):
        l, h = _unpack_halves(gbuf_ref[slot, k])
        g = rec[:, REC_GATE + k:REC_GATE + k + 1]
        lo = l * g if lo is None else lo + l * g
        hi = h * g if hi is None else hi + h * g
    gate2 = mod_ref[5, 0]
    x2l = x1_ref[:, :w] + gate2[:, :w] * lo
    x2h = x1_ref[:, w:] + gate2[:, w:] * hi
    if final_norm:
        ms = (jnp.sum(x2l * x2l, axis=-1, keepdims=True) + jnp.sum(x2h * x2h, axis=-1, keepdims=True)) * (1.0 / d)
        inv = lax.rsqrt(ms + EPS)
        x2l = x2l * inv * gf_ref[:, :w]
        x2h = x2h * inv * gf_ref[:, w:]
    y_ref[:, :w] = x2l
    y_ref[:, w:] = x2h


def _combine(x1, out_sorted, dest, rec, mods4, g_final, tm, final_norm):
    t, d = x1.shape
    n_tiles = t // tm
    tiles_per_group = n_tiles // mods4.shape[1]
    dest3 = dest.reshape(n_tiles, 1, tm * TOP_K)
    dspec = lambda nxt: pl.BlockSpec((1, 1, tm * TOP_K), lambda i: (jnp.minimum(i + nxt, n_tiles - 1), 0, 0),
                                     memory_space=pltpu.SMEM)
    return pl.pallas_call(
        functools.partial(_combine_body, final_norm=final_norm),
        grid=(n_tiles,),
        in_specs=[
            dspec(0), dspec(1),
            pl.BlockSpec((tm, d), lambda i: (i, 0)),
            pl.BlockSpec((tm, LANES), lambda i: (i, 0)),
            _mod_spec(mods4, tiles_per_group),
            pl.BlockSpec((1, d), lambda i: (0, 0)),
            pl.BlockSpec(memory_space=pl.ANY),
        ],
        out_specs=pl.BlockSpec((tm, d), lambda i: (i, 0)),
        out_shape=jax.ShapeDtypeStruct((t, d), _F32),
        scratch_shapes=[pltpu.VMEM((2, TOP_K, tm, d // 2), _U32), pltpu.SemaphoreType.DMA((2,))],
        compiler_params=_params(("arbitrary",), 40),
        name="combine_final_norm",
    )(dest3, dest3, x1, rec, mods4, g_final.reshape(1, d), out_sorted)


def _tile(n, pref):
    t = pref
    while n % t:
        t //= 2
    return t


def _moe(h2_p, h2_s, rec_all, counts, w_gu_b, b_gu, w_down_b, b_down, tt, bm, tf):
    t_all = rec_all.shape[0]
    n_experts = w_gu_b.shape[0]
    n_blk = -(-(t_all * TOP_K) // bm) + n_experts
    cnt = counts[0, :n_experts].astype(_I32)
    padded = (cnt + bm - 1) // bm * bm
    pends = jnp.cumsum(padded).astype(_I32)
    pstarts = pends - padded
    idx = rec_all[:, REC_IDX:REC_IDX + TOP_K].astype(_I32)
    rank = rec_all[:, REC_RANK:REC_RANK + TOP_K].astype(_I32)
    dest = pstarts[idx] + rank
    block_e = jnp.minimum(jnp.searchsorted(pends, jnp.arange(n_blk, dtype=_I32) * bm, side="right"),
                          n_experts - 1).astype(_I32)
    n_used = (pends[-1:] // bm).astype(_I32)
    xs = _dispatch(h2_p, h2_s, dest, pends, padded, n_used, n_blk * bm, tt, bm)
    return _expert_ffn(xs, block_e, n_used, w_gu_b, b_gu, w_down_b, b_down, bm, tf), dest


def kernel(x_prompt, x_sample, c_prompt, c_sample, cache_k, cache_v, state_conv, w_ada, b_ada, g_mix, w_in, attn_sink, conv_w, conv_b, conv_ln_g, conv_ln_b, w_out, g_ffn, w_router, b_router, w_gu, b_gu, w_down, b_down, g_final):
    bp, sp, d = x_prompt.shape
    bs, ss, _ = x_sample.shape
    depth = w_ada.shape[0]
    n_experts = w_router.shape[-1]
    tp, ts = bp * sp, bs * ss
    t_all = tp + ts

    tm = _tile(sp, 256)
    tq = _tile(sp, 256)
    tc = _tile(sp, 256)
    tt = _tile(ts, 128)
    bm = 512 if tp >= 8192 else 64
    tf = 512
    assert tp % ts == 0 and tp % tt == 0 and sp % CHUNK == 0 and tq % WINDOW == 0 and tc % CONV_HALO == 0

    xp = x_prompt.reshape(tp, d)
    xsm = x_sample.reshape(ts, d)
    tabs_p = _rope_tables(jnp.arange(sp, dtype=_I32))
    tabs_s = _rope_tables(jnp.tile(PAST_LEN + jnp.arange(ss, dtype=_I32), bs))

    new_kp, new_vp, new_cp, new_ks, new_vs, new_cs = [], [], [], [], [], []
    for l in range(depth):
        w_in_b = w_in[l].astype(_BF16)
        wo_a = w_out[l, :ATTN_WIDTH].astype(_BF16)
        wo_c = w_out[l, ATTN_WIDTH:].astype(_BF16)
        w_gu_b = w_gu[l].astype(_BF16)
        w_down_b = w_down[l].astype(_BF16)
        wr_pad = jnp.pad(w_router[l], ((0, 0), (0, LANES - n_experts)))
        wr_hi = wr_pad.astype(_BF16)
        wr_lo = (wr_pad - wr_hi.astype(_F32)).astype(_BF16)
        br_pad = jnp.pad(b_router[l], (0, LANES - n_experts)).reshape(1, LANES)

        mods = _modulation(jnp.concatenate([c_prompt, c_sample], axis=0), w_ada[l], b_ada[l])
        mods_p = mods[:bp].reshape(bp, 6, 1, d).transpose(1, 0, 2, 3)
        mods_s = jnp.repeat(mods[bp:].reshape(bs, 6, d), ss, axis=0).transpose(1, 0, 2).reshape(6, 1, ts, d)

        qp, kp, vp, up = _in_proj(xp, mods_p, g_mix[l], w_in_b, tabs_p, tm)
        qs, ks, vs, us = _in_proj(xsm, mods_s, g_mix[l], w_in_b, tabs_s, ts)

        att_p = _band_attention(qp, kp, vp, attn_sink[l], bp, sp, tq)
        win = cache_k.shape[2]
        att_s = _cached_attention(qs, ks, vs, cache_k[l].reshape(bs, win, KV_COLS), cache_v[l].reshape(bs, win, KV_COLS),
                                  attn_sink[l], bs, ss)

        cv_p = _conv_module(up, None, conv_w[l], conv_b[l], conv_ln_g[l], conv_ln_b[l], bp, sp, tc)
        hist = jnp.pad(state_conv[l], ((0, 0), (CONV_HALO - (CONV_WIDTH - 1), 0), (0, 0))).reshape(bs * CONV_HALO, CONV_CH)
        cv_s = _conv_module(us, hist, conv_w[l], conv_b[l], conv_ln_g[l], conv_ln_b[l], bs, ss, ss)

        zero_counts = jnp.zeros((1, LANES), _F32)
        x1p, h2p, rec_p, counts = _out_proj(xp, att_p, cv_p, mods_p, g_ffn[l], wo_a, wo_c, wr_hi, wr_lo, br_pad,
                                            zero_counts, tm, n_experts)
        x1s, h2s, rec_s, counts = _out_proj(xsm, att_s, cv_s, mods_s, g_ffn[l], wo_a, wo_c, wr_hi, wr_lo, br_pad,
                                            counts, ts, n_experts)

        out_sorted, dest = _moe(h2p, h2s, jnp.concatenate([rec_p, rec_s], axis=0), counts, w_gu_b, b_gu[l],
                                w_down_b, b_down[l], tt, bm, tf)

        last = l == depth - 1
        xp = _combine(x1p, out_sorted, dest[:tp], rec_p, mods_p, g_final, tm, last)
        xsm = _combine(x1s, out_sorted, dest[tp:], rec_s, mods_s, g_final, ts, last)

        keep = min(WINDOW, sp)
        new_kp.append(kp.reshape(bp, sp, N_KV_HEADS, HEAD_DIM)[:, -keep:])
        new_vp.append(vp.reshape(bp, sp, N_KV_HEADS, HEAD_DIM)[:, -keep:])
        new_cp.append(up.reshape(bp, sp, CONV_CH)[:, -(CONV_WIDTH - 1):])
        new_ks.append(ks.reshape(bs, ss, N_KV_HEADS, HEAD_DIM))
        new_vs.append(vs.reshape(bs, ss, N_KV_HEADS, HEAD_DIM))
        new_cs.append(jnp.concatenate([state_conv[l], us.reshape(bs, ss, CONV_CH)], axis=1)[:, -(CONV_WIDTH - 1):])

    return (xp.reshape(bp, sp, d), xsm.reshape(bs, ss, d), jnp.stack(new_kp), jnp.stack(new_vp), jnp.stack(new_cp),
            jnp.stack(new_ks), jnp.stack(new_vs), jnp.stack(new_cs))
```

```python
import functools

import jax
import jax.numpy as jnp
import numpy as np
from jax import lax
from jax.experimental import pallas as pl
from jax.experimental.pallas import tpu as pltpu

D_MODEL = 2048
CHUNK = 64
HEAD_DIM = 64
ATTN_WIDTH = D_MODEL // 2
CONV_CH = D_MODEL - ATTN_WIDTH
N_HEADS = ATTN_WIDTH // HEAD_DIM
N_KV_HEADS = N_HEADS // 4
GQA_GROUP = N_HEADS // N_KV_HEADS
ROT_DIM = HEAD_DIM // 4
ROPE_THETA = 500000.0
WINDOW = 128
WINDOW_CHUNKS = WINDOW // CHUNK
CONV_WIDTH = 31
TOP_K = 4
SWIGLU_LIMIT = 7.0
SWIGLU_ALPHA = 1.702
EPS = 1e-5
NEG_INF = -1e30
PAST_LEN = 2048
Q_COLS = N_HEADS * HEAD_DIM
KV_COLS = N_KV_HEADS * HEAD_DIM
IN_COLS = Q_COLS + 2 * KV_COLS + 2 * CONV_CH
ATTN_SCALE = HEAD_DIM ** -0.5
assert ATTN_SCALE == 0.125

LANES = 128
SUBLANES = 8
CONV_HALO = 32
MIB = 1024 * 1024

_BF16 = jnp.bfloat16
_F32 = jnp.float32
_I32 = jnp.int32
_U32 = jnp.uint32


def _params(semantics, vmem_mib):
    return pltpu.CompilerParams(dimension_semantics=semantics, vmem_limit_bytes=vmem_mib * MIB)


def _dot(a, b):
    return jnp.dot(a, b, preferred_element_type=_F32)


def _pack_pair(lo, hi):
    lo = lax.bitcast_convert_type(lo.astype(_BF16).astype(_F32), _U32)
    hi = lax.bitcast_convert_type(hi.astype(_BF16).astype(_F32), _U32)
    return (lo >> 16) | (hi & jnp.uint32(0xFFFF0000))


def _pack_halves(x):
    w = x.shape[1] // 2
    return _pack_pair(x[:, :w], x[:, w:])


def _unpack_halves(p):
    return (lax.bitcast_convert_type(p << 16, _F32), lax.bitcast_convert_type(p & jnp.uint32(0xFFFF0000), _F32))


def _rmsnorm_mod(x, g, scale, shift):
    ms = jnp.mean(x * x, axis=-1, keepdims=True)
    return (x * lax.rsqrt(ms + EPS) * g) * (1.0 + scale) + shift


def _ada_body(c_ref, w_ref, b_ref, o_ref):
    c = c_ref[...]
    s = (c * jax.nn.sigmoid(c)).astype(_BF16)
    o_ref[...] = _dot(s, w_ref[...].astype(_BF16)) + b_ref[...]


def _modulation(c, w_ada, b_ada):
    n, d = c.shape
    cols = w_ada.shape[1]
    tn = 1024
    per = d // tn
    return pl.pallas_call(
        _ada_body,
        grid=(cols // tn,),
        in_specs=[
            pl.BlockSpec((n, d), lambda j: (0, 0)),
            pl.BlockSpec((d, tn), lambda j: (0, j)),
            pl.BlockSpec((1, tn), lambda j: (0, j)),
        ],
        out_specs=pl.BlockSpec((None, n, tn), lambda j: (j // per, 0, j % per)),
        out_shape=jax.ShapeDtypeStruct((cols // d, n, d), _F32),
        compiler_params=_params(("arbitrary",), 40),
        name="modulation",
    )(c, w_ada, b_ada.reshape(1, cols))


def _rope_tables(pos):
    half = ROT_DIM // 2
    inv_freq = np.power(np.float32(ROPE_THETA), -np.arange(0, ROT_DIM, 2, dtype=np.float32) / np.float32(ROT_DIM))
    ang = (pos.astype(np.float32)[:, None] * inv_freq[None, :].astype(np.float32)).astype(np.float64)
    cos, sin = np.cos(ang), np.sin(ang)
    d = np.arange(LANES) % HEAD_DIM
    cos_l = np.where(d < ROT_DIM, cos[:, d % half], 1.0)
    sa = np.where(d < half, -sin[:, d % half], 0.0)
    sb = np.where((d >= half) & (d < ROT_DIM), sin[:, d % half], 0.0)
    return tuple(jnp.asarray(t.astype(np.float32)) for t in (cos_l, sa, sb))


def _rotate(z, cos, sa, sb):
    half = ROT_DIM // 2
    parts = []
    for j in range(z.shape[1] // LANES):
        zj = z[:, j * LANES:(j + 1) * LANES]
        parts.append(zj * cos + pltpu.roll(zj, LANES - half, 1) * sa + pltpu.roll(zj, half, 1) * sb)
    return parts[0] if len(parts) == 1 else jnp.concatenate(parts, axis=1)


def _in_proj_body(x_ref, mod_ref, g_ref, w_ref, cos_ref, sa_ref, sb_ref, q_ref, k_ref, v_ref, u_ref):
    h = _rmsnorm_mod(x_ref[...], g_ref[...], mod_ref[1, 0], mod_ref[0, 0]).astype(_BF16)
    cos, sa, sb = cos_ref[...], sa_ref[...], sb_ref[...]
    cw = 512
    for c in range(Q_COLS // cw):
        z = _dot(h, w_ref[:, c * cw:(c + 1) * cw])
        q_ref[:, c * cw:(c + 1) * cw] = (_rotate(z, cos, sa, sb) * ATTN_SCALE).astype(_BF16)
    z = _dot(h, w_ref[:, Q_COLS:Q_COLS + 2 * KV_COLS])
    k_ref[...] = _rotate(z[:, :KV_COLS], cos, sa, sb)
    v_ref[...] = z[:, KV_COLS:]
    o = Q_COLS + 2 * KV_COLS
    for c in range(CONV_CH // cw):
        zv = _dot(h, w_ref[:, o + c * cw:o + (c + 1) * cw])
        zg = _dot(h, w_ref[:, o + CONV_CH + c * cw:o + CONV_CH + (c + 1) * cw])
        u_ref[:, c * cw:(c + 1) * cw] = zv * jax.nn.sigmoid(zg)


def _mod_spec(mods4, tiles_per_group):
    _, _, r, d = mods4.shape
    return pl.BlockSpec((6, 1, r, d), lambda i: (0, i // tiles_per_group, 0, 0))


def _in_proj(x2d, mods4, g_mix, w_in_b, tabs, tm):
    t, d = x2d.shape
    n_tiles = t // tm
    tiles_per_group = n_tiles // mods4.shape[1]
    pos_tiles = tabs[0].shape[0] // tm
    tab_spec = pl.BlockSpec((tm, LANES), lambda i: (i % pos_tiles, 0))
    row = lambda w: pl.BlockSpec((tm, w), lambda i: (i, 0))
    return pl.pallas_call(
        _in_proj_body,
        grid=(n_tiles,),
        in_specs=[
            row(d),
            _mod_spec(mods4, tiles_per_group),
            pl.BlockSpec((1, d), lambda i: (0, 0)),
            pl.BlockSpec((d, IN_COLS), lambda i: (0, 0)),
            tab_spec, tab_spec, tab_spec,
        ],
        out_specs=[row(Q_COLS), row(KV_COLS), row(KV_COLS), row(CONV_CH)],
        out_shape=[
            jax.ShapeDtypeStruct((t, Q_COLS), _BF16),
            jax.ShapeDtypeStruct((t, KV_COLS), _F32),
            jax.ShapeDtypeStruct((t, KV_COLS), _F32),
            jax.ShapeDtypeStruct((t, CONV_CH), _F32),
        ],
        compiler_params=_params(("arbitrary",), 56),
        name="in_proj",
    )(x2d, mods4, g_mix.reshape(1, d), w_in_b, *tabs)


def _attn_group(qg, kk, vv, sink_col, valid):
    s = lax.dot_general(qg, kk, (((1,), (1,)), ((), ())), preferred_element_type=_F32)
    if valid is not None:
        s = jnp.where(valid, s, NEG_INF)
    m = jnp.maximum(jnp.max(s, axis=-1, keepdims=True), sink_col)
    p = jnp.exp(s - m)
    den = jnp.sum(p, axis=-1, keepdims=True) + jnp.exp(sink_col - m)
    return _dot(p.astype(_BF16), vv) / den


def _sink_column(sink_ref, kh, rows_per_head):
    row = lax.broadcasted_iota(_I32, (GQA_GROUP * rows_per_head, 1), 0)
    col = jnp.full(row.shape, sink_ref[kh * GQA_GROUP + GQA_GROUP - 1], _F32)
    for g in range(GQA_GROUP - 2, -1, -1):
        col = jnp.where(row < (g + 1) * rows_per_head, sink_ref[kh * GQA_GROUP + g], col)
    return col


def _store_heads(o_ref, r0, rows, kh, o):
    for pair in range(GQA_GROUP // 2):
        both = jnp.concatenate([o[(2 * pair) * rows:(2 * pair + 1) * rows],
                                o[(2 * pair + 1) * rows:(2 * pair + 2) * rows]], axis=1)
        c0 = (kh * GQA_GROUP + 2 * pair) * HEAD_DIM
        o_ref[r0:r0 + rows, c0:c0 + 2 * HEAD_DIM] = both.astype(o_ref.dtype)


def _stack_heads(q, r0, rows, kh):
    return jnp.concatenate(
        [q[r0:r0 + rows, (kh * GQA_GROUP + g) * HEAD_DIM:(kh * GQA_GROUP + g + 1) * HEAD_DIM] for g in range(GQA_GROUP)],
        axis=0)


def _band_attn_body(sink_ref, q_ref, kc_ref, kh_ref, vc_ref, vh_ref, o_ref, *, tq):
    j = pl.program_id(1)
    q = q_ref[...]
    kall = jnp.concatenate([kh_ref[...], kc_ref[...]], axis=0)
    vall = jnp.concatenate([vh_ref[...], vc_ref[...]], axis=0)
    band = (WINDOW_CHUNKS + 1) * CHUNK
    key_chunk = lax.broadcasted_iota(_I32, (1, band), 1) // CHUNK
    for kh in range(N_KV_HEADS):
        sink_col = _sink_column(sink_ref, kh, CHUNK)
        kk = kall[:, kh * HEAD_DIM:(kh + 1) * HEAD_DIM].astype(_BF16)
        vv = vall[:, kh * HEAD_DIM:(kh + 1) * HEAD_DIM].astype(_BF16)
        for ci in range(tq // CHUNK):
            qg = _stack_heads(q, ci * CHUNK, CHUNK, kh)
            valid = None if ci >= WINDOW_CHUNKS else (j * (tq // CHUNK) + ci - WINDOW_CHUNKS + key_chunk) >= 0
            o = _attn_group(qg, kk[ci * CHUNK:ci * CHUNK + band], vv[ci * CHUNK:ci * CHUNK + band], sink_col, valid)
            _store_heads(o_ref, ci * CHUNK, CHUNK, kh, o)


def _band_attention(q, k, v, sink, batch, seq, tq):
    nq = seq // tq
    r = tq // WINDOW
    cur = lambda w: pl.BlockSpec((tq, w), lambda b, j: (b * nq + j, 0))
    halo = pl.BlockSpec((WINDOW, KV_COLS), lambda b, j: (b * nq * r + jnp.maximum(j * r - 1, 0), 0))
    return pl.pallas_call(
        functools.partial(_band_attn_body, tq=tq),
        grid=(batch, nq),
        in_specs=[pl.BlockSpec(memory_space=pltpu.SMEM), cur(Q_COLS), cur(KV_COLS), halo, cur(KV_COLS), halo],
        out_specs=cur(Q_COLS),
        out_shape=jax.ShapeDtypeStruct((batch * seq, Q_COLS), _BF16),
        compiler_params=_params(("arbitrary", "arbitrary"), 40),
        name="band_attention",
    )(sink, q, k, k, v, v)


def _cached_attn_body(sink_ref, q_ref, kn_ref, kc_ref, vn_ref, vc_ref, o_ref, *, rows):
    q = q_ref[...]
    kall = jnp.concatenate([kc_ref[0], kn_ref[...]], axis=0)
    vall = jnp.concatenate([vc_ref[0], vn_ref[...]], axis=0)
    for kh in range(N_KV_HEADS):
        o = _attn_group(_stack_heads(q, 0, rows, kh), kall[:, kh * HEAD_DIM:(kh + 1) * HEAD_DIM].astype(_BF16),
                        vall[:, kh * HEAD_DIM:(kh + 1) * HEAD_DIM].astype(_BF16), _sink_column(sink_ref, kh, rows), None)
        _store_heads(o_ref, 0, rows, kh, o)


def _cached_attention(q, k, v, cache_k, cache_v, sink, batch, rows):
    win = cache_k.shape[1]
    new = lambda w: pl.BlockSpec((rows, w), lambda b: (b, 0))
    cache = pl.BlockSpec((1, win, KV_COLS), lambda b: (b, 0, 0))
    return pl.pallas_call(
        functools.partial(_cached_attn_body, rows=rows),
        grid=(batch,),
        in_specs=[pl.BlockSpec(memory_space=pltpu.SMEM), new(Q_COLS), new(KV_COLS), cache, new(KV_COLS), cache],
        out_specs=new(Q_COLS),
        out_shape=jax.ShapeDtypeStruct((batch * rows, Q_COLS), _BF16),
        compiler_params=_params(("arbitrary",), 40),
        name="cached_attention",
    )(sink, q, k, cache_k, v, cache_v)


def _conv_body(uc_ref, uh_ref, w_ref, b_ref, g_ref, beta_ref, o_ref, win_ref, y_ref, *, tt, zero_first_halo):
    halo = uh_ref[...]
    if zero_first_halo:
        halo = jnp.where(pl.program_id(1) == 0, 0.0, halo)
    win_ref[0:CONV_HALO, :] = halo
    win_ref[CONV_HALO:, :] = uc_ref[...]
    lead = CONV_HALO - (CONV_WIDTH - 1)
    by_shift = {}
    for tap in range(CONV_WIDTH):
        by_shift.setdefault((lead + tap) % SUBLANES, []).append(tap)
    rb = min(tt, 128)
    for s in range(CONV_CH // LANES):
        cs = slice(s * LANES, (s + 1) * LANES)
        for r0 in range(0, tt, rb):
            y = b_ref[:, cs]
            for sh, taps in sorted(by_shift.items()):
                ext = SUBLANES if sh else 0
                q = None
                for tap in taps:
                    base = r0 + lead + tap - sh
                    term = w_ref[tap:tap + 1, cs] * win_ref[base:base + rb + ext, cs]
                    q = term if q is None else q + term
                y = y + (q[sh:sh + rb] if sh else q)
            y_ref[r0:r0 + rb, cs] = y
    y = y_ref[...]
    mu = jnp.mean(y, axis=-1, keepdims=True)
    var = jnp.mean(jnp.square(y - mu), axis=-1, keepdims=True)
    z = (y - mu) * lax.rsqrt(var + EPS) * g_ref[...] + beta_ref[...]
    o_ref[...] = (z * jax.nn.sigmoid(z)).astype(o_ref.dtype)


def _conv_module(u, hist, conv_w, conv_b, ln_g, ln_b, batch, seq, tt):
    nt = seq // tt
    r = tt // CONV_HALO
    cur = pl.BlockSpec((tt, CONV_CH), lambda b, j: (b * nt + j, 0))
    if hist is None:
        hist_arr = u
        halo = pl.BlockSpec((CONV_HALO, CONV_CH), lambda b, j: (b * nt * r + jnp.maximum(j * r - 1, 0), 0))
    else:
        assert nt == 1
        hist_arr = hist
        halo = pl.BlockSpec((CONV_HALO, CONV_CH), lambda b, j: (b, 0))
    vec = pl.BlockSpec((1, CONV_CH), lambda b, j: (0, 0))
    wpad = jnp.pad(conv_w.reshape(CONV_WIDTH, CONV_CH), ((0, 1), (0, 0)))
    return pl.pallas_call(
        functools.partial(_conv_body, tt=tt, zero_first_halo=hist is None),
        grid=(batch, nt),
        in_specs=[cur, halo, pl.BlockSpec((CONV_WIDTH + 1, CONV_CH), lambda b, j: (0, 0)), vec, vec, vec],
        out_specs=cur,
        out_shape=jax.ShapeDtypeStruct((batch * seq, CONV_CH), _BF16),
        scratch_shapes=[pltpu.VMEM((tt + CONV_HALO, CONV_CH), _F32), pltpu.VMEM((tt, CONV_CH), _F32)],
        compiler_params=_params(("arbitrary", "arbitrary"), 40),
        name="conv_module",
    )(u, hist_arr, wpad, conv_b.reshape(1, -1), ln_g.reshape(1, -1), ln_b.reshape(1, -1))


REC_IDX, REC_GATE, REC_RANK = 0, TOP_K, 2 * TOP_K


def _out_proj_body(x_ref, a_ref, c_ref, mod_ref, g_ref, wa_ref, wc_ref, wrh_ref, wrl_ref, br_ref, cnt_ref,
                   x1_ref, h2_ref, rec_ref, cnt_out_ref, run_ref, *, n_experts):
    tm = x_ref.shape[0]

    @pl.when(pl.program_id(0) == 0)
    def _():
        run_ref[...] = cnt_ref[...]

    o = _dot(a_ref[...], wa_ref[...]) + _dot(c_ref[...], wc_ref[...])
    x1 = x_ref[...] + mod_ref[2, 0] * o
    x1_ref[...] = x1
    h2 = _rmsnorm_mod(x1, g_ref[...], mod_ref[4, 0], mod_ref[3, 0])
    h2_ref[...] = _pack_halves(h2)

    h2_hi = h2.astype(_BF16)
    h2_lo = (h2 - h2_hi.astype(_F32)).astype(_BF16)
    logits = _dot(h2_hi, wrh_ref[...]) + (_dot(h2_lo, wrh_ref[...]) + _dot(h2_hi, wrl_ref[...])) + br_ref[...]
    lane = lax.broadcasted_iota(_I32, (tm, LANES), 1)
    work = jnp.where(lane < n_experts, logits, -jnp.inf)
    vals, hots = [], []
    rec = jnp.zeros((tm, LANES), _F32)
    for k in range(TOP_K):
        m = jnp.max(work, axis=-1, keepdims=True)
        idx = jnp.min(jnp.where(work == m, lane, LANES), axis=-1, keepdims=True)
        hot = lane == idx
        work = jnp.where(hot, -jnp.inf, work)
        vals.append(m)
        hots.append(hot)
        rec = jnp.where(lane == REC_IDX + k, idx.astype(_F32), rec)
    exps = [jnp.exp(v - vals[0]) for v in vals]
    den = exps[0]
    for e in exps[1:]:
        den = den + e
    for k in range(TOP_K):
        rec = jnp.where(lane == REC_GATE + k, exps[k] / den, rec)

    chosen = jnp.zeros((tm, LANES), _F32)
    for hot in hots:
        chosen = jnp.where(hot, 1.0, chosen)
    r_i = lax.broadcasted_iota(_I32, (tm, tm), 0)
    c_i = lax.broadcasted_iota(_I32, (tm, tm), 1)
    before = _dot(jnp.where(c_i < r_i, 1.0, 0.0).astype(_BF16), chosen.astype(_BF16)) + run_ref[...]
    for k in range(TOP_K):
        rank = jnp.sum(jnp.where(hots[k], before, 0.0), axis=-1, keepdims=True)
        rec = jnp.where(lane == REC_RANK + k, rank, rec)
    rec_ref[...] = rec
    run_ref[...] = run_ref[...] + jnp.sum(chosen, axis=0, keepdims=True)
    cnt_out_ref[...] = run_ref[...]


def _out_proj(x2d, attn, conv, mods4, g_ffn, wo_a, wo_c, wr_hi, wr_lo, br_pad, counts, tm, n_experts):
    t, d = x2d.shape
    n_tiles = t // tm
    tiles_per_group = n_tiles // mods4.shape[1]
    row = lambda w: pl.BlockSpec((tm, w), lambda i: (i, 0))
    const = lambda a, b: pl.BlockSpec((a, b), lambda i: (0, 0))
    return pl.pallas_call(
        functools.partial(_out_proj_body, n_experts=n_experts),
        grid=(n_tiles,),
        in_specs=[row(d), row(ATTN_WIDTH), row(CONV_CH), _mod_spec(mods4, tiles_per_group), const(1, d),
                  const(ATTN_WIDTH, d), const(CONV_CH, d), const(d, LANES), const(d, LANES), const(1, LANES),
                  const(1, LANES)],
        out_specs=[row(d), row(d // 2), row(LANES), const(1, LANES)],
        out_shape=[
            jax.ShapeDtypeStruct((t, d), _F32),
            jax.ShapeDtypeStruct((t, d // 2), _U32),
            jax.ShapeDtypeStruct((t, LANES), _F32),
            jax.ShapeDtypeStruct((1, LANES), _F32),
        ],
        scratch_shapes=[pltpu.VMEM((1, LANES), _F32)],
        compiler_params=_params(("arbitrary",), 56),
        name="out_proj_router",
    )(x2d, attn, conv, mods4, g_ffn.reshape(1, d), wo_a, wo_c, wr_hi, wr_lo, br_pad, counts)


def _row_copy(src, dst, sem):
    return pltpu.make_async_copy(src, dst, sem)


def _dispatch_body(pends_ref, padded_ref, nu_ref, dest_ref, hp_ref, hs_ref, xs_ref, stage_ref, zeros_ref, sems, zsem, *,
                   tt, bm, n_experts, prompt_steps):
    i = pl.program_id(0)
    n = pl.num_programs(0)
    n_blk = xs_ref.shape[0] // bm
    slot = i % 2

    def zero_block(row0):
        return _row_copy(zeros_ref, xs_ref.at[pl.ds(pl.multiple_of(row0, bm), bm)], zsem)

    @pl.when(i == 0)
    def _():
        zeros_ref[...] = jnp.zeros(zeros_ref.shape, zeros_ref.dtype)
        for e in range(n_experts):
            @pl.when(padded_ref[e] > 0)
            def _():
                zero_block(pends_ref[e] - bm).start()

        def tail_start(b, carry):
            zero_block(b * bm).start()
            return carry

        def tail_wait(b, carry):
            zero_block(b * bm).wait()
            return carry

        lax.fori_loop(nu_ref[0], n_blk, tail_start, 0)
        for e in range(n_experts):
            @pl.when(padded_ref[e] > 0)
            def _():
                zero_block(pends_ref[e] - bm).wait()
        lax.fori_loop(nu_ref[0], n_blk, tail_wait, 0)

    @pl.when(i < prompt_steps)
    def _():
        stage_ref[slot] = hp_ref[...]

    @pl.when(i >= prompt_steps)
    def _():
        stage_ref[slot] = hs_ref[...]

    def issue(r, carry):
        src = stage_ref.at[slot, pl.ds(r, 1)]
        for k in range(TOP_K):
            _row_copy(src, xs_ref.at[pl.ds(dest_ref[0, 0, r * TOP_K + k], 1)], sems.at[slot]).start(priority=k % 2)
        return carry

    lax.fori_loop(0, tt, issue, 0, unroll=8)

    def drain(s):
        for _ in range(TOP_K):
            _row_copy(stage_ref.at[s], xs_ref.at[pl.ds(0, tt)], sems.at[s]).wait()

    @pl.when(i > 0)
    def _():
        drain(1 - slot)

    @pl.when(i == n - 1)
    def _():
        drain(slot)


def _dispatch(h2_p, h2_s, dest, pends, padded, n_used, rows, tt, bm):
    w = h2_p.shape[1]
    tp, ts = h2_p.shape[0], h2_s.shape[0]
    n_experts = pends.shape[0]
    prompt_steps = tp // tt
    grid_spec = pltpu.PrefetchScalarGridSpec(
        num_scalar_prefetch=3,
        grid=((tp + ts) // tt,),
        in_specs=[
            pl.BlockSpec((1, 1, tt * TOP_K), lambda i, *_: (i, 0, 0), memory_space=pltpu.SMEM),
            pl.BlockSpec((tt, w), lambda i, *_: (jnp.minimum(i, prompt_steps - 1), 0)),
            pl.BlockSpec((tt, w), lambda i, *_: (jnp.maximum(i - prompt_steps, 0), 0)),
        ],
        out_specs=pl.BlockSpec(memory_space=pl.ANY),
        scratch_shapes=[pltpu.VMEM((2, tt, w), _U32), pltpu.VMEM((bm, w), _U32),
                        pltpu.SemaphoreType.DMA((2,)), pltpu.SemaphoreType.DMA(())],
    )
    return pl.pallas_call(
        functools.partial(_dispatch_body, tt=tt, bm=bm, n_experts=n_experts, prompt_steps=prompt_steps),
        grid_spec=grid_spec,
        out_shape=jax.ShapeDtypeStruct((rows, w), _U32),
        compiler_params=_params(("arbitrary",), 24),
        name="moe_dispatch",
    )(pends, padded, n_used, dest.reshape((tp + ts) // tt, 1, tt * TOP_K), h2_p, h2_s)


def _ffn_body(be_ref, nu_ref, xs_ref, bgu_ref, bd_ref, wgu_hbm, wd_hbm, o_ref, wgu_ref, wd_ref, xb_ref, a_ref, sems, *,
              chunk):
    i = pl.program_id(0)
    n_used = nu_ref[0]
    d_ff = wd_ref.shape[0]
    half = xs_ref.shape[1]

    def gu_copy(e):
        return _row_copy(wgu_hbm.at[e], wgu_ref, sems.at[0])

    def down_copy(e):
        return _row_copy(wd_hbm.at[e], wd_ref, sems.at[1])

    @pl.when(i < n_used)
    def _():
        e = be_ref[i]
        first = (i == 0) | (e != be_ref[jnp.maximum(i - 1, 0)])
        e_next = be_ref[jnp.minimum(i + 1, n_used - 1)]

        @pl.when(i == 0)
        def _():
            gu_copy(e).start()

        @pl.when(first)
        def _():
            down_copy(e).start()
            gu_copy(e).wait()

        lo, hi = _unpack_halves(xs_ref[...])
        xb_ref[:, :half] = lo.astype(_BF16)
        xb_ref[:, half:] = hi.astype(_BF16)
        xb = xb_ref[...]
        for c in range(d_ff // chunk):
            cs = slice(c * chunk, (c + 1) * chunk)
            us = slice(d_ff + c * chunk, d_ff + (c + 1) * chunk)
            g = jnp.minimum(_dot(xb, wgu_ref[:, cs]) + bgu_ref[:, cs], SWIGLU_LIMIT)
            u = jnp.clip(_dot(xb, wgu_ref[:, us]) + bgu_ref[:, us], -SWIGLU_LIMIT, SWIGLU_LIMIT)
            a_ref[:, cs] = (g * jax.nn.sigmoid(SWIGLU_ALPHA * g) * (u + 1.0)).astype(_BF16)

        @pl.when(e_next != e)
        def _():
            gu_copy(e_next).start()

        @pl.when(first)
        def _():
            down_copy(e).wait()

        a = a_ref[...]
        for c in range(half // chunk):
            cs = slice(c * chunk, (c + 1) * chunk)
            hs = slice(half + c * chunk, half + (c + 1) * chunk)
            o_lo = _dot(a, wd_ref[:, cs]) + bd_ref[:, cs]
            o_hi = _dot(a, wd_ref[:, hs]) + bd_ref[:, hs]
            o_ref[:, cs] = _pack_pair(o_lo, o_hi)

    @pl.when(i >= n_used)
    def _():
        o_ref[...] = jnp.zeros(o_ref.shape, o_ref.dtype)


def _expert_ffn(xs, block_e, n_used, w_gu_b, b_gu, w_down_b, b_down, bm, chunk):
    rows, half = xs.shape
    d = 2 * half
    n_experts, _, two_ff = w_gu_b.shape
    d_ff = two_ff // 2
    n_blk = rows // bm

    def blk(i, nu):
        return jnp.minimum(i, nu[0] - 1)

    hbm = pl.BlockSpec(memory_space=pl.ANY)
    grid_spec = pltpu.PrefetchScalarGridSpec(
        num_scalar_prefetch=2,
        grid=(n_blk,),
        in_specs=[
            pl.BlockSpec((bm, half), lambda i, be, nu: (blk(i, nu), 0)),
            pl.BlockSpec((None, 1, two_ff), lambda i, be, nu: (be[blk(i, nu)], 0, 0)),
            pl.BlockSpec((None, 1, d), lambda i, be, nu: (be[blk(i, nu)], 0, 0)),
            hbm, hbm,
        ],
        out_specs=pl.BlockSpec((bm, half), lambda i, be, nu: (i, 0)),
        scratch_shapes=[pltpu.VMEM((d, two_ff), _BF16), pltpu.VMEM((d_ff, d), _BF16), pltpu.VMEM((bm, d), _BF16),
                        pltpu.VMEM((bm, d_ff), _BF16), pltpu.SemaphoreType.DMA((2,))],
    )
    return pl.pallas_call(
        functools.partial(_ffn_body, chunk=chunk),
        grid_spec=grid_spec,
        out_shape=jax.ShapeDtypeStruct((rows, half), _U32),
        compiler_params=_params(("arbitrary",), 56),
        name="expert_ffn",
    )(block_e, n_used, xs, b_gu.reshape(n_experts, 1, two_ff), b_down.reshape(n_experts, 1, d), w_gu_b, w_down_b)


def _combine_body(dcur_ref, dnext_ref, x1_ref, rec_ref, mod_ref, gf_ref, out_ref, y_ref, gbuf_ref, sems, *, final_norm):
    i = pl.program_id(0)
    n = pl.num_programs(0)
    tm, d = x1_ref.shape
    w = d // 2
    slot = i % 2

    def gather(dest_ref, s):
        def issue(r, carry):
            for k in range(TOP_K):
                _row_copy(out_ref.at[pl.ds(dest_ref[0, 0, r * TOP_K + k], 1)], gbuf_ref.at[s, k, pl.ds(r, 1)],
                          sems.at[s]).start(priority=k % 2)
            return carry
        lax.fori_loop(0, tm, issue, 0, unroll=8)

    @pl.when(i == 0)
    def _():
        gather(dcur_ref, 0)

    @pl.when(i + 1 < n)
    def _():
        gather(dnext_ref, 1 - slot)

    for k in range(TOP_K):
        _row_copy(out_ref.at[pl.ds(0, tm)], gbuf_ref.at[slot, k], sems.at[slot]).wait()

    rec = rec_ref[...]
    lo = hi = None
    for k in range(TOP_K):
        l, h = _unpack_halves(gbuf_ref[slot, k])
        g = rec[:, REC_GATE + k:REC_GATE + k + 1]
        lo = l * g if lo is None else lo + l * g
        hi = h * g if hi is None else hi + h * g
    gate2 = mod_ref[5, 0]
    x2l = x1_ref[:, :w] + gate2[:, :w] * lo
    x2h = x1_ref[:, w:] + gate2[:, w:] * hi
    if final_norm:
        ms = (jnp.sum(x2l * x2l, axis=-1, keepdims=True) + jnp.sum(x2h * x2h, axis=-1, keepdims=True)) * (1.0 / d)
        inv = lax.rsqrt(ms + EPS)
        x2l = x2l * inv * gf_ref[:, :w]
        x2h = x2h * inv * gf_ref[:, w:]
    y_ref[:, :w] = x2l
    y_ref[:, w:] = x2h


def _combine(x1, out_sorted, dest, rec, mods4, g_final, tm, final_norm):
    t, d = x1.shape
    n_tiles = t // tm
    tiles_per_group = n_tiles // mods4.shape[1]
    dest3 = dest.reshape(n_tiles, 1, tm * TOP_K)
    dspec = lambda nxt: pl.BlockSpec((1, 1, tm * TOP_K), lambda i: (jnp.minimum(i + nxt, n_tiles - 1), 0, 0),
                                     memory_space=pltpu.SMEM)
    return pl.pallas_call(
        functools.partial(_combine_body, final_norm=final_norm),
        grid=(n_tiles,),
        in_specs=[
            dspec(0), dspec(1),
            pl.BlockSpec((tm, d), lambda i: (i, 0)),
            pl.BlockSpec((tm, LANES), lambda i: (i, 0)),
            _mod_spec(mods4, tiles_per_group),
            pl.BlockSpec((1, d), lambda i: (0, 0)),
            pl.BlockSpec(memory_space=pl.ANY),
        ],
        out_specs=pl.BlockSpec((tm, d), lambda i: (i, 0)),
        out_shape=jax.ShapeDtypeStruct((t, d), _F32),
        scratch_shapes=[pltpu.VMEM((2, TOP_K, tm, d // 2), _U32), pltpu.SemaphoreType.DMA((2,))],
        compiler_params=_params(("arbitrary",), 40),
        name="combine_final_norm",
    )(dest3, dest3, x1, rec, mods4, g_final.reshape(1, d), out_sorted)


def _tile(n, pref):
    t = pref
    while n % t:
        t //= 2
    return t


def _moe(h2_p, h2_s, rec_all, counts, w_gu_b, b_gu, w_down_b, b_down, tt, bm, chunk):
    t_all = rec_all.shape[0]
    n_experts = w_gu_b.shape[0]
    n_blk = -(-(t_all * TOP_K) // bm) + n_experts
    cnt = counts[0, :n_experts].astype(_I32)
    padded = (cnt + bm - 1) // bm * bm
    pends = jnp.cumsum(padded).astype(_I32)
    pstarts = pends - padded
    idx = rec_all[:, REC_IDX:REC_IDX + TOP_K].astype(_I32)
    rank = rec_all[:, REC_RANK:REC_RANK + TOP_K].astype(_I32)
    experts = jnp.arange(n_experts, dtype=_I32)
    dest = rank + jnp.sum(jnp.where(idx[:, :, None] == experts, pstarts, 0), axis=-1)
    blk_row0 = jnp.arange(n_blk, dtype=_I32) * bm
    block_e = jnp.minimum(jnp.sum((pends[None, :] <= blk_row0[:, None]).astype(_I32), axis=1), n_experts - 1)
    n_used = (pends[-1:] // bm).astype(_I32)
    xs = _dispatch(h2_p, h2_s, dest, pends, padded, n_used, n_blk * bm, tt, bm)
    return _expert_ffn(xs, block_e, n_used, w_gu_b, b_gu, w_down_b, b_down, bm, chunk), dest


def kernel(x_prompt, x_sample, c_prompt, c_sample, cache_k, cache_v, state_conv, w_ada, b_ada, g_mix, w_in, attn_sink, conv_w, conv_b, conv_ln_g, conv_ln_b, w_out, g_ffn, w_router, b_router, w_gu, b_gu, w_down, b_down, g_final):
    bp, sp, d = x_prompt.shape
    bs, ss, _ = x_sample.shape
    depth = w_ada.shape[0]
    n_experts = w_router.shape[-1]
    tp, ts = bp * sp, bs * ss
    t_all = tp + ts

    tm = _tile(sp, 256)
    to = _tile(sp, 512)
    tq = _tile(sp, 256)
    tc = _tile(sp, 256)
    tt = _tile(ts, 128)
    bm = 512 if tp >= 8192 else 64
    chunk = 512
    assert tp % ts == 0 and tp % tt == 0 and sp % CHUNK == 0 and tq % WINDOW == 0 and tc % CONV_HALO == 0

    xp = x_prompt.reshape(tp, d)
    xsm = x_sample.reshape(ts, d)
    tabs_p = _rope_tables(np.arange(sp))
    tabs_s = _rope_tables(np.tile(PAST_LEN + np.arange(ss), bs))

    new_kp, new_vp, new_cp, new_ks, new_vs, new_cs = [], [], [], [], [], []
    for l in range(depth):
        w_in_b = w_in[l].astype(_BF16)
        wo_a = w_out[l, :ATTN_WIDTH].astype(_BF16)
        wo_c = w_out[l, ATTN_WIDTH:].astype(_BF16)
        w_gu_b = w_gu[l].astype(_BF16)
        w_down_b = w_down[l].astype(_BF16)
        wr_pad = jnp.pad(w_router[l], ((0, 0), (0, LANES - n_experts)))
        wr_hi = wr_pad.astype(_BF16)
        wr_lo = (wr_pad - wr_hi.astype(_F32)).astype(_BF16)
        br_pad = jnp.pad(b_router[l], (0, LANES - n_experts)).reshape(1, LANES)

        mods = _modulation(jnp.concatenate([c_prompt, c_sample], axis=0), w_ada[l], b_ada[l])
        mods_p = mods[:, :bp].reshape(6, bp, 1, d)
        mods_s = jnp.repeat(mods[:, bp:], ss, axis=1).reshape(6, 1, ts, d)

        qp, kp, vp, up = _in_proj(xp, mods_p, g_mix[l], w_in_b, tabs_p, tm)
        qs, ks, vs, us = _in_proj(xsm, mods_s, g_mix[l], w_in_b, tabs_s, ts)

        att_p = _band_attention(qp, kp, vp, attn_sink[l], bp, sp, tq)
        win = cache_k.shape[2]
        att_s = _cached_attention(qs, ks, vs, cache_k[l].reshape(bs, win, KV_COLS), cache_v[l].reshape(bs, win, KV_COLS),
                                  attn_sink[l], bs, ss)

        cv_p = _conv_module(up, None, conv_w[l], conv_b[l], conv_ln_g[l], conv_ln_b[l], bp, sp, tc)
        hist = jnp.pad(state_conv[l], ((0, 0), (CONV_HALO - (CONV_WIDTH - 1), 0), (0, 0))).reshape(bs * CONV_HALO, CONV_CH)
        cv_s = _conv_module(us, hist, conv_w[l], conv_b[l], conv_ln_g[l], conv_ln_b[l], bs, ss, ss)

        zero_counts = jnp.zeros((1, LANES), _F32)
        x1p, h2p, rec_p, counts = _out_proj(xp, att_p, cv_p, mods_p, g_ffn[l], wo_a, wo_c, wr_hi, wr_lo, br_pad,
                                            zero_counts, to, n_experts)
        x1s, h2s, rec_s, counts = _out_proj(xsm, att_s, cv_s, mods_s, g_ffn[l], wo_a, wo_c, wr_hi, wr_lo, br_pad,
                                            counts, ts, n_experts)

        out_sorted, dest = _moe(h2p, h2s, jnp.concatenate([rec_p, rec_s], axis=0), counts, w_gu_b, b_gu[l],
                                w_down_b, b_down[l], tt, bm, chunk)

        last = l == depth - 1
        xp = _combine(x1p, out_sorted, dest[:tp], rec_p, mods_p, g_final, tm, last)
        xsm = _combine(x1s, out_sorted, dest[tp:], rec_s, mods_s, g_final, ts, last)

        keep = min(WINDOW, sp)
        new_kp.append(kp.reshape(bp, sp, N_KV_HEADS, HEAD_DIM)[:, -keep:])
        new_vp.append(vp.reshape(bp, sp, N_KV_HEADS, HEAD_DIM)[:, -keep:])
        new_cp.append(up.reshape(bp, sp, CONV_CH)[:, -(CONV_WIDTH - 1):])
        new_ks.append(ks.reshape(bs, ss, N_KV_HEADS, HEAD_DIM))
        new_vs.append(vs.reshape(bs, ss, N_KV_HEADS, HEAD_DIM))
        new_cs.append(jnp.concatenate([state_conv[l], us.reshape(bs, ss, CONV_CH)], axis=1)[:, -(CONV_WIDTH - 1):])

    return (xp.reshape(bp, sp, d), xsm.reshape(bs, ss, d), jnp.stack(new_kp), jnp.stack(new_vp), jnp.stack(new_cp),
            jnp.stack(new_ks), jnp.stack(new_vs), jnp.stack(new_cs))
```

```python
import functools

import jax
import jax.numpy as jnp
import numpy as np
from jax import lax
from jax.experimental import pallas as pl
from jax.experimental.pallas import tpu as pltpu

D_MODEL = 2048
CHUNK = 64
HEAD_DIM = 64
ATTN_WIDTH = D_MODEL // 2
CONV_CH = D_MODEL - ATTN_WIDTH
N_HEADS = ATTN_WIDTH // HEAD_DIM
N_KV_HEADS = N_HEADS // 4
GQA_GROUP = N_HEADS // N_KV_HEADS
ROT_DIM = HEAD_DIM // 4
ROPE_THETA = 500000.0
WINDOW = 128
WINDOW_CHUNKS = WINDOW // CHUNK
CONV_WIDTH = 31
TOP_K = 4
SWIGLU_LIMIT = 7.0
SWIGLU_ALPHA = 1.702
EPS = 1e-5
NEG_INF = -1e30
PAST_LEN = 2048
Q_COLS = N_HEADS * HEAD_DIM
KV_COLS = N_KV_HEADS * HEAD_DIM
IN_COLS = Q_COLS + 2 * KV_COLS + 2 * CONV_CH
ATTN_SCALE = HEAD_DIM ** -0.5
assert ATTN_SCALE == 0.125

LANES = 128
SUBLANES = 8
CONV_HALO = 32
MIB = 1024 * 1024

_BF16 = jnp.bfloat16
_F32 = jnp.float32
_I32 = jnp.int32
_U32 = jnp.uint32


def _params(semantics, vmem_mib):
    return pltpu.CompilerParams(dimension_semantics=semantics, vmem_limit_bytes=vmem_mib * MIB)


def _dot(a, b):
    return jnp.dot(a, b, preferred_element_type=_F32)


def _pack_pair(lo, hi):
    lo = lax.bitcast_convert_type(lo.astype(_BF16).astype(_F32), _U32)
    hi = lax.bitcast_convert_type(hi.astype(_BF16).astype(_F32), _U32)
    return (lo >> 16) | (hi & jnp.uint32(0xFFFF0000))


def _pack_halves(x):
    w = x.shape[1] // 2
    return _pack_pair(x[:, :w], x[:, w:])


def _unpack_halves(p):
    return (lax.bitcast_convert_type(p << 16, _F32), lax.bitcast_convert_type(p & jnp.uint32(0xFFFF0000), _F32))


def _rmsnorm_mod(x, g, scale, shift):
    ms = jnp.mean(x * x, axis=-1, keepdims=True)
    return (x * lax.rsqrt(ms + EPS) * g) * (1.0 + scale) + shift


def _ada_body(c_ref, w_ref, b_ref, o_ref):
    c = c_ref[...]
    s = (c * jax.nn.sigmoid(c)).astype(_BF16)
    o_ref[...] = _dot(s, w_ref[...].astype(_BF16)) + b_ref[...]


def _modulation(c, w_ada, b_ada):
    n, d = c.shape
    cols = w_ada.shape[1]
    tn = 1024
    per = d // tn
    return pl.pallas_call(
        _ada_body,
        grid=(cols // tn,),
        in_specs=[
            pl.BlockSpec((n, d), lambda j: (0, 0)),
            pl.BlockSpec((d, tn), lambda j: (0, j)),
            pl.BlockSpec((1, tn), lambda j: (0, j)),
        ],
        out_specs=pl.BlockSpec((None, n, tn), lambda j: (j // per, 0, j % per)),
        out_shape=jax.ShapeDtypeStruct((cols // d, n, d), _F32),
        compiler_params=_params(("arbitrary",), 40),
        name="modulation",
    )(c, w_ada, b_ada.reshape(1, cols))


def _rope_tables(pos):
    half = ROT_DIM // 2
    inv_freq = np.power(np.float32(ROPE_THETA), -np.arange(0, ROT_DIM, 2, dtype=np.float32) / np.float32(ROT_DIM))
    ang = (pos.astype(np.float32)[:, None] * inv_freq[None, :].astype(np.float32)).astype(np.float64)
    cos, sin = np.cos(ang), np.sin(ang)
    d = np.arange(LANES) % HEAD_DIM
    cos_l = np.where(d < ROT_DIM, cos[:, d % half], 1.0)
    sa = np.where(d < half, -sin[:, d % half], 0.0)
    sb = np.where((d >= half) & (d < ROT_DIM), sin[:, d % half], 0.0)
    return tuple(jnp.asarray(t.astype(np.float32)) for t in (cos_l, sa, sb))


def _rotate(z, cos, sa, sb):
    half = ROT_DIM // 2
    parts = []
    for j in range(z.shape[1] // LANES):
        zj = z[:, j * LANES:(j + 1) * LANES]
        parts.append(zj * cos + pltpu.roll(zj, LANES - half, 1) * sa + pltpu.roll(zj, half, 1) * sb)
    return parts[0] if len(parts) == 1 else jnp.concatenate(parts, axis=1)


def _in_proj_body(x_ref, mod_ref, g_ref, w_ref, cos_ref, sa_ref, sb_ref, q_ref, k_ref, v_ref, u_ref):
    h = _rmsnorm_mod(x_ref[...], g_ref[...], mod_ref[1, 0], mod_ref[0, 0]).astype(_BF16)
    cos, sa, sb = cos_ref[...], sa_ref[...], sb_ref[...]
    cw = 512
    for c in range(Q_COLS // cw):
        z = _dot(h, w_ref[:, c * cw:(c + 1) * cw])
        q_ref[:, c * cw:(c + 1) * cw] = (_rotate(z, cos, sa, sb) * ATTN_SCALE).astype(_BF16)
    z = _dot(h, w_ref[:, Q_COLS:Q_COLS + 2 * KV_COLS])
    k_ref[...] = _rotate(z[:, :KV_COLS], cos, sa, sb)
    v_ref[...] = z[:, KV_COLS:]
    o = Q_COLS + 2 * KV_COLS
    for c in range(CONV_CH // cw):
        zv = _dot(h, w_ref[:, o + c * cw:o + (c + 1) * cw])
        zg = _dot(h, w_ref[:, o + CONV_CH + c * cw:o + CONV_CH + (c + 1) * cw])
        u_ref[:, c * cw:(c + 1) * cw] = zv * jax.nn.sigmoid(zg)


def _mod_spec(mods4, tiles_per_group):
    _, _, r, d = mods4.shape
    return pl.BlockSpec((6, 1, r, d), lambda i: (0, i // tiles_per_group, 0, 0))


def _in_proj(x2d, mods4, g_mix, w_in_b, tabs, tm):
    t, d = x2d.shape
    n_tiles = t // tm
    tiles_per_group = n_tiles // mods4.shape[1]
    pos_tiles = tabs[0].shape[0] // tm
    tab_spec = pl.BlockSpec((tm, LANES), lambda i: (i % pos_tiles, 0))
    row = lambda w: pl.BlockSpec((tm, w), lambda i: (i, 0))
    return pl.pallas_call(
        _in_proj_body,
        grid=(n_tiles,),
        in_specs=[
            row(d),
            _mod_spec(mods4, tiles_per_group),
            pl.BlockSpec((1, d), lambda i: (0, 0)),
            pl.BlockSpec((d, IN_COLS), lambda i: (0, 0)),
            tab_spec, tab_spec, tab_spec,
        ],
        out_specs=[row(Q_COLS), row(KV_COLS), row(KV_COLS), row(CONV_CH)],
        out_shape=[
            jax.ShapeDtypeStruct((t, Q_COLS), _BF16),
            jax.ShapeDtypeStruct((t, KV_COLS), _F32),
            jax.ShapeDtypeStruct((t, KV_COLS), _F32),
            jax.ShapeDtypeStruct((t, CONV_CH), _F32),
        ],
        compiler_params=_params(("arbitrary",), 56),
        name="in_proj",
    )(x2d, mods4, g_mix.reshape(1, d), w_in_b, *tabs)


def _attn_group(qg, kk, vv, sink_col, valid):
    s = lax.dot_general(qg, kk, (((1,), (1,)), ((), ())), preferred_element_type=_F32)
    if valid is not None:
        s = jnp.where(valid, s, NEG_INF)
    m = jnp.maximum(jnp.max(s, axis=-1, keepdims=True), sink_col)
    p = jnp.exp(s - m)
    den = jnp.sum(p, axis=-1, keepdims=True) + jnp.exp(sink_col - m)
    return _dot(p.astype(_BF16), vv) / den


def _sink_column(sink_ref, kh, rows_per_head):
    row = lax.broadcasted_iota(_I32, (GQA_GROUP * rows_per_head, 1), 0)
    col = jnp.full(row.shape, sink_ref[kh * GQA_GROUP + GQA_GROUP - 1], _F32)
    for g in range(GQA_GROUP - 2, -1, -1):
        col = jnp.where(row < (g + 1) * rows_per_head, sink_ref[kh * GQA_GROUP + g], col)
    return col


def _store_heads(o_ref, r0, rows, kh, o):
    for pair in range(GQA_GROUP // 2):
        both = jnp.concatenate([o[(2 * pair) * rows:(2 * pair + 1) * rows],
                                o[(2 * pair + 1) * rows:(2 * pair + 2) * rows]], axis=1)
        c0 = (kh * GQA_GROUP + 2 * pair) * HEAD_DIM
        o_ref[r0:r0 + rows, c0:c0 + 2 * HEAD_DIM] = both.astype(o_ref.dtype)


def _stack_heads(q, r0, rows, kh):
    return jnp.concatenate(
        [q[r0:r0 + rows, (kh * GQA_GROUP + g) * HEAD_DIM:(kh * GQA_GROUP + g + 1) * HEAD_DIM] for g in range(GQA_GROUP)],
        axis=0)


def _band_attn_body(sink_ref, q_ref, kc_ref, kh_ref, vc_ref, vh_ref, o_ref, *, tq):
    j = pl.program_id(1)
    q = q_ref[...]
    kall = jnp.concatenate([kh_ref[...], kc_ref[...]], axis=0)
    vall = jnp.concatenate([vh_ref[...], vc_ref[...]], axis=0)
    band = (WINDOW_CHUNKS + 1) * CHUNK
    key_chunk = lax.broadcasted_iota(_I32, (1, band), 1) // CHUNK
    for kh in range(N_KV_HEADS):
        sink_col = _sink_column(sink_ref, kh, CHUNK)
        kk = kall[:, kh * HEAD_DIM:(kh + 1) * HEAD_DIM].astype(_BF16)
        vv = vall[:, kh * HEAD_DIM:(kh + 1) * HEAD_DIM].astype(_BF16)
        for ci in range(tq // CHUNK):
            qg = _stack_heads(q, ci * CHUNK, CHUNK, kh)
            valid = None if ci >= WINDOW_CHUNKS else (j * (tq // CHUNK) + ci - WINDOW_CHUNKS + key_chunk) >= 0
            o = _attn_group(qg, kk[ci * CHUNK:ci * CHUNK + band], vv[ci * CHUNK:ci * CHUNK + band], sink_col, valid)
            _store_heads(o_ref, ci * CHUNK, CHUNK, kh, o)


def _band_attention(q, k, v, sink, batch, seq, tq):
    nq = seq // tq
    r = tq // WINDOW
    cur = lambda w: pl.BlockSpec((tq, w), lambda b, j: (b * nq + j, 0))
    halo = pl.BlockSpec((WINDOW, KV_COLS), lambda b, j: (b * nq * r + jnp.maximum(j * r - 1, 0), 0))
    return pl.pallas_call(
        functools.partial(_band_attn_body, tq=tq),
        grid=(batch, nq),
        in_specs=[pl.BlockSpec(memory_space=pltpu.SMEM), cur(Q_COLS), cur(KV_COLS), halo, cur(KV_COLS), halo],
        out_specs=cur(Q_COLS),
        out_shape=jax.ShapeDtypeStruct((batch * seq, Q_COLS), _BF16),
        compiler_params=_params(("arbitrary", "arbitrary"), 40),
        name="band_attention",
    )(sink, q, k, k, v, v)


def _cached_attn_body(sink_ref, q_ref, kn_ref, kc_ref, vn_ref, vc_ref, o_ref, *, rows):
    q = q_ref[...]
    kall = jnp.concatenate([kc_ref[0], kn_ref[...]], axis=0)
    vall = jnp.concatenate([vc_ref[0], vn_ref[...]], axis=0)
    for kh in range(N_KV_HEADS):
        o = _attn_group(_stack_heads(q, 0, rows, kh), kall[:, kh * HEAD_DIM:(kh + 1) * HEAD_DIM].astype(_BF16),
                        vall[:, kh * HEAD_DIM:(kh + 1) * HEAD_DIM].astype(_BF16), _sink_column(sink_ref, kh, rows), None)
        _store_heads(o_ref, 0, rows, kh, o)


def _cached_attention(q, k, v, cache_k, cache_v, sink, batch, rows):
    win = cache_k.shape[1]
    new = lambda w: pl.BlockSpec((rows, w), lambda b: (b, 0))
    cache = pl.BlockSpec((1, win, KV_COLS), lambda b: (b, 0, 0))
    return pl.pallas_call(
        functools.partial(_cached_attn_body, rows=rows),
        grid=(batch,),
        in_specs=[pl.BlockSpec(memory_space=pltpu.SMEM), new(Q_COLS), new(KV_COLS), cache, new(KV_COLS), cache],
        out_specs=new(Q_COLS),
        out_shape=jax.ShapeDtypeStruct((batch * rows, Q_COLS), _BF16),
        compiler_params=_params(("arbitrary",), 40),
        name="cached_attention",
    )(sink, q, k, cache_k, v, cache_v)


def _conv_body(uc_ref, uh_ref, w_ref, b_ref, g_ref, beta_ref, o_ref, win_ref, y_ref, *, tt, zero_first_halo):
    halo = uh_ref[...]
    if zero_first_halo:
        halo = jnp.where(pl.program_id(1) == 0, 0.0, halo)
    win_ref[0:CONV_HALO, :] = halo
    win_ref[CONV_HALO:, :] = uc_ref[...]
    lead = CONV_HALO - (CONV_WIDTH - 1)
    by_shift = {}
    for tap in range(CONV_WIDTH):
        by_shift.setdefault((lead + tap) % SUBLANES, []).append(tap)
    rb = min(tt, 128)
    for s in range(CONV_CH // LANES):
        cs = slice(s * LANES, (s + 1) * LANES)
        for r0 in range(0, tt, rb):
            y = b_ref[:, cs]
            for sh, taps in sorted(by_shift.items()):
                ext = SUBLANES if sh else 0
                q = None
                for tap in taps:
                    base = r0 + lead + tap - sh
                    term = w_ref[tap:tap + 1, cs] * win_ref[base:base + rb + ext, cs]
                    q = term if q is None else q + term
                y = y + (q[sh:sh + rb] if sh else q)
            y_ref[r0:r0 + rb, cs] = y
    y = y_ref[...]
    mu = jnp.mean(y, axis=-1, keepdims=True)
    var = jnp.mean(jnp.square(y - mu), axis=-1, keepdims=True)
    z = (y - mu) * lax.rsqrt(var + EPS) * g_ref[...] + beta_ref[...]
    o_ref[...] = (z * jax.nn.sigmoid(z)).astype(o_ref.dtype)


def _conv_module(u, hist, conv_w, conv_b, ln_g, ln_b, batch, seq, tt):
    nt = seq // tt
    r = tt // CONV_HALO
    cur = pl.BlockSpec((tt, CONV_CH), lambda b, j: (b * nt + j, 0))
    if hist is None:
        hist_arr = u
        halo = pl.BlockSpec((CONV_HALO, CONV_CH), lambda b, j: (b * nt * r + jnp.maximum(j * r - 1, 0), 0))
    else:
        assert nt == 1
        hist_arr = hist
        halo = pl.BlockSpec((CONV_HALO, CONV_CH), lambda b, j: (b, 0))
    vec = pl.BlockSpec((1, CONV_CH), lambda b, j: (0, 0))
    wpad = jnp.pad(conv_w.reshape(CONV_WIDTH, CONV_CH), ((0, 1), (0, 0)))
    return pl.pallas_call(
        functools.partial(_conv_body, tt=tt, zero_first_halo=hist is None),
        grid=(batch, nt),
        in_specs=[cur, halo, pl.BlockSpec((CONV_WIDTH + 1, CONV_CH), lambda b, j: (0, 0)), vec, vec, vec],
        out_specs=cur,
        out_shape=jax.ShapeDtypeStruct((batch * seq, CONV_CH), _BF16),
        scratch_shapes=[pltpu.VMEM((tt + CONV_HALO, CONV_CH), _F32), pltpu.VMEM((tt, CONV_CH), _F32)],
        compiler_params=_params(("arbitrary", "arbitrary"), 40),
        name="conv_module",
    )(u, hist_arr, wpad, conv_b.reshape(1, -1), ln_g.reshape(1, -1), ln_b.reshape(1, -1))


REC_IDX, REC_GATE, REC_RANK = 0, TOP_K, 2 * TOP_K


def _out_proj_body(x_ref, a_ref, c_ref, mod_ref, g_ref, wa_ref, wc_ref, wrh_ref, wrl_ref, br_ref, cnt_ref,
                   x1_ref, h2_ref, rec_ref, cnt_out_ref, run_ref, *, n_experts):
    tm = x_ref.shape[0]

    @pl.when(pl.program_id(0) == 0)
    def _():
        run_ref[...] = cnt_ref[...]

    o = _dot(a_ref[...], wa_ref[...]) + _dot(c_ref[...], wc_ref[...])
    x1 = x_ref[...] + mod_ref[2, 0] * o
    x1_ref[...] = x1
    h2 = _rmsnorm_mod(x1, g_ref[...], mod_ref[4, 0], mod_ref[3, 0])
    h2_ref[...] = _pack_halves(h2)

    h2_hi = h2.astype(_BF16)
    h2_lo = (h2 - h2_hi.astype(_F32)).astype(_BF16)
    logits = _dot(h2_hi, wrh_ref[...]) + (_dot(h2_lo, wrh_ref[...]) + _dot(h2_hi, wrl_ref[...])) + br_ref[...]
    lane = lax.broadcasted_iota(_I32, (tm, LANES), 1)
    work = jnp.where(lane < n_experts, logits, -jnp.inf)
    vals, hots = [], []
    rec = jnp.zeros((tm, LANES), _F32)
    for k in range(TOP_K):
        m = jnp.max(work, axis=-1, keepdims=True)
        idx = jnp.min(jnp.where(work == m, lane, LANES), axis=-1, keepdims=True)
        hot = lane == idx
        work = jnp.where(hot, -jnp.inf, work)
        vals.append(m)
        hots.append(hot)
        rec = jnp.where(lane == REC_IDX + k, idx.astype(_F32), rec)
    exps = [jnp.exp(v - vals[0]) for v in vals]
    den = exps[0]
    for e in exps[1:]:
        den = den + e
    for k in range(TOP_K):
        rec = jnp.where(lane == REC_GATE + k, exps[k] / den, rec)

    chosen = jnp.zeros((tm, LANES), _F32)
    for hot in hots:
        chosen = jnp.where(hot, 1.0, chosen)
    r_i = lax.broadcasted_iota(_I32, (tm, tm), 0)
    c_i = lax.broadcasted_iota(_I32, (tm, tm), 1)
    before = _dot(jnp.where(c_i < r_i, 1.0, 0.0).astype(_BF16), chosen.astype(_BF16)) + run_ref[...]
    for k in range(TOP_K):
        rank = jnp.sum(jnp.where(hots[k], before, 0.0), axis=-1, keepdims=True)
        rec = jnp.where(lane == REC_RANK + k, rank, rec)
    rec_ref[...] = rec
    run_ref[...] = run_ref[...] + jnp.sum(chosen, axis=0, keepdims=True)
    cnt_out_ref[...] = run_ref[...]


def _out_proj(x2d, attn, conv, mods4, g_ffn, wo_a, wo_c, wr_hi, wr_lo, br_pad, counts, tm, n_experts):
    t, d = x2d.shape
    n_tiles = t // tm
    tiles_per_group = n_tiles // mods4.shape[1]
    row = lambda w: pl.BlockSpec((tm, w), lambda i: (i, 0))
    const = lambda a, b: pl.BlockSpec((a, b), lambda i: (0, 0))
    return pl.pallas_call(
        functools.partial(_out_proj_body, n_experts=n_experts),
        grid=(n_tiles,),
        in_specs=[row(d), row(ATTN_WIDTH), row(CONV_CH), _mod_spec(mods4, tiles_per_group), const(1, d),
                  const(ATTN_WIDTH, d), const(CONV_CH, d), const(d, LANES), const(d, LANES), const(1, LANES),
                  const(1, LANES)],
        out_specs=[row(d), row(d // 2), row(LANES), const(1, LANES)],
        out_shape=[
            jax.ShapeDtypeStruct((t, d), _F32),
            jax.ShapeDtypeStruct((t, d // 2), _U32),
            jax.ShapeDtypeStruct((t, LANES), _F32),
            jax.ShapeDtypeStruct((1, LANES), _F32),
        ],
        scratch_shapes=[pltpu.VMEM((1, LANES), _F32)],
        compiler_params=_params(("arbitrary",), 56),
        name="out_proj_router",
    )(x2d, attn, conv, mods4, g_ffn.reshape(1, d), wo_a, wo_c, wr_hi, wr_lo, br_pad, counts)


def _row_copy(src, dst, sem):
    return pltpu.make_async_copy(src, dst, sem)


def _rows(ref, row0, n):
    return ref.at[pl.ds(row0, n), 0]


def _dispatch_body(pends_ref, padded_ref, nu_ref, dest_ref, hp_ref, hs_ref, xs_ref, stage_ref, zeros_ref, sems, zsem, *,
                   tt, bm, n_experts, prompt_steps):
    i = pl.program_id(0)
    n = pl.num_programs(0)
    n_blk = xs_ref.shape[0] // bm
    slot = i % 2

    def zero_block(row0):
        return _row_copy(zeros_ref, _rows(xs_ref, pl.multiple_of(row0, bm), bm), zsem)

    @pl.when(i == 0)
    def _():
        zeros_ref[...] = jnp.zeros(zeros_ref.shape, zeros_ref.dtype)
        for e in range(n_experts):
            @pl.when(padded_ref[e] > 0)
            def _():
                zero_block(pends_ref[e] - bm).start()

        def tail_start(b, carry):
            zero_block(b * bm).start()
            return carry

        def tail_wait(b, carry):
            zero_block(b * bm).wait()
            return carry

        lax.fori_loop(nu_ref[0], n_blk, tail_start, 0)
        for e in range(n_experts):
            @pl.when(padded_ref[e] > 0)
            def _():
                zero_block(pends_ref[e] - bm).wait()
        lax.fori_loop(nu_ref[0], n_blk, tail_wait, 0)

    @pl.when(i < prompt_steps)
    def _():
        stage_ref[slot] = hp_ref[...]

    @pl.when(i >= prompt_steps)
    def _():
        stage_ref[slot] = hs_ref[...]

    def issue(g, carry):
        r0 = pl.multiple_of(g * SUBLANES, SUBLANES)
        for j in range(SUBLANES):
            src = stage_ref.at[slot, pl.ds(r0 + j, 1)]
            for k in range(TOP_K):
                _row_copy(src, xs_ref.at[dest_ref[0, 0, (r0 + j) * TOP_K + k]], sems.at[slot]).start(priority=k % 2)
        return carry

    lax.fori_loop(0, tt // SUBLANES, issue, 0)

    def drain(s):
        for _ in range(TOP_K):
            _row_copy(stage_ref.at[s], _rows(xs_ref, 0, tt), sems.at[s]).wait()

    @pl.when(i > 0)
    def _():
        drain(1 - slot)

    @pl.when(i == n - 1)
    def _():
        drain(slot)


def _dispatch(h2_p, h2_s, dest, pends, padded, n_used, rows, tt, bm):
    w = h2_p.shape[1]
    tp, ts = h2_p.shape[0], h2_s.shape[0]
    n_experts = pends.shape[0]
    prompt_steps = tp // tt
    grid_spec = pltpu.PrefetchScalarGridSpec(
        num_scalar_prefetch=3,
        grid=((tp + ts) // tt,),
        in_specs=[
            pl.BlockSpec((1, 1, tt * TOP_K), lambda i, *_: (i, 0, 0), memory_space=pltpu.SMEM),
            pl.BlockSpec((tt, w), lambda i, *_: (jnp.minimum(i, prompt_steps - 1), 0)),
            pl.BlockSpec((tt, w), lambda i, *_: (jnp.maximum(i - prompt_steps, 0), 0)),
        ],
        out_specs=pl.BlockSpec(memory_space=pl.ANY),
        scratch_shapes=[pltpu.VMEM((2, tt, w), _U32), pltpu.VMEM((bm, w), _U32),
                        pltpu.SemaphoreType.DMA((2,)), pltpu.SemaphoreType.DMA(())],
    )
    return pl.pallas_call(
        functools.partial(_dispatch_body, tt=tt, bm=bm, n_experts=n_experts, prompt_steps=prompt_steps),
        grid_spec=grid_spec,
        out_shape=jax.ShapeDtypeStruct((rows, 1, w), _U32),
        compiler_params=_params(("arbitrary",), 24),
        name="moe_dispatch",
    )(pends, padded, n_used, dest.reshape((tp + ts) // tt, 1, tt * TOP_K), h2_p, h2_s)


def _ffn_body(be_ref, nu_ref, bgu_ref, bd_ref, xs_hbm, wgu_hbm, wd_hbm, o_hbm, wgu_ref, wd_ref, xbuf_ref, obuf_ref,
              xb_ref, a_ref, wsems, xsems, osems, *, bm, n_blk, chunk):
    i = pl.program_id(0)
    n_used = nu_ref[0]
    d_ff = wd_ref.shape[0]
    half = xbuf_ref.shape[2]
    slot = i % 2

    def fetch(b, s):
        return _row_copy(_rows(xs_hbm, pl.multiple_of(b * bm, bm), bm), xbuf_ref.at[s], xsems.at[s])

    def put(b, s):
        return _row_copy(obuf_ref.at[s], _rows(o_hbm, pl.multiple_of(b * bm, bm), bm), osems.at[s])

    def gu_copy(e):
        return _row_copy(wgu_hbm.at[e], wgu_ref, wsems.at[0])

    def down_copy(e):
        return _row_copy(wd_hbm.at[e], wd_ref, wsems.at[1])

    @pl.when(i == 0)
    def _():
        fetch(0, 0).start()

    @pl.when(i + 1 < n_used)
    def _():
        fetch(i + 1, 1 - slot).start()

    @pl.when(i >= 2)
    def _():
        put(i - 2, slot).wait()

    @pl.when(i < n_used)
    def _():
        e = be_ref[i]
        first = (i == 0) | (e != be_ref[jnp.maximum(i - 1, 0)])
        e_next = be_ref[jnp.minimum(i + 1, n_used - 1)]

        @pl.when(i == 0)
        def _():
            gu_copy(e).start()

        @pl.when(first)
        def _():
            down_copy(e).start()
            gu_copy(e).wait()

        fetch(i, slot).wait()
        lo, hi = _unpack_halves(xbuf_ref[slot])
        xb_ref[:, :half] = lo.astype(_BF16)
        xb_ref[:, half:] = hi.astype(_BF16)
        xb = xb_ref[...]
        for c in range(d_ff // chunk):
            cs = slice(c * chunk, (c + 1) * chunk)
            us = slice(d_ff + c * chunk, d_ff + (c + 1) * chunk)
            g = jnp.minimum(_dot(xb, wgu_ref[:, cs]) + bgu_ref[:, cs], SWIGLU_LIMIT)
            u = jnp.clip(_dot(xb, wgu_ref[:, us]) + bgu_ref[:, us], -SWIGLU_LIMIT, SWIGLU_LIMIT)
            a_ref[:, cs] = (g * jax.nn.sigmoid(SWIGLU_ALPHA * g) * (u + 1.0)).astype(_BF16)

        @pl.when(e_next != e)
        def _():
            gu_copy(e_next).start()

        @pl.when(first)
        def _():
            down_copy(e).wait()

        a = a_ref[...]
        for c in range(half // chunk):
            cs = slice(c * chunk, (c + 1) * chunk)
            hs = slice(half + c * chunk, half + (c + 1) * chunk)
            o_lo = _dot(a, wd_ref[:, cs]) + bd_ref[:, cs]
            o_hi = _dot(a, wd_ref[:, hs]) + bd_ref[:, hs]
            obuf_ref[slot, :, cs] = _pack_pair(o_lo, o_hi)

    @pl.when(i >= n_used)
    def _():
        obuf_ref[slot] = jnp.zeros(obuf_ref.shape[1:], obuf_ref.dtype)

    put(i, slot).start()

    @pl.when(i == n_blk - 1)
    def _():
        put(i, slot).wait()
        if n_blk > 1:
            put(i - 1, 1 - slot).wait()


def _expert_ffn(xs, block_e, n_used, w_gu_b, b_gu, w_down_b, b_down, bm, chunk):
    rows, _, half = xs.shape
    d = 2 * half
    n_experts, _, two_ff = w_gu_b.shape
    d_ff = two_ff // 2
    n_blk = rows // bm

    def blk(i, nu):
        return jnp.minimum(i, nu[0] - 1)

    hbm = pl.BlockSpec(memory_space=pl.ANY)
    grid_spec = pltpu.PrefetchScalarGridSpec(
        num_scalar_prefetch=2,
        grid=(n_blk,),
        in_specs=[
            pl.BlockSpec((None, 1, two_ff), lambda i, be, nu: (be[blk(i, nu)], 0, 0)),
            pl.BlockSpec((None, 1, d), lambda i, be, nu: (be[blk(i, nu)], 0, 0)),
            hbm, hbm, hbm,
        ],
        out_specs=hbm,
        scratch_shapes=[pltpu.VMEM((d, two_ff), _BF16), pltpu.VMEM((d_ff, d), _BF16),
                        pltpu.VMEM((2, bm, half), _U32), pltpu.VMEM((2, bm, half), _U32),
                        pltpu.VMEM((bm, d), _BF16), pltpu.VMEM((bm, d_ff), _BF16),
                        pltpu.SemaphoreType.DMA((2,)), pltpu.SemaphoreType.DMA((2,)), pltpu.SemaphoreType.DMA((2,))],
    )
    return pl.pallas_call(
        functools.partial(_ffn_body, bm=bm, n_blk=n_blk, chunk=chunk),
        grid_spec=grid_spec,
        out_shape=jax.ShapeDtypeStruct((rows, 1, half), _U32),
        compiler_params=_params(("arbitrary",), 56),
        name="expert_ffn",
    )(block_e, n_used, b_gu.reshape(n_experts, 1, two_ff), b_down.reshape(n_experts, 1, d), xs, w_gu_b, w_down_b)


def _combine_body(dcur_ref, dnext_ref, x1_ref, rec_ref, mod_ref, gf_ref, out_ref, y_ref, gbuf_ref, sems, *, final_norm):
    i = pl.program_id(0)
    n = pl.num_programs(0)
    tm, d = x1_ref.shape
    w = d // 2
    slot = i % 2

    def gather(dest_ref, s):
        def issue(g, carry):
            r0 = pl.multiple_of(g * SUBLANES, SUBLANES)
            for j in range(SUBLANES):
                for k in range(TOP_K):
                    _row_copy(out_ref.at[dest_ref[0, 0, (r0 + j) * TOP_K + k]], gbuf_ref.at[s, k, pl.ds(r0 + j, 1)],
                              sems.at[s]).start(priority=k % 2)
            return carry
        lax.fori_loop(0, tm // SUBLANES, issue, 0)

    @pl.when(i == 0)
    def _():
        gather(dcur_ref, 0)

    @pl.when(i + 1 < n)
    def _():
        gather(dnext_ref, 1 - slot)

    for k in range(TOP_K):
        _row_copy(_rows(out_ref, 0, tm), gbuf_ref.at[slot, k], sems.at[slot]).wait()

    rec = rec_ref[...]
    lo = hi = None
    for k in range(TOP_K):
        l, h = _unpack_halves(gbuf_ref[slot, k])
        g = rec[:, REC_GATE + k:REC_GATE + k + 1]
        lo = l * g if lo is None else lo + l * g
        hi = h * g if hi is None else hi + h * g
    gate2 = mod_ref[5, 0]
    x2l = x1_ref[:, :w] + gate2[:, :w] * lo
    x2h = x1_ref[:, w:] + gate2[:, w:] * hi
    if final_norm:
        ms = (jnp.sum(x2l * x2l, axis=-1, keepdims=True) + jnp.sum(x2h * x2h, axis=-1, keepdims=True)) * (1.0 / d)
        inv = lax.rsqrt(ms + EPS)
        x2l = x2l * inv * gf_ref[:, :w]
        x2h = x2h * inv * gf_ref[:, w:]
    y_ref[:, :w] = x2l
    y_ref[:, w:] = x2h


def _combine(x1, out_sorted, dest, rec, mods4, g_final, tm, final_norm):
    t, d = x1.shape
    n_tiles = t // tm
    tiles_per_group = n_tiles // mods4.shape[1]
    dest3 = dest.reshape(n_tiles, 1, tm * TOP_K)
    dspec = lambda nxt: pl.BlockSpec((1, 1, tm * TOP_K), lambda i: (jnp.minimum(i + nxt, n_tiles - 1), 0, 0),
                                     memory_space=pltpu.SMEM)
    return pl.pallas_call(
        functools.partial(_combine_body, final_norm=final_norm),
        grid=(n_tiles,),
        in_specs=[
            dspec(0), dspec(1),
            pl.BlockSpec((tm, d), lambda i: (i, 0)),
            pl.BlockSpec((tm, LANES), lambda i: (i, 0)),
            _mod_spec(mods4, tiles_per_group),
            pl.BlockSpec((1, d), lambda i: (0, 0)),
            pl.BlockSpec(memory_space=pl.ANY),
        ],
        out_specs=pl.BlockSpec((tm, d), lambda i: (i, 0)),
        out_shape=jax.ShapeDtypeStruct((t, d), _F32),
        scratch_shapes=[pltpu.VMEM((2, TOP_K, tm, d // 2), _U32), pltpu.SemaphoreType.DMA((2,))],
        compiler_params=_params(("arbitrary",), 40),
        name="combine_final_norm",
    )(dest3, dest3, x1, rec, mods4, g_final.reshape(1, d), out_sorted)


def _tile(n, pref):
    t = pref
    while n % t:
        t //= 2
    return t


def _moe(h2_p, h2_s, rec_all, counts, w_gu_b, b_gu, w_down_b, b_down, tt, bm, chunk):
    t_all = rec_all.shape[0]
    n_experts = w_gu_b.shape[0]
    n_blk = -(-(t_all * TOP_K) // bm) + n_experts
    cnt = counts[0, :n_experts].astype(_I32)
    padded = (cnt + bm - 1) // bm * bm
    pends = jnp.cumsum(padded).astype(_I32)
    pstarts = pends - padded
    idx = rec_all[:, REC_IDX:REC_IDX + TOP_K].astype(_I32)
    rank = rec_all[:, REC_RANK:REC_RANK + TOP_K].astype(_I32)
    experts = jnp.arange(n_experts, dtype=_I32)
    dest = rank + jnp.sum(jnp.where(idx[:, :, None] == experts, pstarts, 0), axis=-1)
    blk_row0 = jnp.arange(n_blk, dtype=_I32) * bm
    block_e = jnp.minimum(jnp.sum((pends[None, :] <= blk_row0[:, None]).astype(_I32), axis=1), n_experts - 1)
    n_used = (pends[-1:] // bm).astype(_I32)
    xs = _dispatch(h2_p, h2_s, dest, pends, padded, n_used, n_blk * bm, tt, bm)
    return _expert_ffn(xs, block_e, n_used, w_gu_b, b_gu, w_down_b, b_down, bm, chunk), dest


def kernel(x_prompt, x_sample, c_prompt, c_sample, cache_k, cache_v, state_conv, w_ada, b_ada, g_mix, w_in, attn_sink, conv_w, conv_b, conv_ln_g, conv_ln_b, w_out, g_ffn, w_router, b_router, w_gu, b_gu, w_down, b_down, g_final):
    bp, sp, d = x_prompt.shape
    bs, ss, _ = x_sample.shape
    depth = w_ada.shape[0]
    n_experts = w_router.shape[-1]
    tp, ts = bp * sp, bs * ss
    t_all = tp + ts

    tm = _tile(sp, 256)
    to = _tile(sp, 512)
    tq = _tile(sp, 256)
    tc = _tile(sp, 256)
    tt = _tile(ts, 128)
    bm = 512 if tp >= 8192 else 64
    chunk = 512
    assert tp % ts == 0 and tp % tt == 0 and sp % CHUNK == 0 and tq % WINDOW == 0 and tc % CONV_HALO == 0

    xp = x_prompt.reshape(tp, d)
    xsm = x_sample.reshape(ts, d)
    tabs_p = _rope_tables(np.arange(sp))
    tabs_s = _rope_tables(np.tile(PAST_LEN + np.arange(ss), bs))

    new_kp, new_vp, new_cp, new_ks, new_vs, new_cs = [], [], [], [], [], []
    for l in range(depth):
        w_in_b = w_in[l].astype(_BF16)
        wo_a = w_out[l, :ATTN_WIDTH].astype(_BF16)
        wo_c = w_out[l, ATTN_WIDTH:].astype(_BF16)
        w_gu_b = w_gu[l].astype(_BF16)
        w_down_b = w_down[l].astype(_BF16)
        wr_pad = jnp.pad(w_router[l], ((0, 0), (0, LANES - n_experts)))
        wr_hi = wr_pad.astype(_BF16)
        wr_lo = (wr_pad - wr_hi.astype(_F32)).astype(_BF16)
        br_pad = jnp.pad(b_router[l], (0, LANES - n_experts)).reshape(1, LANES)

        mods = _modulation(jnp.concatenate([c_prompt, c_sample], axis=0), w_ada[l], b_ada[l])
        mods_p = mods[:, :bp].reshape(6, bp, 1, d)
        mods_s = jnp.repeat(mods[:, bp:], ss, axis=1).reshape(6, 1, ts, d)

        qp, kp, vp, up = _in_proj(xp, mods_p, g_mix[l], w_in_b, tabs_p, tm)
        qs, ks, vs, us = _in_proj(xsm, mods_s, g_mix[l], w_in_b, tabs_s, ts)

        att_p = _band_attention(qp, kp, vp, attn_sink[l], bp, sp, tq)
        win = cache_k.shape[2]
        att_s = _cached_attention(qs, ks, vs, cache_k[l].reshape(bs, win, KV_COLS), cache_v[l].reshape(bs, win, KV_COLS),
                                  attn_sink[l], bs, ss)

        cv_p = _conv_module(up, None, conv_w[l], conv_b[l], conv_ln_g[l], conv_ln_b[l], bp, sp, tc)
        hist = jnp.pad(state_conv[l], ((0, 0), (CONV_HALO - (CONV_WIDTH - 1), 0), (0, 0))).reshape(bs * CONV_HALO, CONV_CH)
        cv_s = _conv_module(us, hist, conv_w[l], conv_b[l], conv_ln_g[l], conv_ln_b[l], bs, ss, ss)

        zero_counts = jnp.zeros((1, LANES), _F32)
        x1p, h2p, rec_p, counts = _out_proj(xp, att_p, cv_p, mods_p, g_ffn[l], wo_a, wo_c, wr_hi, wr_lo, br_pad,
                                            zero_counts, to, n_experts)
        x1s, h2s, rec_s, counts = _out_proj(xsm, att_s, cv_s, mods_s, g_ffn[l], wo_a, wo_c, wr_hi, wr_lo, br_pad,
                                            counts, ts, n_experts)

        out_sorted, dest = _moe(h2p, h2s, jnp.concatenate([rec_p, rec_s], axis=0), counts, w_gu_b, b_gu[l],
                                w_down_b, b_down[l], tt, bm, chunk)

        last = l == depth - 1
        xp = _combine(x1p, out_sorted, dest[:tp], rec_p, mods_p, g_final, tm, last)
        xsm = _combine(x1s, out_sorted, dest[tp:], rec_s, mods_s, g_final, ts, last)

        keep = min(WINDOW, sp)
        new_kp.append(kp.reshape(bp, sp, N_KV_HEADS, HEAD_DIM)[:, -keep:])
        new_vp.append(vp.reshape(bp, sp, N_KV_HEADS, HEAD_DIM)[:, -keep:])
        new_cp.append(up.reshape(bp, sp, CONV_CH)[:, -(CONV_WIDTH - 1):])
        new_ks.append(ks.reshape(bs, ss, N_KV_HEADS, HEAD_DIM))
        new_vs.append(vs.reshape(bs, ss, N_KV_HEADS, HEAD_DIM))
        new_cs.append(jnp.concatenate([state_conv[l], us.reshape(bs, ss, CONV_CH)], axis=1)[:, -(CONV_WIDTH - 1):])

    return (xp.reshape(bp, sp, d), xsm.reshape(bs, ss, d), jnp.stack(new_kp), jnp.stack(new_vp), jnp.stack(new_cp),
            jnp.stack(new_ks), jnp.stack(new_vs), jnp.stack(new_cs))
```

```python
import functools

import jax
import jax.numpy as jnp
import numpy as np
from jax import lax
from jax.experimental import pallas as pl
from jax.experimental.pallas import tpu as pltpu

D_MODEL = 2048
CHUNK = 64
HEAD_DIM = 64
ATTN_WIDTH = D_MODEL // 2
CONV_CH = D_MODEL - ATTN_WIDTH
N_HEADS = ATTN_WIDTH // HEAD_DIM
N_KV_HEADS = N_HEADS // 4
GQA_GROUP = N_HEADS // N_KV_HEADS
ROT_DIM = HEAD_DIM // 4
ROPE_THETA = 500000.0
WINDOW = 128
WINDOW_CHUNKS = WINDOW // CHUNK
CONV_WIDTH = 31
TOP_K = 4
SWIGLU_LIMIT = 7.0
SWIGLU_ALPHA = 1.702
EPS = 1e-5
NEG_INF = -1e30
PAST_LEN = 2048
Q_COLS = N_HEADS * HEAD_DIM
KV_COLS = N_KV_HEADS * HEAD_DIM
IN_COLS = Q_COLS + 2 * KV_COLS + 2 * CONV_CH
ATTN_SCALE = HEAD_DIM ** -0.5
assert ATTN_SCALE == 0.125

LANES = 128
SUBLANES = 8
CONV_HALO = 32
MIB = 1024 * 1024

_BF16 = jnp.bfloat16
_F32 = jnp.float32
_I32 = jnp.int32
_U32 = jnp.uint32


def _params(semantics, vmem_mib):
    return pltpu.CompilerParams(dimension_semantics=semantics, vmem_limit_bytes=vmem_mib * MIB)


def _dot(a, b):
    return jnp.dot(a, b, preferred_element_type=_F32)


def _pack_pair(lo, hi):
    lo = lax.bitcast_convert_type(lo.astype(_BF16).astype(_F32), _U32)
    hi = lax.bitcast_convert_type(hi.astype(_BF16).astype(_F32), _U32)
    return (lo >> 16) | (hi & jnp.uint32(0xFFFF0000))


def _pack_halves(x):
    w = x.shape[1] // 2
    return _pack_pair(x[:, :w], x[:, w:])


def _unpack_halves(p):
    return (lax.bitcast_convert_type(p << 16, _F32), lax.bitcast_convert_type(p & jnp.uint32(0xFFFF0000), _F32))


def _rmsnorm_mod(x, g, scale, shift):
    ms = jnp.mean(x * x, axis=-1, keepdims=True)
    return (x * lax.rsqrt(ms + EPS) * g) * (1.0 + scale) + shift


def _ada_body(c_ref, w_ref, b_ref, o_ref):
    c = c_ref[...]
    s = (c * jax.nn.sigmoid(c)).astype(_BF16)
    o_ref[...] = _dot(s, w_ref[...].astype(_BF16)) + b_ref[...]


def _modulation(c, w_ada, b_ada):
    n, d = c.shape
    cols = w_ada.shape[1]
    tn = 1024
    per = d // tn
    return pl.pallas_call(
        _ada_body,
        grid=(cols // tn,),
        in_specs=[
            pl.BlockSpec((n, d), lambda j: (0, 0)),
            pl.BlockSpec((d, tn), lambda j: (0, j)),
            pl.BlockSpec((1, tn), lambda j: (0, j)),
        ],
        out_specs=pl.BlockSpec((None, n, tn), lambda j: (j // per, 0, j % per)),
        out_shape=jax.ShapeDtypeStruct((cols // d, n, d), _F32),
        compiler_params=_params(("arbitrary",), 40),
        name="modulation",
    )(c, w_ada, b_ada.reshape(1, cols))


def _rope_tables(pos):
    half = ROT_DIM // 2
    inv_freq = np.power(np.float32(ROPE_THETA), -np.arange(0, ROT_DIM, 2, dtype=np.float32) / np.float32(ROT_DIM))
    ang = (pos.astype(np.float32)[:, None] * inv_freq[None, :].astype(np.float32)).astype(np.float64)
    cos, sin = np.cos(ang), np.sin(ang)
    d = np.arange(LANES) % HEAD_DIM
    cos_l = np.where(d < ROT_DIM, cos[:, d % half], 1.0)
    sa = np.where(d < half, -sin[:, d % half], 0.0)
    sb = np.where((d >= half) & (d < ROT_DIM), sin[:, d % half], 0.0)
    return tuple(jnp.asarray(t.astype(np.float32)) for t in (cos_l, sa, sb))


def _rotate(z, cos, sa, sb):
    half = ROT_DIM // 2
    parts = []
    for j in range(z.shape[1] // LANES):
        zj = z[:, j * LANES:(j + 1) * LANES]
        parts.append(zj * cos + pltpu.roll(zj, LANES - half, 1) * sa + pltpu.roll(zj, half, 1) * sb)
    return parts[0] if len(parts) == 1 else jnp.concatenate(parts, axis=1)


def _in_proj_body(x_ref, mod_ref, g_ref, w_ref, cos_ref, sa_ref, sb_ref, q_ref, k_ref, v_ref, u_ref):
    h = _rmsnorm_mod(x_ref[...], g_ref[...], mod_ref[1, 0], mod_ref[0, 0]).astype(_BF16)
    cos, sa, sb = cos_ref[...], sa_ref[...], sb_ref[...]
    cw = 512
    for c in range(Q_COLS // cw):
        z = _dot(h, w_ref[:, c * cw:(c + 1) * cw])
        q_ref[:, c * cw:(c + 1) * cw] = (_rotate(z, cos, sa, sb) * ATTN_SCALE).astype(_BF16)
    z = _dot(h, w_ref[:, Q_COLS:Q_COLS + 2 * KV_COLS])
    k_ref[...] = _rotate(z[:, :KV_COLS], cos, sa, sb)
    v_ref[...] = z[:, KV_COLS:]
    o = Q_COLS + 2 * KV_COLS
    for c in range(CONV_CH // cw):
        zv = _dot(h, w_ref[:, o + c * cw:o + (c + 1) * cw])
        zg = _dot(h, w_ref[:, o + CONV_CH + c * cw:o + CONV_CH + (c + 1) * cw])
        u_ref[:, c * cw:(c + 1) * cw] = zv * jax.nn.sigmoid(zg)


def _mod_spec(mods4, tiles_per_group):
    _, _, r, d = mods4.shape
    return pl.BlockSpec((6, 1, r, d), lambda i: (0, i // tiles_per_group, 0, 0))


def _in_proj(x2d, mods4, g_mix, w_in_b, tabs, tm):
    t, d = x2d.shape
    n_tiles = t // tm
    tiles_per_group = n_tiles // mods4.shape[1]
    pos_tiles = tabs[0].shape[0] // tm
    tab_spec = pl.BlockSpec((tm, LANES), lambda i: (i % pos_tiles, 0))
    row = lambda w: pl.BlockSpec((tm, w), lambda i: (i, 0))
    return pl.pallas_call(
        _in_proj_body,
        grid=(n_tiles,),
        in_specs=[
            row(d),
            _mod_spec(mods4, tiles_per_group),
            pl.BlockSpec((1, d), lambda i: (0, 0)),
            pl.BlockSpec((d, IN_COLS), lambda i: (0, 0)),
            tab_spec, tab_spec, tab_spec,
        ],
        out_specs=[row(Q_COLS), row(KV_COLS), row(KV_COLS), row(CONV_CH)],
        out_shape=[
            jax.ShapeDtypeStruct((t, Q_COLS), _BF16),
            jax.ShapeDtypeStruct((t, KV_COLS), _F32),
            jax.ShapeDtypeStruct((t, KV_COLS), _F32),
            jax.ShapeDtypeStruct((t, CONV_CH), _F32),
        ],
        compiler_params=_params(("arbitrary",), 56),
        name="in_proj",
    )(x2d, mods4, g_mix.reshape(1, d), w_in_b, *tabs)


def _attn_group(qg, kk, vv, sink_col, valid):
    s = lax.dot_general(qg, kk, (((1,), (1,)), ((), ())), preferred_element_type=_F32)
    if valid is not None:
        s = jnp.where(valid, s, NEG_INF)
    m = jnp.maximum(jnp.max(s, axis=-1, keepdims=True), sink_col)
    p = jnp.exp(s - m)
    den = jnp.sum(p, axis=-1, keepdims=True) + jnp.exp(sink_col - m)
    return _dot(p.astype(_BF16), vv) / den


def _sink_column(sink_ref, kh, rows_per_head):
    row = lax.broadcasted_iota(_I32, (GQA_GROUP * rows_per_head, 1), 0)
    col = jnp.full(row.shape, sink_ref[kh * GQA_GROUP + GQA_GROUP - 1], _F32)
    for g in range(GQA_GROUP - 2, -1, -1):
        col = jnp.where(row < (g + 1) * rows_per_head, sink_ref[kh * GQA_GROUP + g], col)
    return col


def _store_heads(o_ref, r0, rows, kh, o):
    for pair in range(GQA_GROUP // 2):
        both = jnp.concatenate([o[(2 * pair) * rows:(2 * pair + 1) * rows],
                                o[(2 * pair + 1) * rows:(2 * pair + 2) * rows]], axis=1)
        c0 = (kh * GQA_GROUP + 2 * pair) * HEAD_DIM
        o_ref[r0:r0 + rows, c0:c0 + 2 * HEAD_DIM] = both.astype(o_ref.dtype)


def _stack_heads(q, r0, rows, kh):
    return jnp.concatenate(
        [q[r0:r0 + rows, (kh * GQA_GROUP + g) * HEAD_DIM:(kh * GQA_GROUP + g + 1) * HEAD_DIM] for g in range(GQA_GROUP)],
        axis=0)


def _rider_rows(weight2d, n_steps):
    rows, cols = weight2d.shape
    per = rows // n_steps
    if rows % n_steps or per % 16 or per * cols * 4 > 8 * MIB:
        return None
    return per


def _band_attn_body(sink_ref, q_ref, kc_ref, kh_ref, vc_ref, vh_ref, *rest, tq):
    o_ref = rest[-1] if len(rest) == 1 else rest[1]
    if len(rest) == 3:
        rest[2][...] = rest[0][...].astype(_BF16)
    j = pl.program_id(1)
    q = q_ref[...]
    kall = jnp.concatenate([kh_ref[...], kc_ref[...]], axis=0)
    vall = jnp.concatenate([vh_ref[...], vc_ref[...]], axis=0)
    band = (WINDOW_CHUNKS + 1) * CHUNK
    key_chunk = lax.broadcasted_iota(_I32, (1, band), 1) // CHUNK
    for kh in range(N_KV_HEADS):
        sink_col = _sink_column(sink_ref, kh, CHUNK)
        kk = kall[:, kh * HEAD_DIM:(kh + 1) * HEAD_DIM].astype(_BF16)
        vv = vall[:, kh * HEAD_DIM:(kh + 1) * HEAD_DIM].astype(_BF16)
        for ci in range(tq // CHUNK):
            qg = _stack_heads(q, ci * CHUNK, CHUNK, kh)
            valid = None if ci >= WINDOW_CHUNKS else (j * (tq // CHUNK) + ci - WINDOW_CHUNKS + key_chunk) >= 0
            o = _attn_group(qg, kk[ci * CHUNK:ci * CHUNK + band], vv[ci * CHUNK:ci * CHUNK + band], sink_col, valid)
            _store_heads(o_ref, ci * CHUNK, CHUNK, kh, o)


def _band_attention(q, k, v, sink, batch, seq, tq, rider):
    nq = seq // tq
    r = tq // WINDOW
    cur = lambda w: pl.BlockSpec((tq, w), lambda b, j: (b * nq + j, 0))
    halo = pl.BlockSpec((WINDOW, KV_COLS), lambda b, j: (b * nq * r + jnp.maximum(j * r - 1, 0), 0))
    in_specs = [pl.BlockSpec(memory_space=pltpu.SMEM), cur(Q_COLS), cur(KV_COLS), halo, cur(KV_COLS), halo]
    out_specs = [cur(Q_COLS)]
    out_shape = [jax.ShapeDtypeStruct((batch * seq, Q_COLS), _BF16)]
    args = [sink, q, k, k, v, v]
    per = _rider_rows(rider, batch * nq)
    if per is not None:
        slab = pl.BlockSpec((per, rider.shape[1]), lambda b, j: (b * nq + j, 0))
        in_specs.append(slab)
        out_specs.append(slab)
        out_shape.append(jax.ShapeDtypeStruct(rider.shape, _BF16))
        args.append(rider)
    res = pl.pallas_call(
        functools.partial(_band_attn_body, tq=tq),
        grid=(batch, nq),
        in_specs=in_specs,
        out_specs=out_specs,
        out_shape=out_shape,
        compiler_params=_params(("arbitrary", "arbitrary"), 48),
        name="band_attention",
    )(*args)
    return (res[0], res[1]) if per is not None else (res[0], rider.astype(_BF16))


def _cached_attn_body(sink_ref, q_ref, kn_ref, kc_ref, vn_ref, vc_ref, o_ref, *, rows):
    q = q_ref[...]
    kall = jnp.concatenate([kc_ref[0], kn_ref[...]], axis=0)
    vall = jnp.concatenate([vc_ref[0], vn_ref[...]], axis=0)
    for kh in range(N_KV_HEADS):
        o = _attn_group(_stack_heads(q, 0, rows, kh), kall[:, kh * HEAD_DIM:(kh + 1) * HEAD_DIM].astype(_BF16),
                        vall[:, kh * HEAD_DIM:(kh + 1) * HEAD_DIM].astype(_BF16), _sink_column(sink_ref, kh, rows), None)
        _store_heads(o_ref, 0, rows, kh, o)


def _cached_attention(q, k, v, cache_k, cache_v, sink, batch, rows):
    win = cache_k.shape[1]
    new = lambda w: pl.BlockSpec((rows, w), lambda b: (b, 0))
    cache = pl.BlockSpec((1, win, KV_COLS), lambda b: (b, 0, 0))
    return pl.pallas_call(
        functools.partial(_cached_attn_body, rows=rows),
        grid=(batch,),
        in_specs=[pl.BlockSpec(memory_space=pltpu.SMEM), new(Q_COLS), new(KV_COLS), cache, new(KV_COLS), cache],
        out_specs=new(Q_COLS),
        out_shape=jax.ShapeDtypeStruct((batch * rows, Q_COLS), _BF16),
        compiler_params=_params(("arbitrary",), 40),
        name="cached_attention",
    )(sink, q, k, cache_k, v, cache_v)


def _conv_body(uc_ref, uh_ref, w_ref, b_ref, g_ref, beta_ref, *rest, tt, zero_first_halo):
    win_ref, y_ref = rest[-2:]
    o_ref = rest[0] if len(rest) == 3 else rest[1]
    if len(rest) == 5:
        rest[2][...] = rest[0][...].astype(_BF16)
    halo = uh_ref[...]
    if zero_first_halo:
        halo = jnp.where(pl.program_id(1) == 0, 0.0, halo)
    win_ref[0:CONV_HALO, :] = halo
    win_ref[CONV_HALO:, :] = uc_ref[...]
    lead = CONV_HALO - (CONV_WIDTH - 1)
    by_shift = {}
    for tap in range(CONV_WIDTH):
        by_shift.setdefault((lead + tap) % SUBLANES, []).append(tap)
    rb = min(tt, 128)
    for s in range(CONV_CH // LANES):
        cs = slice(s * LANES, (s + 1) * LANES)
        for r0 in range(0, tt, rb):
            y = b_ref[:, cs]
            for sh, taps in sorted(by_shift.items()):
                ext = SUBLANES if sh else 0
                q = None
                for tap in taps:
                    base = r0 + lead + tap - sh
                    term = w_ref[tap:tap + 1, cs] * win_ref[base:base + rb + ext, cs]
                    q = term if q is None else q + term
                y = y + (q[sh:sh + rb] if sh else q)
            y_ref[r0:r0 + rb, cs] = y
    y = y_ref[...]
    mu = jnp.mean(y, axis=-1, keepdims=True)
    var = jnp.mean(jnp.square(y - mu), axis=-1, keepdims=True)
    z = (y - mu) * lax.rsqrt(var + EPS) * g_ref[...] + beta_ref[...]
    o_ref[...] = (z * jax.nn.sigmoid(z)).astype(o_ref.dtype)


def _conv_module(u, hist, conv_w, conv_b, ln_g, ln_b, batch, seq, tt, rider=None):
    nt = seq // tt
    r = tt // CONV_HALO
    cur = pl.BlockSpec((tt, CONV_CH), lambda b, j: (b * nt + j, 0))
    if hist is None:
        hist_arr = u
        halo = pl.BlockSpec((CONV_HALO, CONV_CH), lambda b, j: (b * nt * r + jnp.maximum(j * r - 1, 0), 0))
    else:
        assert nt == 1
        hist_arr = hist
        halo = pl.BlockSpec((CONV_HALO, CONV_CH), lambda b, j: (b, 0))
    vec = pl.BlockSpec((1, CONV_CH), lambda b, j: (0, 0))
    wpad = jnp.pad(conv_w.reshape(CONV_WIDTH, CONV_CH), ((0, 1), (0, 0)))
    in_specs = [cur, halo, pl.BlockSpec((CONV_WIDTH + 1, CONV_CH), lambda b, j: (0, 0)), vec, vec, vec]
    out_specs = [cur]
    out_shape = [jax.ShapeDtypeStruct((batch * seq, CONV_CH), _BF16)]
    args = [u, hist_arr, wpad, conv_b.reshape(1, -1), ln_g.reshape(1, -1), ln_b.reshape(1, -1)]
    per = None if rider is None else _rider_rows(rider, batch * nt)
    if per is not None:
        slab = pl.BlockSpec((per, rider.shape[1]), lambda b, j: (b * nt + j, 0))
        in_specs.append(slab)
        out_specs.append(slab)
        out_shape.append(jax.ShapeDtypeStruct(rider.shape, _BF16))
        args.append(rider)
    res = pl.pallas_call(
        functools.partial(_conv_body, tt=tt, zero_first_halo=hist is None),
        grid=(batch, nt),
        in_specs=in_specs,
        out_specs=out_specs,
        out_shape=out_shape,
        scratch_shapes=[pltpu.VMEM((tt + CONV_HALO, CONV_CH), _F32), pltpu.VMEM((tt, CONV_CH), _F32)],
        compiler_params=_params(("arbitrary", "arbitrary"), 40),
        name="conv_module",
    )(*args)
    if rider is None:
        return res[0]
    return (res[0], res[1]) if per is not None else (res[0], rider.astype(_BF16))


REC_IDX, REC_GATE, REC_RANK = 0, TOP_K, 2 * TOP_K


def _out_proj_body(x_ref, a_ref, c_ref, mod_ref, g_ref, wa_ref, wc_ref, wrh_ref, wrl_ref, br_ref, cnt_ref,
                   x1_ref, h2_ref, rec_ref, cnt_out_ref, run_ref, *, n_experts):
    tm = x_ref.shape[0]

    @pl.when(pl.program_id(0) == 0)
    def _():
        run_ref[...] = cnt_ref[...]

    o = _dot(a_ref[...], wa_ref[...]) + _dot(c_ref[...], wc_ref[...])
    x1 = x_ref[...] + mod_ref[2, 0] * o
    x1_ref[...] = x1
    h2 = _rmsnorm_mod(x1, g_ref[...], mod_ref[4, 0], mod_ref[3, 0])
    h2_ref[...] = _pack_halves(h2)

    h2_hi = h2.astype(_BF16)
    h2_lo = (h2 - h2_hi.astype(_F32)).astype(_BF16)
    logits = _dot(h2_hi, wrh_ref[...]) + (_dot(h2_lo, wrh_ref[...]) + _dot(h2_hi, wrl_ref[...])) + br_ref[...]
    lane = lax.broadcasted_iota(_I32, (tm, LANES), 1)
    work = jnp.where(lane < n_experts, logits, -jnp.inf)
    vals, hots = [], []
    rec = jnp.zeros((tm, LANES), _F32)
    for k in range(TOP_K):
        m = jnp.max(work, axis=-1, keepdims=True)
        idx = jnp.min(jnp.where(work == m, lane, LANES), axis=-1, keepdims=True)
        hot = lane == idx
        work = jnp.where(hot, -jnp.inf, work)
        vals.append(m)
        hots.append(hot)
        rec = jnp.where(lane == REC_IDX + k, idx.astype(_F32), rec)
    exps = [jnp.exp(v - vals[0]) for v in vals]
    den = exps[0]
    for e in exps[1:]:
        den = den + e
    for k in range(TOP_K):
        rec = jnp.where(lane == REC_GATE + k, exps[k] / den, rec)

    chosen = jnp.zeros((tm, LANES), _F32)
    for hot in hots:
        chosen = jnp.where(hot, 1.0, chosen)
    r_i = lax.broadcasted_iota(_I32, (tm, tm), 0)
    c_i = lax.broadcasted_iota(_I32, (tm, tm), 1)
    before = _dot(jnp.where(c_i < r_i, 1.0, 0.0).astype(_BF16), chosen.astype(_BF16)) + run_ref[...]
    for k in range(TOP_K):
        rank = jnp.sum(jnp.where(hots[k], before, 0.0), axis=-1, keepdims=True)
        rec = jnp.where(lane == REC_RANK + k, rank, rec)
    rec_ref[...] = rec
    run_ref[...] = run_ref[...] + jnp.sum(chosen, axis=0, keepdims=True)
    cnt_out_ref[...] = run_ref[...]


def _out_proj(x2d, attn, conv, mods4, g_ffn, wo_a, wo_c, wr_hi, wr_lo, br_pad, counts, tm, n_experts):
    t, d = x2d.shape
    n_tiles = t // tm
    tiles_per_group = n_tiles // mods4.shape[1]
    row = lambda w: pl.BlockSpec((tm, w), lambda i: (i, 0))
    const = lambda a, b: pl.BlockSpec((a, b), lambda i: (0, 0))
    return pl.pallas_call(
        functools.partial(_out_proj_body, n_experts=n_experts),
        grid=(n_tiles,),
        in_specs=[row(d), row(ATTN_WIDTH), row(CONV_CH), _mod_spec(mods4, tiles_per_group), const(1, d),
                  const(ATTN_WIDTH, d), const(CONV_CH, d), const(d, LANES), const(d, LANES), const(1, LANES),
                  const(1, LANES)],
        out_specs=[row(d), row(d // 2), row(LANES), const(1, LANES)],
        out_shape=[
            jax.ShapeDtypeStruct((t, d), _F32),
            jax.ShapeDtypeStruct((t, d // 2), _U32),
            jax.ShapeDtypeStruct((t, LANES), _F32),
            jax.ShapeDtypeStruct((1, LANES), _F32),
        ],
        scratch_shapes=[pltpu.VMEM((1, LANES), _F32)],
        compiler_params=_params(("arbitrary",), 56),
        name="out_proj_router",
    )(x2d, attn, conv, mods4, g_ffn.reshape(1, d), wo_a, wo_c, wr_hi, wr_lo, br_pad, counts)


def _row_copy(src, dst, sem):
    return pltpu.make_async_copy(src, dst, sem)


def _rows(ref, row0, n):
    return ref.at[pl.ds(row0, n), 0]


def _dispatch_body(pends_ref, padded_ref, nu_ref, dest_ref, hp_ref, hs_ref, xs_ref, stage_ref, zeros_ref, sems, zsem, *,
                   tt, bm, n_experts, prompt_steps):
    i = pl.program_id(0)
    n = pl.num_programs(0)
    n_blk = xs_ref.shape[0] // bm
    slot = i % 2

    def zero_block(row0):
        return _row_copy(zeros_ref, _rows(xs_ref, pl.multiple_of(row0, bm), bm), zsem)

    @pl.when(i == 0)
    def _():
        zeros_ref[...] = jnp.zeros(zeros_ref.shape, zeros_ref.dtype)
        for e in range(n_experts):
            @pl.when(padded_ref[e] > 0)
            def _():
                zero_block(pends_ref[e] - bm).start()

        def tail_start(b, carry):
            zero_block(b * bm).start()
            return carry

        def tail_wait(b, carry):
            zero_block(b * bm).wait()
            return carry

        lax.fori_loop(nu_ref[0], n_blk, tail_start, 0)
        for e in range(n_experts):
            @pl.when(padded_ref[e] > 0)
            def _():
                zero_block(pends_ref[e] - bm).wait()
        lax.fori_loop(nu_ref[0], n_blk, tail_wait, 0)

    @pl.when(i < prompt_steps)
    def _():
        stage_ref[slot] = hp_ref[...]

    @pl.when(i >= prompt_steps)
    def _():
        stage_ref[slot] = hs_ref[...]

    def issue(g, carry):
        r0 = pl.multiple_of(g * SUBLANES, SUBLANES)
        for j in range(SUBLANES):
            src = stage_ref.at[slot, pl.ds(r0 + j, 1)]
            for k in range(TOP_K):
                _row_copy(src, xs_ref.at[dest_ref[0, 0, (r0 + j) * TOP_K + k]], sems.at[slot]).start(priority=k % 2)
        return carry

    lax.fori_loop(0, tt // SUBLANES, issue, 0)

    def drain(s):
        for _ in range(TOP_K):
            _row_copy(stage_ref.at[s], _rows(xs_ref, 0, tt), sems.at[s]).wait()

    @pl.when(i > 0)
    def _():
        drain(1 - slot)

    @pl.when(i == n - 1)
    def _():
        drain(slot)


def _dispatch(h2_p, h2_s, dest, pends, padded, n_used, rows, tt, bm):
    w = h2_p.shape[1]
    tp, ts = h2_p.shape[0], h2_s.shape[0]
    n_experts = pends.shape[0]
    prompt_steps = tp // tt
    grid_spec = pltpu.PrefetchScalarGridSpec(
        num_scalar_prefetch=3,
        grid=((tp + ts) // tt,),
        in_specs=[
            pl.BlockSpec((1, 1, tt * TOP_K), lambda i, *_: (i, 0, 0), memory_space=pltpu.SMEM),
            pl.BlockSpec((tt, w), lambda i, *_: (jnp.minimum(i, prompt_steps - 1), 0)),
            pl.BlockSpec((tt, w), lambda i, *_: (jnp.maximum(i - prompt_steps, 0), 0)),
        ],
        out_specs=pl.BlockSpec(memory_space=pl.ANY),
        scratch_shapes=[pltpu.VMEM((2, tt, w), _U32), pltpu.VMEM((bm, w), _U32),
                        pltpu.SemaphoreType.DMA((2,)), pltpu.SemaphoreType.DMA(())],
    )
    return pl.pallas_call(
        functools.partial(_dispatch_body, tt=tt, bm=bm, n_experts=n_experts, prompt_steps=prompt_steps),
        grid_spec=grid_spec,
        out_shape=jax.ShapeDtypeStruct((rows, 1, w), _U32),
        compiler_params=_params(("arbitrary",), 24),
        name="moe_dispatch",
    )(pends, padded, n_used, dest.reshape((tp + ts) // tt, 1, tt * TOP_K), h2_p, h2_s)


def _ffn_body(be_ref, nu_ref, bgu_ref, bd_ref, xs_hbm, wgu_hbm, wd_hbm, o_hbm, wgu_ref, wd_ref, xbuf_ref, obuf_ref,
              xb_ref, a_ref, wsems, xsems, osems, *, bm, n_blk, chunk):
    i = pl.program_id(0)
    n_used = nu_ref[0]
    d_ff = wd_ref.shape[0]
    half = xbuf_ref.shape[2]
    slot = i % 2

    def fetch(b, s):
        return _row_copy(_rows(xs_hbm, pl.multiple_of(b * bm, bm), bm), xbuf_ref.at[s], xsems.at[s])

    def put(b, s):
        return _row_copy(obuf_ref.at[s], _rows(o_hbm, pl.multiple_of(b * bm, bm), bm), osems.at[s])

    def gu_copy(e):
        return _row_copy(wgu_hbm.at[e], wgu_ref, wsems.at[0])

    def down_copy(e):
        return _row_copy(wd_hbm.at[e], wd_ref, wsems.at[1])

    @pl.when(i == 0)
    def _():
        fetch(0, 0).start()

    @pl.when(i + 1 < n_used)
    def _():
        fetch(i + 1, 1 - slot).start()

    @pl.when(i >= 2)
    def _():
        put(i - 2, slot).wait()

    @pl.when(i < n_used)
    def _():
        e = be_ref[i]
        first = (i == 0) | (e != be_ref[jnp.maximum(i - 1, 0)])
        e_next = be_ref[jnp.minimum(i + 1, n_used - 1)]

        @pl.when(i == 0)
        def _():
            gu_copy(e).start()

        @pl.when(first)
        def _():
            down_copy(e).start()
            gu_copy(e).wait()

        fetch(i, slot).wait()
        lo, hi = _unpack_halves(xbuf_ref[slot])
        xb_ref[:, :half] = lo.astype(_BF16)
        xb_ref[:, half:] = hi.astype(_BF16)
        xb = xb_ref[...]
        for c in range(d_ff // chunk):
            cs = slice(c * chunk, (c + 1) * chunk)
            us = slice(d_ff + c * chunk, d_ff + (c + 1) * chunk)
            g = jnp.minimum(_dot(xb, wgu_ref[:, cs]) + bgu_ref[:, cs], SWIGLU_LIMIT)
            u = jnp.clip(_dot(xb, wgu_ref[:, us]) + bgu_ref[:, us], -SWIGLU_LIMIT, SWIGLU_LIMIT)
            a_ref[:, cs] = (g * jax.nn.sigmoid(SWIGLU_ALPHA * g) * (u + 1.0)).astype(_BF16)

        @pl.when(e_next != e)
        def _():
            gu_copy(e_next).start()

        @pl.when(first)
        def _():
            down_copy(e).wait()

        a = a_ref[...]
        for c in range(half // chunk):
            cs = slice(c * chunk, (c + 1) * chunk)
            hs = slice(half + c * chunk, half + (c + 1) * chunk)
            o_lo = _dot(a, wd_ref[:, cs]) + bd_ref[:, cs]
            o_hi = _dot(a, wd_ref[:, hs]) + bd_ref[:, hs]
            obuf_ref[slot, :, cs] = _pack_pair(o_lo, o_hi)

    @pl.when(i >= n_used)
    def _():
        obuf_ref[slot] = jnp.zeros(obuf_ref.shape[1:], obuf_ref.dtype)

    put(i, slot).start()

    @pl.when(i == n_blk - 1)
    def _():
        put(i, slot).wait()
        if n_blk > 1:
            put(i - 1, 1 - slot).wait()


def _expert_ffn(xs, block_e, n_used, w_gu_b, b_gu, w_down_b, b_down, bm, chunk):
    rows, _, half = xs.shape
    d = 2 * half
    n_experts, _, two_ff = w_gu_b.shape
    d_ff = two_ff // 2
    n_blk = rows // bm

    def blk(i, nu):
        return jnp.minimum(i, nu[0] - 1)

    hbm = pl.BlockSpec(memory_space=pl.ANY)
    grid_spec = pltpu.PrefetchScalarGridSpec(
        num_scalar_prefetch=2,
        grid=(n_blk,),
        in_specs=[
            pl.BlockSpec((None, 1, two_ff), lambda i, be, nu: (be[blk(i, nu)], 0, 0)),
            pl.BlockSpec((None, 1, d), lambda i, be, nu: (be[blk(i, nu)], 0, 0)),
            hbm, hbm, hbm,
        ],
        out_specs=hbm,
        scratch_shapes=[pltpu.VMEM((d, two_ff), _BF16), pltpu.VMEM((d_ff, d), _BF16),
                        pltpu.VMEM((2, bm, half), _U32), pltpu.VMEM((2, bm, half), _U32),
                        pltpu.VMEM((bm, d), _BF16), pltpu.VMEM((bm, d_ff), _BF16),
                        pltpu.SemaphoreType.DMA((2,)), pltpu.SemaphoreType.DMA((2,)), pltpu.SemaphoreType.DMA((2,))],
    )
    return pl.pallas_call(
        functools.partial(_ffn_body, bm=bm, n_blk=n_blk, chunk=chunk),
        grid_spec=grid_spec,
        out_shape=jax.ShapeDtypeStruct((rows, 1, half), _U32),
        compiler_params=_params(("arbitrary",), 56),
        name="expert_ffn",
    )(block_e, n_used, b_gu.reshape(n_experts, 1, two_ff), b_down.reshape(n_experts, 1, d), xs, w_gu_b, w_down_b)


def _combine_body(dcur_ref, dnext_ref, x1_ref, rec_ref, mod_ref, gf_ref, out_ref, y_ref, gbuf_ref, sems, *, final_norm):
    i = pl.program_id(0)
    n = pl.num_programs(0)
    tm, d = x1_ref.shape
    w = d // 2
    slot = i % 2

    def gather(dest_ref, s):
        def issue(g, carry):
            r0 = pl.multiple_of(g * SUBLANES, SUBLANES)
            for j in range(SUBLANES):
                for k in range(TOP_K):
                    _row_copy(out_ref.at[dest_ref[0, 0, (r0 + j) * TOP_K + k]], gbuf_ref.at[s, k, pl.ds(r0 + j, 1)],
                              sems.at[s]).start(priority=k % 2)
            return carry
        lax.fori_loop(0, tm // SUBLANES, issue, 0)

    @pl.when(i == 0)
    def _():
        gather(dcur_ref, 0)

    @pl.when(i + 1 < n)
    def _():
        gather(dnext_ref, 1 - slot)

    for k in range(TOP_K):
        _row_copy(_rows(out_ref, 0, tm), gbuf_ref.at[slot, k], sems.at[slot]).wait()

    rec = rec_ref[...]
    lo = hi = None
    for k in range(TOP_K):
        l, h = _unpack_halves(gbuf_ref[slot, k])
        g = rec[:, REC_GATE + k:REC_GATE + k + 1]
        lo = l * g if lo is None else lo + l * g
        hi = h * g if hi is None else hi + h * g
    gate2 = mod_ref[5, 0]
    x2l = x1_ref[:, :w] + gate2[:, :w] * lo
    x2h = x1_ref[:, w:] + gate2[:, w:] * hi
    if final_norm:
        ms = (jnp.sum(x2l * x2l, axis=-1, keepdims=True) + jnp.sum(x2h * x2h, axis=-1, keepdims=True)) * (1.0 / d)
        inv = lax.rsqrt(ms + EPS)
        x2l = x2l * inv * gf_ref[:, :w]
        x2h = x2h * inv * gf_ref[:, w:]
    y_ref[:, :w] = x2l
    y_ref[:, w:] = x2h


def _combine(x1, out_sorted, dest, rec, mods4, g_final, tm, final_norm):
    t, d = x1.shape
    n_tiles = t // tm
    tiles_per_group = n_tiles // mods4.shape[1]
    dest3 = dest.reshape(n_tiles, 1, tm * TOP_K)
    dspec = lambda nxt: pl.BlockSpec((1, 1, tm * TOP_K), lambda i: (jnp.minimum(i + nxt, n_tiles - 1), 0, 0),
                                     memory_space=pltpu.SMEM)
    return pl.pallas_call(
        functools.partial(_combine_body, final_norm=final_norm),
        grid=(n_tiles,),
        in_specs=[
            dspec(0), dspec(1),
            pl.BlockSpec((tm, d), lambda i: (i, 0)),
            pl.BlockSpec((tm, LANES), lambda i: (i, 0)),
            _mod_spec(mods4, tiles_per_group),
            pl.BlockSpec((1, d), lambda i: (0, 0)),
            pl.BlockSpec(memory_space=pl.ANY),
        ],
        out_specs=pl.BlockSpec((tm, d), lambda i: (i, 0)),
        out_shape=jax.ShapeDtypeStruct((t, d), _F32),
        scratch_shapes=[pltpu.VMEM((2, TOP_K, tm, d // 2), _U32), pltpu.SemaphoreType.DMA((2,))],
        compiler_params=_params(("arbitrary",), 40),
        name="combine_final_norm",
    )(dest3, dest3, x1, rec, mods4, g_final.reshape(1, d), out_sorted)


def _tile(n, pref):
    t = pref
    while n % t:
        t //= 2
    return t


def _moe(h2_p, h2_s, rec_all, counts, w_gu_b, b_gu, w_down_b, b_down, tt, bm, chunk):
    t_all = rec_all.shape[0]
    n_experts = w_gu_b.shape[0]
    n_blk = -(-(t_all * TOP_K) // bm) + n_experts
    cnt = counts[0, :n_experts].astype(_I32)
    padded = (cnt + bm - 1) // bm * bm
    pends = jnp.cumsum(padded).astype(_I32)
    pstarts = pends - padded
    idx = rec_all[:, REC_IDX:REC_IDX + TOP_K].astype(_I32)
    rank = rec_all[:, REC_RANK:REC_RANK + TOP_K].astype(_I32)
    experts = jnp.arange(n_experts, dtype=_I32)
    dest = rank + jnp.sum(jnp.where(idx[:, :, None] == experts, pstarts, 0), axis=-1)
    blk_row0 = jnp.arange(n_blk, dtype=_I32) * bm
    block_e = jnp.minimum(jnp.sum((pends[None, :] <= blk_row0[:, None]).astype(_I32), axis=1), n_experts - 1)
    n_used = (pends[-1:] // bm).astype(_I32)
    xs = _dispatch(h2_p, h2_s, dest, pends, padded, n_used, n_blk * bm, tt, bm)
    return _expert_ffn(xs, block_e, n_used, w_gu_b, b_gu, w_down_b, b_down, bm, chunk), dest


def kernel(x_prompt, x_sample, c_prompt, c_sample, cache_k, cache_v, state_conv, w_ada, b_ada, g_mix, w_in, attn_sink, conv_w, conv_b, conv_ln_g, conv_ln_b, w_out, g_ffn, w_router, b_router, w_gu, b_gu, w_down, b_down, g_final):
    bp, sp, d = x_prompt.shape
    bs, ss, _ = x_sample.shape
    depth = w_ada.shape[0]
    n_experts = w_router.shape[-1]
    tp, ts = bp * sp, bs * ss
    t_all = tp + ts

    tm = _tile(sp, 256)
    to = _tile(sp, 512)
    tq = _tile(sp, 256)
    tc = _tile(sp, 256)
    tt = _tile(ts, 128)
    bm = 512 if tp >= 8192 else 64
    chunk = 512
    assert tp % ts == 0 and tp % tt == 0 and sp % CHUNK == 0 and tq % WINDOW == 0 and tc % CONV_HALO == 0

    xp = x_prompt.reshape(tp, d)
    xsm = x_sample.reshape(ts, d)
    tabs_p = _rope_tables(np.arange(sp))
    tabs_s = _rope_tables(np.tile(PAST_LEN + np.arange(ss), bs))

    new_kp, new_vp, new_cp, new_ks, new_vs, new_cs = [], [], [], [], [], []
    for l in range(depth):
        w_in_b = w_in[l].astype(_BF16)
        wo_a = w_out[l, :ATTN_WIDTH].astype(_BF16)
        wo_c = w_out[l, ATTN_WIDTH:].astype(_BF16)
        wr_pad = jnp.pad(w_router[l], ((0, 0), (0, LANES - n_experts)))
        wr_hi = wr_pad.astype(_BF16)
        wr_lo = (wr_pad - wr_hi.astype(_F32)).astype(_BF16)
        br_pad = jnp.pad(b_router[l], (0, LANES - n_experts)).reshape(1, LANES)

        mods = _modulation(jnp.concatenate([c_prompt, c_sample], axis=0), w_ada[l], b_ada[l])
        mods_p = mods[:, :bp].reshape(6, bp, 1, d)
        mods_s = jnp.repeat(mods[:, bp:], ss, axis=1).reshape(6, 1, ts, d)

        qp, kp, vp, up = _in_proj(xp, mods_p, g_mix[l], w_in_b, tabs_p, tm)
        qs, ks, vs, us = _in_proj(xsm, mods_s, g_mix[l], w_in_b, tabs_s, ts)

        d_ff = w_down.shape[2]
        att_p, w_gu_b = _band_attention(qp, kp, vp, attn_sink[l], bp, sp, tq, w_gu[l].reshape(n_experts * d, 2 * d_ff))
        w_gu_b = w_gu_b.reshape(n_experts, d, 2 * d_ff)
        win = cache_k.shape[2]
        att_s = _cached_attention(qs, ks, vs, cache_k[l].reshape(bs, win, KV_COLS), cache_v[l].reshape(bs, win, KV_COLS),
                                  attn_sink[l], bs, ss)

        cv_p, w_down_b = _conv_module(up, None, conv_w[l], conv_b[l], conv_ln_g[l], conv_ln_b[l], bp, sp, tc,
                                      w_down[l].reshape(n_experts * d_ff, d))
        w_down_b = w_down_b.reshape(n_experts, d_ff, d)
        hist = jnp.pad(state_conv[l], ((0, 0), (CONV_HALO - (CONV_WIDTH - 1), 0), (0, 0))).reshape(bs * CONV_HALO, CONV_CH)
        cv_s = _conv_module(us, hist, conv_w[l], conv_b[l], conv_ln_g[l], conv_ln_b[l], bs, ss, ss)

        zero_counts = jnp.zeros((1, LANES), _F32)
        x1p, h2p, rec_p, counts = _out_proj(xp, att_p, cv_p, mods_p, g_ffn[l], wo_a, wo_c, wr_hi, wr_lo, br_pad,
                                            zero_counts, to, n_experts)
        x1s, h2s, rec_s, counts = _out_proj(xsm, att_s, cv_s, mods_s, g_ffn[l], wo_a, wo_c, wr_hi, wr_lo, br_pad,
                                            counts, ts, n_experts)

        out_sorted, dest = _moe(h2p, h2s, jnp.concatenate([rec_p, rec_s], axis=0), counts, w_gu_b, b_gu[l],
                                w_down_b, b_down[l], tt, bm, chunk)

        last = l == depth - 1
        xp = _combine(x1p, out_sorted, dest[:tp], rec_p, mods_p, g_final, tm, last)
        xsm = _combine(x1s, out_sorted, dest[tp:], rec_s, mods_s, g_final, ts, last)

        keep = min(WINDOW, sp)
        new_kp.append(kp.reshape(bp, sp, N_KV_HEADS, HEAD_DIM)[:, -keep:])
        new_vp.append(vp.reshape(bp, sp, N_KV_HEADS, HEAD_DIM)[:, -keep:])
        new_cp.append(up.reshape(bp, sp, CONV_CH)[:, -(CONV_WIDTH - 1):])
        new_ks.append(ks.reshape(bs, ss, N_KV_HEADS, HEAD_DIM))
        new_vs.append(vs.reshape(bs, ss, N_KV_HEADS, HEAD_DIM))
        new_cs.append(jnp.concatenate([state_conv[l], us.reshape(bs, ss, CONV_CH)], axis=1)[:, -(CONV_WIDTH - 1):])

    return (xp.reshape(bp, sp, d), xsm.reshape(bs, ss, d), jnp.stack(new_kp), jnp.stack(new_vp), jnp.stack(new_cp),
            jnp.stack(new_ks), jnp.stack(new_vs), jnp.stack(new_cs))
```

```python
import functools

import jax
import jax.numpy as jnp
import numpy as np
from jax import lax
from jax.experimental import pallas as pl
from jax.experimental.pallas import tpu as pltpu

D_MODEL = 2048
CHUNK = 64
HEAD_DIM = 64
ATTN_WIDTH = D_MODEL // 2
CONV_CH = D_MODEL - ATTN_WIDTH
N_HEADS = ATTN_WIDTH // HEAD_DIM
N_KV_HEADS = N_HEADS // 4
GQA_GROUP = N_HEADS // N_KV_HEADS
ROT_DIM = HEAD_DIM // 4
ROPE_THETA = 500000.0
WINDOW = 128
WINDOW_CHUNKS = WINDOW // CHUNK
CONV_WIDTH = 31
TOP_K = 4
SWIGLU_LIMIT = 7.0
SWIGLU_ALPHA = 1.702
EPS = 1e-5
NEG_INF = -1e30
PAST_LEN = 2048
Q_COLS = N_HEADS * HEAD_DIM
KV_COLS = N_KV_HEADS * HEAD_DIM
IN_COLS = Q_COLS + 2 * KV_COLS + 2 * CONV_CH
ATTN_SCALE = HEAD_DIM ** -0.5
assert ATTN_SCALE == 0.125

LANES = 128
SUBLANES = 8
CONV_HALO = 32
MIB = 1024 * 1024

_BF16 = jnp.bfloat16
_F32 = jnp.float32
_I32 = jnp.int32
_U32 = jnp.uint32


def _params(semantics, vmem_mib):
    return pltpu.CompilerParams(dimension_semantics=semantics, vmem_limit_bytes=vmem_mib * MIB)


def _dot(a, b):
    return jnp.dot(a, b, preferred_element_type=_F32)


def _pack_pair(lo, hi):
    lo = lax.bitcast_convert_type(lo.astype(_BF16).astype(_F32), _U32)
    hi = lax.bitcast_convert_type(hi.astype(_BF16).astype(_F32), _U32)
    return (lo >> 16) | (hi & jnp.uint32(0xFFFF0000))


def _pack_halves(x):
    w = x.shape[1] // 2
    return _pack_pair(x[:, :w], x[:, w:])


def _unpack_halves(p):
    return (lax.bitcast_convert_type(p << 16, _F32), lax.bitcast_convert_type(p & jnp.uint32(0xFFFF0000), _F32))


def _rmsnorm_mod(x, g, scale, shift):
    ms = jnp.mean(x * x, axis=-1, keepdims=True)
    return (x * lax.rsqrt(ms + EPS) * g) * (1.0 + scale) + shift


def _ada_body(c_ref, w_ref, b_ref, o_ref):
    c = c_ref[...]
    s = (c * jax.nn.sigmoid(c)).astype(_BF16)
    o_ref[...] = _dot(s, w_ref[...].astype(_BF16)) + b_ref[...]


def _modulation(c, w_ada, b_ada):
    n, d = c.shape
    cols = w_ada.shape[1]
    tn = 1024
    per = d // tn
    return pl.pallas_call(
        _ada_body,
        grid=(cols // tn,),
        in_specs=[
            pl.BlockSpec((n, d), lambda j: (0, 0)),
            pl.BlockSpec((d, tn), lambda j: (0, j)),
            pl.BlockSpec((1, tn), lambda j: (0, j)),
        ],
        out_specs=pl.BlockSpec((None, n, tn), lambda j: (j // per, 0, j % per)),
        out_shape=jax.ShapeDtypeStruct((cols // d, n, d), _F32),
        compiler_params=_params(("arbitrary",), 40),
        name="modulation",
    )(c, w_ada, b_ada.reshape(1, cols))


def _rope_tables(pos):
    half = ROT_DIM // 2
    inv_freq = np.power(np.float32(ROPE_THETA), -np.arange(0, ROT_DIM, 2, dtype=np.float32) / np.float32(ROT_DIM))
    ang = (pos.astype(np.float32)[:, None] * inv_freq[None, :].astype(np.float32)).astype(np.float64)
    cos, sin = np.cos(ang), np.sin(ang)
    d = np.arange(LANES) % HEAD_DIM
    cos_l = np.where(d < ROT_DIM, cos[:, d % half], 1.0)
    sa = np.where(d < half, -sin[:, d % half], 0.0)
    sb = np.where((d >= half) & (d < ROT_DIM), sin[:, d % half], 0.0)
    return tuple(jnp.asarray(t.astype(np.float32)) for t in (cos_l, sa, sb))


def _rotate(z, cos, sa, sb):
    half = ROT_DIM // 2
    parts = []
    for j in range(z.shape[1] // LANES):
        zj = z[:, j * LANES:(j + 1) * LANES]
        parts.append(zj * cos + pltpu.roll(zj, LANES - half, 1) * sa + pltpu.roll(zj, half, 1) * sb)
    return parts[0] if len(parts) == 1 else jnp.concatenate(parts, axis=1)


def _in_proj_body(x_ref, mod_ref, g_ref, w_ref, cos_ref, sa_ref, sb_ref, q_ref, k_ref, v_ref, u_ref):
    h = _rmsnorm_mod(x_ref[...], g_ref[...], mod_ref[1, 0], mod_ref[0, 0]).astype(_BF16)
    cos, sa, sb = cos_ref[...], sa_ref[...], sb_ref[...]
    cw = 512
    for c in range(Q_COLS // cw):
        z = _dot(h, w_ref[:, c * cw:(c + 1) * cw])
        q_ref[:, c * cw:(c + 1) * cw] = (_rotate(z, cos, sa, sb) * ATTN_SCALE).astype(_BF16)
    z = _dot(h, w_ref[:, Q_COLS:Q_COLS + 2 * KV_COLS])
    k_ref[...] = _rotate(z[:, :KV_COLS], cos, sa, sb)
    v_ref[...] = z[:, KV_COLS:]
    o = Q_COLS + 2 * KV_COLS
    for c in range(CONV_CH // cw):
        zv = _dot(h, w_ref[:, o + c * cw:o + (c + 1) * cw])
        zg = _dot(h, w_ref[:, o + CONV_CH + c * cw:o + CONV_CH + (c + 1) * cw])
        u_ref[:, c * cw:(c + 1) * cw] = zv * jax.nn.sigmoid(zg)


def _mod_spec(mods4, tiles_per_group):
    _, _, r, d = mods4.shape
    return pl.BlockSpec((6, 1, r, d), lambda i: (0, i // tiles_per_group, 0, 0))


def _in_proj(x2d, mods4, g_mix, w_in_b, tabs, tm):
    t, d = x2d.shape
    n_tiles = t // tm
    tiles_per_group = n_tiles // mods4.shape[1]
    pos_tiles = tabs[0].shape[0] // tm
    tab_spec = pl.BlockSpec((tm, LANES), lambda i: (i % pos_tiles, 0))
    row = lambda w: pl.BlockSpec((tm, w), lambda i: (i, 0))
    return pl.pallas_call(
        _in_proj_body,
        grid=(n_tiles,),
        in_specs=[
            row(d),
            _mod_spec(mods4, tiles_per_group),
            pl.BlockSpec((1, d), lambda i: (0, 0)),
            pl.BlockSpec((d, IN_COLS), lambda i: (0, 0)),
            tab_spec, tab_spec, tab_spec,
        ],
        out_specs=[row(Q_COLS), row(KV_COLS), row(KV_COLS), row(CONV_CH)],
        out_shape=[
            jax.ShapeDtypeStruct((t, Q_COLS), _BF16),
            jax.ShapeDtypeStruct((t, KV_COLS), _F32),
            jax.ShapeDtypeStruct((t, KV_COLS), _F32),
            jax.ShapeDtypeStruct((t, CONV_CH), _F32),
        ],
        compiler_params=_params(("arbitrary",), 56),
        name="in_proj",
    )(x2d, mods4, g_mix.reshape(1, d), w_in_b, *tabs)


def _attn_group(qg, kk, vv, sink_col, valid):
    s = lax.dot_general(qg, kk, (((1,), (1,)), ((), ())), preferred_element_type=_F32)
    if valid is not None:
        s = jnp.where(valid, s, NEG_INF)
    m = jnp.maximum(jnp.max(s, axis=-1, keepdims=True), sink_col)
    p = jnp.exp(s - m)
    den = jnp.sum(p, axis=-1, keepdims=True) + jnp.exp(sink_col - m)
    return _dot(p.astype(_BF16), vv) / den


def _sink_column(sink_ref, kh, rows_per_head):
    row = lax.broadcasted_iota(_I32, (GQA_GROUP * rows_per_head, 1), 0)
    col = jnp.full(row.shape, sink_ref[kh * GQA_GROUP + GQA_GROUP - 1], _F32)
    for g in range(GQA_GROUP - 2, -1, -1):
        col = jnp.where(row < (g + 1) * rows_per_head, sink_ref[kh * GQA_GROUP + g], col)
    return col


def _store_heads(o_ref, r0, rows, kh, o):
    for pair in range(GQA_GROUP // 2):
        both = jnp.concatenate([o[(2 * pair) * rows:(2 * pair + 1) * rows],
                                o[(2 * pair + 1) * rows:(2 * pair + 2) * rows]], axis=1)
        c0 = (kh * GQA_GROUP + 2 * pair) * HEAD_DIM
        o_ref[r0:r0 + rows, c0:c0 + 2 * HEAD_DIM] = both.astype(o_ref.dtype)


def _stack_heads(q, r0, rows, kh):
    return jnp.concatenate(
        [q[r0:r0 + rows, (kh * GQA_GROUP + g) * HEAD_DIM:(kh * GQA_GROUP + g + 1) * HEAD_DIM] for g in range(GQA_GROUP)],
        axis=0)


def _rider_rows(weight2d, n_steps):
    rows, cols = weight2d.shape
    per = rows // n_steps
    if rows % n_steps or per % 16 or per * cols * 4 > 8 * MIB:
        return None
    return per


def _band_attn_body(sink_ref, q_ref, kc_ref, kh_ref, vc_ref, vh_ref, *rest, tq):
    o_ref = rest[-1] if len(rest) == 1 else rest[1]
    if len(rest) == 3:
        rest[2][...] = rest[0][...].astype(_BF16)
    j = pl.program_id(1)
    q = q_ref[...]
    kall = jnp.concatenate([kh_ref[...], kc_ref[...]], axis=0)
    vall = jnp.concatenate([vh_ref[...], vc_ref[...]], axis=0)
    band = (WINDOW_CHUNKS + 1) * CHUNK
    key_chunk = lax.broadcasted_iota(_I32, (1, band), 1) // CHUNK
    for kh in range(N_KV_HEADS):
        sink_col = _sink_column(sink_ref, kh, CHUNK)
        kk = kall[:, kh * HEAD_DIM:(kh + 1) * HEAD_DIM].astype(_BF16)
        vv = vall[:, kh * HEAD_DIM:(kh + 1) * HEAD_DIM].astype(_BF16)
        for ci in range(tq // CHUNK):
            qg = _stack_heads(q, ci * CHUNK, CHUNK, kh)
            valid = None if ci >= WINDOW_CHUNKS else (j * (tq // CHUNK) + ci - WINDOW_CHUNKS + key_chunk) >= 0
            o = _attn_group(qg, kk[ci * CHUNK:ci * CHUNK + band], vv[ci * CHUNK:ci * CHUNK + band], sink_col, valid)
            _store_heads(o_ref, ci * CHUNK, CHUNK, kh, o)


def _band_attention(q, k, v, sink, batch, seq, tq, rider):
    nq = seq // tq
    r = tq // WINDOW
    cur = lambda w: pl.BlockSpec((tq, w), lambda b, j: (b * nq + j, 0))
    halo = pl.BlockSpec((WINDOW, KV_COLS), lambda b, j: (b * nq * r + jnp.maximum(j * r - 1, 0), 0))
    in_specs = [pl.BlockSpec(memory_space=pltpu.SMEM), cur(Q_COLS), cur(KV_COLS), halo, cur(KV_COLS), halo]
    out_specs = [cur(Q_COLS)]
    out_shape = [jax.ShapeDtypeStruct((batch * seq, Q_COLS), _BF16)]
    args = [sink, q, k, k, v, v]
    per = _rider_rows(rider, batch * nq)
    if per is not None:
        slab = pl.BlockSpec((per, rider.shape[1]), lambda b, j: (b * nq + j, 0))
        in_specs.append(slab)
        out_specs.append(slab)
        out_shape.append(jax.ShapeDtypeStruct(rider.shape, _BF16))
        args.append(rider)
    res = pl.pallas_call(
        functools.partial(_band_attn_body, tq=tq),
        grid=(batch, nq),
        in_specs=in_specs,
        out_specs=out_specs,
        out_shape=out_shape,
        compiler_params=_params(("arbitrary", "arbitrary"), 48),
        name="band_attention",
    )(*args)
    return (res[0], res[1]) if per is not None else (res[0], rider.astype(_BF16))


def _cached_attn_body(sink_ref, q_ref, kn_ref, kc_ref, vn_ref, vc_ref, o_ref, *, rows):
    q = q_ref[...]
    kall = jnp.concatenate([kc_ref[0], kn_ref[...]], axis=0)
    vall = jnp.concatenate([vc_ref[0], vn_ref[...]], axis=0)
    for kh in range(N_KV_HEADS):
        o = _attn_group(_stack_heads(q, 0, rows, kh), kall[:, kh * HEAD_DIM:(kh + 1) * HEAD_DIM].astype(_BF16),
                        vall[:, kh * HEAD_DIM:(kh + 1) * HEAD_DIM].astype(_BF16), _sink_column(sink_ref, kh, rows), None)
        _store_heads(o_ref, 0, rows, kh, o)


def _cached_attention(q, k, v, cache_k, cache_v, sink, batch, rows):
    win = cache_k.shape[1]
    new = lambda w: pl.BlockSpec((rows, w), lambda b: (b, 0))
    cache = pl.BlockSpec((1, win, KV_COLS), lambda b: (b, 0, 0))
    return pl.pallas_call(
        functools.partial(_cached_attn_body, rows=rows),
        grid=(batch,),
        in_specs=[pl.BlockSpec(memory_space=pltpu.SMEM), new(Q_COLS), new(KV_COLS), cache, new(KV_COLS), cache],
        out_specs=new(Q_COLS),
        out_shape=jax.ShapeDtypeStruct((batch * rows, Q_COLS), _BF16),
        compiler_params=_params(("arbitrary",), 40),
        name="cached_attention",
    )(sink, q, k, cache_k, v, cache_v)


def _conv_body(uc_ref, uh_ref, w_ref, b_ref, g_ref, beta_ref, *rest, tt, zero_first_halo):
    win_ref, y_ref = rest[-2:]
    o_ref = rest[0] if len(rest) == 3 else rest[1]
    if len(rest) == 5:
        rest[2][...] = rest[0][...].astype(_BF16)
    halo = uh_ref[...]
    if zero_first_halo:
        halo = jnp.where(pl.program_id(1) == 0, 0.0, halo)
    win_ref[0:CONV_HALO, :] = halo
    win_ref[CONV_HALO:, :] = uc_ref[...]
    lead = CONV_HALO - (CONV_WIDTH - 1)
    by_shift = {}
    for tap in range(CONV_WIDTH):
        by_shift.setdefault((lead + tap) % SUBLANES, []).append(tap)
    rb = min(tt, 128)
    for s in range(CONV_CH // LANES):
        cs = slice(s * LANES, (s + 1) * LANES)
        for r0 in range(0, tt, rb):
            y = b_ref[:, cs]
            for sh, taps in sorted(by_shift.items()):
                ext = SUBLANES if sh else 0
                q = None
                for tap in taps:
                    base = r0 + lead + tap - sh
                    term = w_ref[tap:tap + 1, cs] * win_ref[base:base + rb + ext, cs]
                    q = term if q is None else q + term
                y = y + (q[sh:sh + rb] if sh else q)
            y_ref[r0:r0 + rb, cs] = y
    y = y_ref[...]
    mu = jnp.mean(y, axis=-1, keepdims=True)
    var = jnp.mean(jnp.square(y - mu), axis=-1, keepdims=True)
    z = (y - mu) * lax.rsqrt(var + EPS) * g_ref[...] + beta_ref[...]
    o_ref[...] = (z * jax.nn.sigmoid(z)).astype(o_ref.dtype)


def _conv_module(u, hist, conv_w, conv_b, ln_g, ln_b, batch, seq, tt, rider=None):
    nt = seq // tt
    r = tt // CONV_HALO
    cur = pl.BlockSpec((tt, CONV_CH), lambda b, j: (b * nt + j, 0))
    if hist is None:
        hist_arr = u
        halo = pl.BlockSpec((CONV_HALO, CONV_CH), lambda b, j: (b * nt * r + jnp.maximum(j * r - 1, 0), 0))
    else:
        assert nt == 1
        hist_arr = hist
        halo = pl.BlockSpec((CONV_HALO, CONV_CH), lambda b, j: (b, 0))
    vec = pl.BlockSpec((1, CONV_CH), lambda b, j: (0, 0))
    wpad = jnp.pad(conv_w.reshape(CONV_WIDTH, CONV_CH), ((0, 1), (0, 0)))
    in_specs = [cur, halo, pl.BlockSpec((CONV_WIDTH + 1, CONV_CH), lambda b, j: (0, 0)), vec, vec, vec]
    out_specs = [cur]
    out_shape = [jax.ShapeDtypeStruct((batch * seq, CONV_CH), _BF16)]
    args = [u, hist_arr, wpad, conv_b.reshape(1, -1), ln_g.reshape(1, -1), ln_b.reshape(1, -1)]
    per = None if rider is None else _rider_rows(rider, batch * nt)
    if per is not None:
        slab = pl.BlockSpec((per, rider.shape[1]), lambda b, j: (b * nt + j, 0))
        in_specs.append(slab)
        out_specs.append(slab)
        out_shape.append(jax.ShapeDtypeStruct(rider.shape, _BF16))
        args.append(rider)
    res = pl.pallas_call(
        functools.partial(_conv_body, tt=tt, zero_first_halo=hist is None),
        grid=(batch, nt),
        in_specs=in_specs,
        out_specs=out_specs,
        out_shape=out_shape,
        scratch_shapes=[pltpu.VMEM((tt + CONV_HALO, CONV_CH), _F32), pltpu.VMEM((tt, CONV_CH), _F32)],
        compiler_params=_params(("arbitrary", "arbitrary"), 40),
        name="conv_module",
    )(*args)
    if rider is None:
        return res[0]
    return (res[0], res[1]) if per is not None else (res[0], rider.astype(_BF16))


REC_IDX, REC_GATE, REC_RANK = 0, TOP_K, 2 * TOP_K


def _out_proj_body(x_ref, a_ref, c_ref, mod_ref, g_ref, wa_ref, wc_ref, wrh_ref, wrl_ref, br_ref, cnt_ref,
                   x1_ref, h2_ref, rec_ref, cnt_out_ref, run_ref, *, n_experts):
    tm = x_ref.shape[0]

    @pl.when(pl.program_id(0) == 0)
    def _():
        run_ref[...] = cnt_ref[...]

    o = _dot(a_ref[...], wa_ref[...]) + _dot(c_ref[...], wc_ref[...])
    x1 = x_ref[...] + mod_ref[2, 0] * o
    x1_ref[...] = x1
    h2 = _rmsnorm_mod(x1, g_ref[...], mod_ref[4, 0], mod_ref[3, 0])
    h2_ref[...] = _pack_halves(h2)

    h2_hi = h2.astype(_BF16)
    h2_lo = (h2 - h2_hi.astype(_F32)).astype(_BF16)
    logits = _dot(h2_hi, wrh_ref[...]) + (_dot(h2_lo, wrh_ref[...]) + _dot(h2_hi, wrl_ref[...])) + br_ref[...]
    lane = lax.broadcasted_iota(_I32, (tm, LANES), 1)
    work = jnp.where(lane < n_experts, logits, -jnp.inf)
    vals, hots = [], []
    rec = jnp.zeros((tm, LANES), _F32)
    for k in range(TOP_K):
        m = jnp.max(work, axis=-1, keepdims=True)
        idx = jnp.min(jnp.where(work == m, lane, LANES), axis=-1, keepdims=True)
        hot = lane == idx
        work = jnp.where(hot, -jnp.inf, work)
        vals.append(m)
        hots.append(hot)
        rec = jnp.where(lane == REC_IDX + k, idx.astype(_F32), rec)
    exps = [jnp.exp(v - vals[0]) for v in vals]
    den = exps[0]
    for e in exps[1:]:
        den = den + e
    for k in range(TOP_K):
        rec = jnp.where(lane == REC_GATE + k, exps[k] / den, rec)

    chosen = jnp.zeros((tm, LANES), _F32)
    for hot in hots:
        chosen = jnp.where(hot, 1.0, chosen)
    r_i = lax.broadcasted_iota(_I32, (tm, tm), 0)
    c_i = lax.broadcasted_iota(_I32, (tm, tm), 1)
    before = _dot(jnp.where(c_i < r_i, 1.0, 0.0).astype(_BF16), chosen.astype(_BF16)) + run_ref[...]
    for k in range(TOP_K):
        rank = jnp.sum(jnp.where(hots[k], before, 0.0), axis=-1, keepdims=True)
        rec = jnp.where(lane == REC_RANK + k, rank, rec)
    rec_ref[...] = rec
    run_ref[...] = run_ref[...] + jnp.sum(chosen, axis=0, keepdims=True)
    cnt_out_ref[...] = run_ref[...]


def _out_proj(x2d, attn, conv, mods4, g_ffn, wo_a, wo_c, wr_hi, wr_lo, br_pad, counts, tm, n_experts):
    t, d = x2d.shape
    n_tiles = t // tm
    tiles_per_group = n_tiles // mods4.shape[1]
    row = lambda w: pl.BlockSpec((tm, w), lambda i: (i, 0))
    const = lambda a, b: pl.BlockSpec((a, b), lambda i: (0, 0))
    return pl.pallas_call(
        functools.partial(_out_proj_body, n_experts=n_experts),
        grid=(n_tiles,),
        in_specs=[row(d), row(ATTN_WIDTH), row(CONV_CH), _mod_spec(mods4, tiles_per_group), const(1, d),
                  const(ATTN_WIDTH, d), const(CONV_CH, d), const(d, LANES), const(d, LANES), const(1, LANES),
                  const(1, LANES)],
        out_specs=[row(d), row(d // 2), row(LANES), const(1, LANES)],
        out_shape=[
            jax.ShapeDtypeStruct((t, d), _F32),
            jax.ShapeDtypeStruct((t, d // 2), _U32),
            jax.ShapeDtypeStruct((t, LANES), _F32),
            jax.ShapeDtypeStruct((1, LANES), _F32),
        ],
        scratch_shapes=[pltpu.VMEM((1, LANES), _F32)],
        compiler_params=_params(("arbitrary",), 56),
        name="out_proj_router",
    )(x2d, attn, conv, mods4, g_ffn.reshape(1, d), wo_a, wo_c, wr_hi, wr_lo, br_pad, counts)


def _row_copy(src, dst, sem):
    return pltpu.make_async_copy(src, dst, sem)


def _rows(ref, row0, n):
    return ref.at[pl.ds(row0, n), 0]


def _dispatch_body(pends_ref, padded_ref, nu_ref, dest_ref, hp_ref, hs_ref, xs_ref, stage_ref, zeros_ref, sems, zsem, *,
                   tt, bm, n_experts, prompt_steps):
    i = pl.program_id(0)
    n = pl.num_programs(0)
    n_blk = xs_ref.shape[0] // bm
    slot = i % 2

    def zero_block(row0):
        return _row_copy(zeros_ref, _rows(xs_ref, pl.multiple_of(row0, bm), bm), zsem)

    @pl.when(i == 0)
    def _():
        zeros_ref[...] = jnp.zeros(zeros_ref.shape, zeros_ref.dtype)
        for e in range(n_experts):
            @pl.when(padded_ref[e] > 0)
            def _():
                zero_block(pends_ref[e] - bm).start()

        def tail_start(b, carry):
            zero_block(b * bm).start()
            return carry

        def tail_wait(b, carry):
            zero_block(b * bm).wait()
            return carry

        lax.fori_loop(nu_ref[0], n_blk, tail_start, 0)
        for e in range(n_experts):
            @pl.when(padded_ref[e] > 0)
            def _():
                zero_block(pends_ref[e] - bm).wait()
        lax.fori_loop(nu_ref[0], n_blk, tail_wait, 0)

    @pl.when(i < prompt_steps)
    def _():
        stage_ref[slot] = hp_ref[...]

    @pl.when(i >= prompt_steps)
    def _():
        stage_ref[slot] = hs_ref[...]

    for r in range(tt):
        src = stage_ref.at[slot, pl.ds(r, 1)]
        for k in range(TOP_K):
            _row_copy(src, xs_ref.at[dest_ref[0, 0, r * TOP_K + k]], sems.at[slot]).start(priority=k % 2)

    def drain(s):
        for _ in range(TOP_K):
            _row_copy(stage_ref.at[s], _rows(xs_ref, 0, tt), sems.at[s]).wait()

    @pl.when(i > 0)
    def _():
        drain(1 - slot)

    @pl.when(i == n - 1)
    def _():
        drain(slot)


def _dispatch(h2_p, h2_s, dest, pends, padded, n_used, rows, tt, bm):
    w = h2_p.shape[1]
    tp, ts = h2_p.shape[0], h2_s.shape[0]
    n_experts = pends.shape[0]
    prompt_steps = tp // tt
    grid_spec = pltpu.PrefetchScalarGridSpec(
        num_scalar_prefetch=3,
        grid=((tp + ts) // tt,),
        in_specs=[
            pl.BlockSpec((1, 1, tt * TOP_K), lambda i, *_: (i, 0, 0), memory_space=pltpu.SMEM),
            pl.BlockSpec((tt, w), lambda i, *_: (jnp.minimum(i, prompt_steps - 1), 0)),
            pl.BlockSpec((tt, w), lambda i, *_: (jnp.maximum(i - prompt_steps, 0), 0)),
        ],
        out_specs=pl.BlockSpec(memory_space=pl.ANY),
        scratch_shapes=[pltpu.VMEM((2, tt, w), _U32), pltpu.VMEM((bm, w), _U32),
                        pltpu.SemaphoreType.DMA((2,)), pltpu.SemaphoreType.DMA(())],
    )
    return pl.pallas_call(
        functools.partial(_dispatch_body, tt=tt, bm=bm, n_experts=n_experts, prompt_steps=prompt_steps),
        grid_spec=grid_spec,
        out_shape=jax.ShapeDtypeStruct((rows, 1, w), _U32),
        compiler_params=_params(("arbitrary",), 24),
        name="moe_dispatch",
    )(pends, padded, n_used, dest.reshape((tp + ts) // tt, 1, tt * TOP_K), h2_p, h2_s)


def _ffn_body(be_ref, nu_ref, valid_ref, bgu_ref, bd_ref, xs_hbm, wgu_hbm, wd_hbm, o_hbm, wgu_ref, wd_ref, xbuf_ref, obuf_ref,
              xb_ref, a_ref, wsems, xsems, osems, *, bm, n_blk, chunk):
    i = pl.program_id(0)
    n_used = nu_ref[0]
    d_ff = wd_ref.shape[0]
    half = xbuf_ref.shape[2]
    slot = i % 2

    def fetch(b, s):
        return _row_copy(_rows(xs_hbm, pl.multiple_of(b * bm, bm), bm), xbuf_ref.at[s], xsems.at[s])

    def put(b, s):
        return _row_copy(obuf_ref.at[s], _rows(o_hbm, pl.multiple_of(b * bm, bm), bm), osems.at[s])

    def gu_copy(e):
        return _row_copy(wgu_hbm.at[e], wgu_ref, wsems.at[0])

    def down_copy(e):
        return _row_copy(wd_hbm.at[e], wd_ref, wsems.at[1])

    @pl.when(i == 0)
    def _():
        fetch(0, 0).start()

    @pl.when(i + 1 < n_used)
    def _():
        fetch(i + 1, 1 - slot).start()

    @pl.when(i >= 2)
    def _():
        put(i - 2, slot).wait()

    @pl.when(i < n_used)
    def _():
        e = be_ref[i]
        first = (i == 0) | (e != be_ref[jnp.maximum(i - 1, 0)])
        e_next = be_ref[jnp.minimum(i + 1, n_used - 1)]

        @pl.when(i == 0)
        def _():
            gu_copy(e).start()

        @pl.when(first)
        def _():
            down_copy(e).start()
            gu_copy(e).wait()

        fetch(i, slot).wait()

        def ffn(m):
            lo, hi = _unpack_halves(xbuf_ref[slot, :m])
            xb_ref[:m, :half] = lo.astype(_BF16)
            xb_ref[:m, half:] = hi.astype(_BF16)
            xb = xb_ref[:m]
            for c in range(d_ff // chunk):
                cs = slice(c * chunk, (c + 1) * chunk)
                us = slice(d_ff + c * chunk, d_ff + (c + 1) * chunk)
                g = jnp.minimum(_dot(xb, wgu_ref[:, cs]) + bgu_ref[:, cs], SWIGLU_LIMIT)
                u = jnp.clip(_dot(xb, wgu_ref[:, us]) + bgu_ref[:, us], -SWIGLU_LIMIT, SWIGLU_LIMIT)
                a_ref[:m, cs] = (g * jax.nn.sigmoid(SWIGLU_ALPHA * g) * (u + 1.0)).astype(_BF16)

            @pl.when(e_next != e)
            def _():
                gu_copy(e_next).start()

            @pl.when(first)
            def _():
                down_copy(e).wait()

            a = a_ref[:m]
            for c in range(half // chunk):
                cs = slice(c * chunk, (c + 1) * chunk)
                hs = slice(half + c * chunk, half + (c + 1) * chunk)
                o_lo = _dot(a, wd_ref[:, cs]) + bd_ref[:, cs]
                o_hi = _dot(a, wd_ref[:, hs]) + bd_ref[:, hs]
                obuf_ref[slot, :m, cs] = _pack_pair(o_lo, o_hi)
            if m < bm:
                obuf_ref[slot, m:] = jnp.zeros((bm - m, half), obuf_ref.dtype)

        short = valid_ref[i] <= bm // 2

        @pl.when(jnp.logical_not(short))
        def _():
            ffn(bm)

        @pl.when(short)
        def _():
            ffn(bm // 2)

    @pl.when(i >= n_used)
    def _():
        obuf_ref[slot] = jnp.zeros(obuf_ref.shape[1:], obuf_ref.dtype)

    put(i, slot).start()

    @pl.when(i == n_blk - 1)
    def _():
        put(i, slot).wait()
        if n_blk > 1:
            put(i - 1, 1 - slot).wait()


def _expert_ffn(xs, block_e, n_used, valid, w_gu_b, b_gu, w_down_b, b_down, bm, chunk):
    rows, _, half = xs.shape
    d = 2 * half
    n_experts, _, two_ff = w_gu_b.shape
    d_ff = two_ff // 2
    n_blk = rows // bm

    def blk(i, nu):
        return jnp.minimum(i, nu[0] - 1)

    hbm = pl.BlockSpec(memory_space=pl.ANY)
    grid_spec = pltpu.PrefetchScalarGridSpec(
        num_scalar_prefetch=3,
        grid=(n_blk,),
        in_specs=[
            pl.BlockSpec((None, 1, two_ff), lambda i, be, nu, vr: (be[blk(i, nu)], 0, 0)),
            pl.BlockSpec((None, 1, d), lambda i, be, nu, vr: (be[blk(i, nu)], 0, 0)),
            hbm, hbm, hbm,
        ],
        out_specs=hbm,
        scratch_shapes=[pltpu.VMEM((d, two_ff), _BF16), pltpu.VMEM((d_ff, d), _BF16),
                        pltpu.VMEM((2, bm, half), _U32), pltpu.VMEM((2, bm, half), _U32),
                        pltpu.VMEM((bm, d), _BF16), pltpu.VMEM((bm, d_ff), _BF16),
                        pltpu.SemaphoreType.DMA((2,)), pltpu.SemaphoreType.DMA((2,)), pltpu.SemaphoreType.DMA((2,))],
    )
    return pl.pallas_call(
        functools.partial(_ffn_body, bm=bm, n_blk=n_blk, chunk=chunk),
        grid_spec=grid_spec,
        out_shape=jax.ShapeDtypeStruct((rows, 1, half), _U32),
        compiler_params=_params(("arbitrary",), 56),
        name="expert_ffn",
    )(block_e, n_used, valid, b_gu.reshape(n_experts, 1, two_ff), b_down.reshape(n_experts, 1, d), xs, w_gu_b, w_down_b)


def _combine_body(dcur_ref, dnext_ref, x1_ref, rec_ref, mod_ref, gf_ref, out_ref, y_ref, gbuf_ref, sems, *, final_norm):
    i = pl.program_id(0)
    n = pl.num_programs(0)
    tm, d = x1_ref.shape
    w = d // 2
    groups = tm // SUBLANES
    slot = i % 2

    def start_group(dest_ref, s, g):
        for j in range(SUBLANES):
            for k in range(TOP_K):
                _row_copy(out_ref.at[dest_ref[0, 0, (g * SUBLANES + j) * TOP_K + k]], gbuf_ref.at[s, k, g, pl.ds(j, 1)],
                          sems.at[s]).start(priority=k % 2)

    def wait_tile(s):
        for k in range(TOP_K):
            for g in range(groups):
                _row_copy(_rows(out_ref, 0, SUBLANES), gbuf_ref.at[s, k, g], sems.at[s]).wait()

    @pl.when(i == 0)
    def _():
        def issue(g, carry):
            start_group(dcur_ref, 0, g)
            return carry
        lax.fori_loop(0, groups, issue, 0)

    wait_tile(slot)
    rec = rec_ref[...]
    gate2 = mod_ref[5, 0]
    for g in range(groups):
        start_group(dnext_ref, 1 - slot, g)
        rows = slice(g * SUBLANES, (g + 1) * SUBLANES)
        lo = hi = None
        for k in range(TOP_K):
            l, h = _unpack_halves(gbuf_ref[slot, k, g])
            gk = rec[rows, REC_GATE + k:REC_GATE + k + 1]
            lo = l * gk if lo is None else lo + l * gk
            hi = h * gk if hi is None else hi + h * gk
        g2 = gate2 if gate2.shape[0] == 1 else gate2[rows]
        x2l = x1_ref[rows, :w] + g2[:, :w] * lo
        x2h = x1_ref[rows, w:] + g2[:, w:] * hi
        if final_norm:
            ms = (jnp.sum(x2l * x2l, axis=-1, keepdims=True) + jnp.sum(x2h * x2h, axis=-1, keepdims=True)) * (1.0 / d)
            inv = lax.rsqrt(ms + EPS)
            x2l = x2l * inv * gf_ref[:, :w]
            x2h = x2h * inv * gf_ref[:, w:]
        y_ref[rows, :w] = x2l
        y_ref[rows, w:] = x2h

    @pl.when(i == n - 1)
    def _():
        wait_tile(1 - slot)


def _combine(x1, out_sorted, dest, rec, mods4, g_final, tm, final_norm):
    t, d = x1.shape
    n_tiles = t // tm
    tiles_per_group = n_tiles // mods4.shape[1]
    dest3 = dest.reshape(n_tiles, 1, tm * TOP_K)
    dspec = lambda nxt: pl.BlockSpec((1, 1, tm * TOP_K), lambda i: (jnp.minimum(i + nxt, n_tiles - 1), 0, 0),
                                     memory_space=pltpu.SMEM)
    return pl.pallas_call(
        functools.partial(_combine_body, final_norm=final_norm),
        grid=(n_tiles,),
        in_specs=[
            dspec(0), dspec(1),
            pl.BlockSpec((tm, d), lambda i: (i, 0)),
            pl.BlockSpec((tm, LANES), lambda i: (i, 0)),
            _mod_spec(mods4, tiles_per_group),
            pl.BlockSpec((1, d), lambda i: (0, 0)),
            pl.BlockSpec(memory_space=pl.ANY),
        ],
        out_specs=pl.BlockSpec((tm, d), lambda i: (i, 0)),
        out_shape=jax.ShapeDtypeStruct((t, d), _F32),
        scratch_shapes=[pltpu.VMEM((2, TOP_K, tm // SUBLANES, SUBLANES, d // 2), _U32),
                        pltpu.SemaphoreType.DMA((2,))],
        compiler_params=_params(("arbitrary",), 40),
        name="combine_final_norm",
    )(dest3, dest3, x1, rec, mods4, g_final.reshape(1, d), out_sorted)


def _tile(n, pref):
    t = pref
    while n % t:
        t //= 2
    return t


def _moe(h2_p, h2_s, rec_all, counts, w_gu_b, b_gu, w_down_b, b_down, tt, bm, chunk):
    t_all = rec_all.shape[0]
    n_experts = w_gu_b.shape[0]
    n_blk = -(-(t_all * TOP_K) // bm) + n_experts
    cnt = counts[0, :n_experts].astype(_I32)
    padded = (cnt + bm - 1) // bm * bm
    pends = jnp.cumsum(padded).astype(_I32)
    pstarts = pends - padded
    idx = rec_all[:, REC_IDX:REC_IDX + TOP_K].astype(_I32)
    rank = rec_all[:, REC_RANK:REC_RANK + TOP_K].astype(_I32)
    experts = jnp.arange(n_experts, dtype=_I32)
    dest = rank + jnp.sum(jnp.where(idx[:, :, None] == experts, pstarts, 0), axis=-1)
    blk_row0 = jnp.arange(n_blk, dtype=_I32) * bm
    block_e = jnp.minimum(jnp.sum((pends[None, :] <= blk_row0[:, None]).astype(_I32), axis=1), n_experts - 1)
    n_used = (pends[-1:] // bm).astype(_I32)
    row_end = jnp.sum(jnp.where(block_e[:, None] == experts, pstarts + cnt, 0), axis=-1)
    valid = jnp.clip(row_end - blk_row0, 0, bm).astype(_I32)
    xs = _dispatch(h2_p, h2_s, dest, pends, padded, n_used, n_blk * bm, tt, bm)
    return _expert_ffn(xs, block_e, n_used, valid, w_gu_b, b_gu, w_down_b, b_down, bm, chunk), dest


def kernel(x_prompt, x_sample, c_prompt, c_sample, cache_k, cache_v, state_conv, w_ada, b_ada, g_mix, w_in, attn_sink, conv_w, conv_b, conv_ln_g, conv_ln_b, w_out, g_ffn, w_router, b_router, w_gu, b_gu, w_down, b_down, g_final):
    bp, sp, d = x_prompt.shape
    bs, ss, _ = x_sample.shape
    depth = w_ada.shape[0]
    n_experts = w_router.shape[-1]
    tp, ts = bp * sp, bs * ss
    t_all = tp + ts

    tm = _tile(sp, 256)
    to = _tile(sp, 512)
    tq = _tile(sp, 256)
    tc = _tile(sp, 256)
    tt = _tile(ts, 128)
    bm = 512 if tp >= 8192 else 64
    chunk = 512
    assert tp % ts == 0 and tp % tt == 0 and sp % CHUNK == 0 and tq % WINDOW == 0 and tc % CONV_HALO == 0

    xp = x_prompt.reshape(tp, d)
    xsm = x_sample.reshape(ts, d)
    tabs_p = _rope_tables(np.arange(sp))
    tabs_s = _rope_tables(np.tile(PAST_LEN + np.arange(ss), bs))

    new_kp, new_vp, new_cp, new_ks, new_vs, new_cs = [], [], [], [], [], []
    for l in range(depth):
        w_in_b = w_in[l].astype(_BF16)
        wo_a = w_out[l, :ATTN_WIDTH].astype(_BF16)
        wo_c = w_out[l, ATTN_WIDTH:].astype(_BF16)
        wr_pad = jnp.pad(w_router[l], ((0, 0), (0, LANES - n_experts)))
        wr_hi = wr_pad.astype(_BF16)
        wr_lo = (wr_pad - wr_hi.astype(_F32)).astype(_BF16)
        br_pad = jnp.pad(b_router[l], (0, LANES - n_experts)).reshape(1, LANES)

        mods = _modulation(jnp.concatenate([c_prompt, c_sample], axis=0), w_ada[l], b_ada[l])
        mods_p = mods[:, :bp].reshape(6, bp, 1, d)
        mods_s = jnp.repeat(mods[:, bp:], ss, axis=1).reshape(6, 1, ts, d)

        qp, kp, vp, up = _in_proj(xp, mods_p, g_mix[l], w_in_b, tabs_p, tm)
        qs, ks, vs, us = _in_proj(xsm, mods_s, g_mix[l], w_in_b, tabs_s, ts)

        d_ff = w_down.shape[2]
        att_p, w_gu_b = _band_attention(qp, kp, vp, attn_sink[l], bp, sp, tq, w_gu[l].reshape(n_experts * d, 2 * d_ff))
        w_gu_b = w_gu_b.reshape(n_experts, d, 2 * d_ff)
        win = cache_k.shape[2]
        att_s = _cached_attention(qs, ks, vs, cache_k[l].reshape(bs, win, KV_COLS), cache_v[l].reshape(bs, win, KV_COLS),
                                  attn_sink[l], bs, ss)

        cv_p, w_down_b = _conv_module(up, None, conv_w[l], conv_b[l], conv_ln_g[l], conv_ln_b[l], bp, sp, tc,
                                      w_down[l].reshape(n_experts * d_ff, d))
        w_down_b = w_down_b.reshape(n_experts, d_ff, d)
        hist = jnp.pad(state_conv[l], ((0, 0), (CONV_HALO - (CONV_WIDTH - 1), 0), (0, 0))).reshape(bs * CONV_HALO, CONV_CH)
        cv_s = _conv_module(us, hist, conv_w[l], conv_b[l], conv_ln_g[l], conv_ln_b[l], bs, ss, ss)

        zero_counts = jnp.zeros((1, LANES), _F32)
        x1p, h2p, rec_p, counts = _out_proj(xp, att_p, cv_p, mods_p, g_ffn[l], wo_a, wo_c, wr_hi, wr_lo, br_pad,
                                            zero_counts, to, n_experts)
        x1s, h2s, rec_s, counts = _out_proj(xsm, att_s, cv_s, mods_s, g_ffn[l], wo_a, wo_c, wr_hi, wr_lo, br_pad,
                                            counts, ts, n_experts)

        out_sorted, dest = _moe(h2p, h2s, jnp.concatenate([rec_p, rec_s], axis=0), counts, w_gu_b, b_gu[l],
                                w_down_b, b_down[l], tt, bm, chunk)

        last = l == depth - 1
        xp = _combine(x1p, out_sorted, dest[:tp], rec_p, mods_p, g_final, tm, last)
        xsm = _combine(x1s, out_sorted, dest[tp:], rec_s, mods_s, g_final, ts, last)

        keep = min(WINDOW, sp)
        new_kp.append(kp.reshape(bp, sp, N_KV_HEADS, HEAD_DIM)[:, -keep:])
        new_vp.append(vp.reshape(bp, sp, N_KV_HEADS, HEAD_DIM)[:, -keep:])
        new_cp.append(up.reshape(bp, sp, CONV_CH)[:, -(CONV_WIDTH - 1):])
        new_ks.append(ks.reshape(bs, ss, N_KV_HEADS, HEAD_DIM))
        new_vs.append(vs.reshape(bs, ss, N_KV_HEADS, HEAD_DIM))
        new_cs.append(jnp.concatenate([state_conv[l], us.reshape(bs, ss, CONV_CH)], axis=1)[:, -(CONV_WIDTH - 1):])

    return (xp.reshape(bp, sp, d), xsm.reshape(bs, ss, d), jnp.stack(new_kp), jnp.stack(new_vp), jnp.stack(new_cp),
            jnp.stack(new_ks), jnp.stack(new_vs), jnp.stack(new_cs))
```

```python
import functools

import jax
import jax.numpy as jnp
import numpy as np
from jax import lax
from jax.experimental import pallas as pl
from jax.experimental.pallas import tpu as pltpu

D_MODEL = 2048
CHUNK = 64
HEAD_DIM = 64
ATTN_WIDTH = D_MODEL // 2
CONV_CH = D_MODEL - ATTN_WIDTH
N_HEADS = ATTN_WIDTH // HEAD_DIM
N_KV_HEADS = N_HEADS // 4
GQA_GROUP = N_HEADS // N_KV_HEADS
ROT_DIM = HEAD_DIM // 4
ROPE_THETA = 500000.0
WINDOW = 128
WINDOW_CHUNKS = WINDOW // CHUNK
CONV_WIDTH = 31
TOP_K = 4
SWIGLU_LIMIT = 7.0
SWIGLU_ALPHA = 1.702
EPS = 1e-5
NEG_INF = -1e30
PAST_LEN = 2048
Q_COLS = N_HEADS * HEAD_DIM
KV_COLS = N_KV_HEADS * HEAD_DIM
IN_COLS = Q_COLS + 2 * KV_COLS + 2 * CONV_CH
ATTN_SCALE = HEAD_DIM ** -0.5
assert ATTN_SCALE == 0.125

LANES = 128
SUBLANES = 8
CONV_HALO = 32
MIB = 1024 * 1024

_BF16 = jnp.bfloat16
_F32 = jnp.float32
_I32 = jnp.int32
_U32 = jnp.uint32


def _params(semantics, vmem_mib):
    return pltpu.CompilerParams(dimension_semantics=semantics, vmem_limit_bytes=vmem_mib * MIB)


def _dot(a, b):
    return jnp.dot(a, b, preferred_element_type=_F32)


def _pack_pair(lo, hi):
    lo = lax.bitcast_convert_type(lo.astype(_BF16).astype(_F32), _U32)
    hi = lax.bitcast_convert_type(hi.astype(_BF16).astype(_F32), _U32)
    return (lo >> 16) | (hi & jnp.uint32(0xFFFF0000))


def _pack_halves(x):
    w = x.shape[1] // 2
    return _pack_pair(x[:, :w], x[:, w:])


def _unpack_halves(p):
    return (lax.bitcast_convert_type(p << 16, _F32), lax.bitcast_convert_type(p & jnp.uint32(0xFFFF0000), _F32))


def _rmsnorm_mod(x, g, scale, shift):
    ms = jnp.mean(x * x, axis=-1, keepdims=True)
    return (x * lax.rsqrt(ms + EPS) * g) * (1.0 + scale) + shift


def _ada_body(c_ref, w_ref, b_ref, o_ref):
    c = c_ref[...]
    s = (c * jax.nn.sigmoid(c)).astype(_BF16)
    o_ref[...] = _dot(s, w_ref[...].astype(_BF16)) + b_ref[...]


def _modulation(c, w_ada, b_ada):
    n, d = c.shape
    cols = w_ada.shape[1]
    tn = 1024
    per = d // tn
    return pl.pallas_call(
        _ada_body,
        grid=(cols // tn,),
        in_specs=[
            pl.BlockSpec((n, d), lambda j: (0, 0)),
            pl.BlockSpec((d, tn), lambda j: (0, j)),
            pl.BlockSpec((1, tn), lambda j: (0, j)),
        ],
        out_specs=pl.BlockSpec((None, n, tn), lambda j: (j // per, 0, j % per)),
        out_shape=jax.ShapeDtypeStruct((cols // d, n, d), _F32),
        compiler_params=_params(("arbitrary",), 40),
        name="modulation",
    )(c, w_ada, b_ada.reshape(1, cols))


def _rope_tables(pos):
    half = ROT_DIM // 2
    inv_freq = np.power(np.float32(ROPE_THETA), -np.arange(0, ROT_DIM, 2, dtype=np.float32) / np.float32(ROT_DIM))
    ang = (pos.astype(np.float32)[:, None] * inv_freq[None, :].astype(np.float32)).astype(np.float64)
    cos, sin = np.cos(ang), np.sin(ang)
    d = np.arange(LANES) % HEAD_DIM
    cos_l = np.where(d < ROT_DIM, cos[:, d % half], 1.0)
    sa = np.where(d < half, -sin[:, d % half], 0.0)
    sb = np.where((d >= half) & (d < ROT_DIM), sin[:, d % half], 0.0)
    return tuple(jnp.asarray(t.astype(np.float32)) for t in (cos_l, sa, sb))


def _rotate(z, cos, sa, sb):
    half = ROT_DIM // 2
    parts = []
    for j in range(z.shape[1] // LANES):
        zj = z[:, j * LANES:(j + 1) * LANES]
        parts.append(zj * cos + pltpu.roll(zj, LANES - half, 1) * sa + pltpu.roll(zj, half, 1) * sb)
    return parts[0] if len(parts) == 1 else jnp.concatenate(parts, axis=1)


def _in_proj_body(x_ref, mod_ref, g_ref, w_ref, cos_ref, sa_ref, sb_ref, *rest):
    q_ref, k_ref, v_ref, u_ref = rest[-4:] if len(rest) == 4 else rest[1:5]
    if len(rest) == 6:
        rest[5][...] = rest[0][...].astype(_BF16)
    h = _rmsnorm_mod(x_ref[...], g_ref[...], mod_ref[1, 0], mod_ref[0, 0]).astype(_BF16)
    cos, sa, sb = cos_ref[...], sa_ref[...], sb_ref[...]
    cw = 512
    for c in range(Q_COLS // cw):
        z = _dot(h, w_ref[:, c * cw:(c + 1) * cw])
        q_ref[:, c * cw:(c + 1) * cw] = (_rotate(z, cos, sa, sb) * ATTN_SCALE).astype(_BF16)
    z = _dot(h, w_ref[:, Q_COLS:Q_COLS + 2 * KV_COLS])
    k_ref[...] = _rotate(z[:, :KV_COLS], cos, sa, sb)
    v_ref[...] = z[:, KV_COLS:]
    o = Q_COLS + 2 * KV_COLS
    for c in range(CONV_CH // cw):
        zv = _dot(h, w_ref[:, o + c * cw:o + (c + 1) * cw])
        zg = _dot(h, w_ref[:, o + CONV_CH + c * cw:o + CONV_CH + (c + 1) * cw])
        u_ref[:, c * cw:(c + 1) * cw] = zv * jax.nn.sigmoid(zg)


def _mod_spec(mods4, tiles_per_group):
    _, _, r, d = mods4.shape
    return pl.BlockSpec((6, 1, r, d), lambda i: (0, i // tiles_per_group, 0, 0))


def _in_proj(x2d, mods4, g_mix, w_in_b, tabs, tm, rider=None):
    t, d = x2d.shape
    n_tiles = t // tm
    tiles_per_group = n_tiles // mods4.shape[1]
    pos_tiles = tabs[0].shape[0] // tm
    tab_spec = pl.BlockSpec((tm, LANES), lambda i: (i % pos_tiles, 0))
    row = lambda w: pl.BlockSpec((tm, w), lambda i: (i, 0))
    in_specs = [
        row(d),
        _mod_spec(mods4, tiles_per_group),
        pl.BlockSpec((1, d), lambda i: (0, 0)),
        pl.BlockSpec((d, IN_COLS), lambda i: (0, 0), pipeline_mode=pl.Buffered(1)),
        tab_spec, tab_spec, tab_spec,
    ]
    out_specs = [row(Q_COLS), row(KV_COLS), row(KV_COLS), row(CONV_CH)]
    out_shape = [
        jax.ShapeDtypeStruct((t, Q_COLS), _BF16),
        jax.ShapeDtypeStruct((t, KV_COLS), _F32),
        jax.ShapeDtypeStruct((t, KV_COLS), _F32),
        jax.ShapeDtypeStruct((t, CONV_CH), _F32),
    ]
    args = [x2d, mods4, g_mix.reshape(1, d), w_in_b, *tabs]
    ride = None if rider is None else _rider_specs(rider, n_tiles, lambda i: i)
    if ride is not None:
        in_specs.append(ride[0])
        out_specs.append(ride[1])
        out_shape.append(ride[2])
        args.append(rider[0])
    res = pl.pallas_call(
        _in_proj_body,
        grid=(n_tiles,),
        in_specs=in_specs,
        out_specs=out_specs,
        out_shape=out_shape,
        compiler_params=_params(("arbitrary",), 56),
        name="in_proj",
    )(*args)
    if rider is not None and ride is None:
        res = list(res) + [_rider_fallback(rider)]
    return res


def _attn_group(qg, kk, vv, sink_col, valid):
    s = lax.dot_general(qg, kk, (((1,), (1,)), ((), ())), preferred_element_type=_F32)
    if valid is not None:
        s = jnp.where(valid, s, NEG_INF)
    m = jnp.maximum(jnp.max(s, axis=-1, keepdims=True), sink_col)
    p = jnp.exp(s - m)
    den = jnp.sum(p, axis=-1, keepdims=True) + jnp.exp(sink_col - m)
    return _dot(p.astype(_BF16), vv) / den


def _sink_column(sink_ref, kh, rows_per_head):
    row = lax.broadcasted_iota(_I32, (GQA_GROUP * rows_per_head, 1), 0)
    col = jnp.full(row.shape, sink_ref[kh * GQA_GROUP + GQA_GROUP - 1], _F32)
    for g in range(GQA_GROUP - 2, -1, -1):
        col = jnp.where(row < (g + 1) * rows_per_head, sink_ref[kh * GQA_GROUP + g], col)
    return col


def _store_heads(o_ref, r0, rows, kh, o):
    for pair in range(GQA_GROUP // 2):
        both = jnp.concatenate([o[(2 * pair) * rows:(2 * pair + 1) * rows],
                                o[(2 * pair + 1) * rows:(2 * pair + 2) * rows]], axis=1)
        c0 = (kh * GQA_GROUP + 2 * pair) * HEAD_DIM
        o_ref[r0:r0 + rows, c0:c0 + 2 * HEAD_DIM] = both.astype(o_ref.dtype)


def _stack_heads(q, r0, rows, kh):
    return jnp.concatenate(
        [q[r0:r0 + rows, (kh * GQA_GROUP + g) * HEAD_DIM:(kh * GQA_GROUP + g + 1) * HEAD_DIM] for g in range(GQA_GROUP)],
        axis=0)


def _rider_specs(rider, n_steps, step_index):
    arr, width, colblk = rider
    rows = arr.shape[0]
    per = rows // n_steps
    if rows % n_steps or per % 16 or per * width * 4 > 8 * MIB:
        return None
    return (pl.BlockSpec((per, width), lambda *ids: (step_index(*ids), colblk)),
            pl.BlockSpec((per, width), lambda *ids: (step_index(*ids), 0)),
            jax.ShapeDtypeStruct((rows, width), _BF16))


def _rider_fallback(rider):
    arr, width, colblk = rider
    return arr[:, colblk * width:(colblk + 1) * width].astype(_BF16)


def _band_attn_body(sink_ref, q_ref, kc_ref, kh_ref, vc_ref, vh_ref, *rest, tq):
    o_ref = rest[-1] if len(rest) == 1 else rest[1]
    if len(rest) == 3:
        rest[2][...] = rest[0][...].astype(_BF16)
    j = pl.program_id(1)
    q = q_ref[...]
    kall = jnp.concatenate([kh_ref[...], kc_ref[...]], axis=0)
    vall = jnp.concatenate([vh_ref[...], vc_ref[...]], axis=0)
    band = (WINDOW_CHUNKS + 1) * CHUNK
    key_chunk = lax.broadcasted_iota(_I32, (1, band), 1) // CHUNK
    for kh in range(N_KV_HEADS):
        sink_col = _sink_column(sink_ref, kh, CHUNK)
        kk = kall[:, kh * HEAD_DIM:(kh + 1) * HEAD_DIM].astype(_BF16)
        vv = vall[:, kh * HEAD_DIM:(kh + 1) * HEAD_DIM].astype(_BF16)
        for ci in range(tq // CHUNK):
            qg = _stack_heads(q, ci * CHUNK, CHUNK, kh)
            valid = None if ci >= WINDOW_CHUNKS else (j * (tq // CHUNK) + ci - WINDOW_CHUNKS + key_chunk) >= 0
            o = _attn_group(qg, kk[ci * CHUNK:ci * CHUNK + band], vv[ci * CHUNK:ci * CHUNK + band], sink_col, valid)
            _store_heads(o_ref, ci * CHUNK, CHUNK, kh, o)


def _band_attention(q, k, v, sink, batch, seq, tq, rider):
    nq = seq // tq
    r = tq // WINDOW
    cur = lambda w: pl.BlockSpec((tq, w), lambda b, j: (b * nq + j, 0))
    halo = pl.BlockSpec((WINDOW, KV_COLS), lambda b, j: (b * nq * r + jnp.maximum(j * r - 1, 0), 0))
    in_specs = [pl.BlockSpec(memory_space=pltpu.SMEM), cur(Q_COLS), cur(KV_COLS), halo, cur(KV_COLS), halo]
    out_specs = [cur(Q_COLS)]
    out_shape = [jax.ShapeDtypeStruct((batch * seq, Q_COLS), _BF16)]
    args = [sink, q, k, k, v, v]
    ride = _rider_specs(rider, batch * nq, lambda b, j: b * nq + j)
    if ride is not None:
        in_specs.append(ride[0])
        out_specs.append(ride[1])
        out_shape.append(ride[2])
        args.append(rider[0])
    res = pl.pallas_call(
        functools.partial(_band_attn_body, tq=tq),
        grid=(batch, nq),
        in_specs=in_specs,
        out_specs=out_specs,
        out_shape=out_shape,
        compiler_params=_params(("arbitrary", "arbitrary"), 48),
        name="band_attention",
    )(*args)
    return (res[0], res[1]) if ride is not None else (res[0], _rider_fallback(rider))


def _cached_attn_body(sink_ref, q_ref, kn_ref, kc_ref, vn_ref, vc_ref, o_ref, *, rows):
    q = q_ref[...]
    kall = jnp.concatenate([kc_ref[0], kn_ref[...]], axis=0)
    vall = jnp.concatenate([vc_ref[0], vn_ref[...]], axis=0)
    for kh in range(N_KV_HEADS):
        o = _attn_group(_stack_heads(q, 0, rows, kh), kall[:, kh * HEAD_DIM:(kh + 1) * HEAD_DIM].astype(_BF16),
                        vall[:, kh * HEAD_DIM:(kh + 1) * HEAD_DIM].astype(_BF16), _sink_column(sink_ref, kh, rows), None)
        _store_heads(o_ref, 0, rows, kh, o)


def _cached_attention(q, k, v, cache_k, cache_v, sink, batch, rows):
    win = cache_k.shape[1]
    new = lambda w: pl.BlockSpec((rows, w), lambda b: (b, 0))
    cache = pl.BlockSpec((1, win, KV_COLS), lambda b: (b, 0, 0))
    return pl.pallas_call(
        functools.partial(_cached_attn_body, rows=rows),
        grid=(batch,),
        in_specs=[pl.BlockSpec(memory_space=pltpu.SMEM), new(Q_COLS), new(KV_COLS), cache, new(KV_COLS), cache],
        out_specs=new(Q_COLS),
        out_shape=jax.ShapeDtypeStruct((batch * rows, Q_COLS), _BF16),
        compiler_params=_params(("arbitrary",), 40),
        name="cached_attention",
    )(sink, q, k, cache_k, v, cache_v)


def _conv_body(uc_ref, uh_ref, w_ref, b_ref, g_ref, beta_ref, *rest, tt, zero_first_halo):
    win_ref, y_ref = rest[-2:]
    o_ref = rest[0] if len(rest) == 3 else rest[1]
    if len(rest) == 5:
        rest[2][...] = rest[0][...].astype(_BF16)
    halo = uh_ref[...]
    if zero_first_halo:
        halo = jnp.where(pl.program_id(1) == 0, 0.0, halo)
    win_ref[0:CONV_HALO, :] = halo
    win_ref[CONV_HALO:, :] = uc_ref[...]
    lead = CONV_HALO - (CONV_WIDTH - 1)
    by_shift = {}
    for tap in range(CONV_WIDTH):
        by_shift.setdefault((lead + tap) % SUBLANES, []).append(tap)
    rb = min(tt, 128)
    for s in range(CONV_CH // LANES):
        cs = slice(s * LANES, (s + 1) * LANES)
        for r0 in range(0, tt, rb):
            y = b_ref[:, cs]
            for sh, taps in sorted(by_shift.items()):
                ext = SUBLANES if sh else 0
                q = None
                for tap in taps:
                    base = r0 + lead + tap - sh
                    term = w_ref[tap:tap + 1, cs] * win_ref[base:base + rb + ext, cs]
                    q = term if q is None else q + term
                y = y + (q[sh:sh + rb] if sh else q)
            y_ref[r0:r0 + rb, cs] = y
    y = y_ref[...]
    mu = jnp.mean(y, axis=-1, keepdims=True)
    var = jnp.mean(jnp.square(y - mu), axis=-1, keepdims=True)
    z = (y - mu) * lax.rsqrt(var + EPS) * g_ref[...] + beta_ref[...]
    o_ref[...] = (z * jax.nn.sigmoid(z)).astype(o_ref.dtype)


def _conv_module(u, hist, conv_w, conv_b, ln_g, ln_b, batch, seq, tt, rider=None):
    nt = seq // tt
    r = tt // CONV_HALO
    cur = pl.BlockSpec((tt, CONV_CH), lambda b, j: (b * nt + j, 0))
    if hist is None:
        hist_arr = u
        halo = pl.BlockSpec((CONV_HALO, CONV_CH), lambda b, j: (b * nt * r + jnp.maximum(j * r - 1, 0), 0))
    else:
        assert nt == 1
        hist_arr = hist
        halo = pl.BlockSpec((CONV_HALO, CONV_CH), lambda b, j: (b, 0))
    vec = pl.BlockSpec((1, CONV_CH), lambda b, j: (0, 0))
    wpad = jnp.pad(conv_w.reshape(CONV_WIDTH, CONV_CH), ((0, 1), (0, 0)))
    in_specs = [cur, halo, pl.BlockSpec((CONV_WIDTH + 1, CONV_CH), lambda b, j: (0, 0)), vec, vec, vec]
    out_specs = [cur]
    out_shape = [jax.ShapeDtypeStruct((batch * seq, CONV_CH), _BF16)]
    args = [u, hist_arr, wpad, conv_b.reshape(1, -1), ln_g.reshape(1, -1), ln_b.reshape(1, -1)]
    ride = None if rider is None else _rider_specs(rider, batch * nt, lambda b, j: b * nt + j)
    if ride is not None:
        in_specs.append(ride[0])
        out_specs.append(ride[1])
        out_shape.append(ride[2])
        args.append(rider[0])
    res = pl.pallas_call(
        functools.partial(_conv_body, tt=tt, zero_first_halo=hist is None),
        grid=(batch, nt),
        in_specs=in_specs,
        out_specs=out_specs,
        out_shape=out_shape,
        scratch_shapes=[pltpu.VMEM((tt + CONV_HALO, CONV_CH), _F32), pltpu.VMEM((tt, CONV_CH), _F32)],
        compiler_params=_params(("arbitrary", "arbitrary"), 40),
        name="conv_module",
    )(*args)
    if rider is None:
        return res[0]
    return (res[0], res[1]) if ride is not None else (res[0], _rider_fallback(rider))


REC_IDX, REC_GATE, REC_RANK = 0, TOP_K, 2 * TOP_K


def _out_proj_body(x_ref, a_ref, c_ref, mod_ref, g_ref, wa_ref, wc_ref, wrh_ref, wrl_ref, br_ref, cnt_ref,
                   x1_ref, h2_ref, rec_ref, cnt_out_ref, run_ref, *, n_experts):
    tm = x_ref.shape[0]

    @pl.when(pl.program_id(0) == 0)
    def _():
        run_ref[...] = cnt_ref[...]

    o = _dot(a_ref[...], wa_ref[...]) + _dot(c_ref[...], wc_ref[...])
    x1 = x_ref[...] + mod_ref[2, 0] * o
    x1_ref[...] = x1
    h2 = _rmsnorm_mod(x1, g_ref[...], mod_ref[4, 0], mod_ref[3, 0])
    h2_ref[...] = _pack_halves(h2)

    h2_hi = h2.astype(_BF16)
    h2_lo = (h2 - h2_hi.astype(_F32)).astype(_BF16)
    logits = _dot(h2_hi, wrh_ref[...]) + (_dot(h2_lo, wrh_ref[...]) + _dot(h2_hi, wrl_ref[...])) + br_ref[...]
    lane = lax.broadcasted_iota(_I32, (tm, LANES), 1)
    work = jnp.where(lane < n_experts, logits, -jnp.inf)
    vals, hots = [], []
    rec = jnp.zeros((tm, LANES), _F32)
    for k in range(TOP_K):
        m = jnp.max(work, axis=-1, keepdims=True)
        idx = jnp.min(jnp.where(work == m, lane, LANES), axis=-1, keepdims=True)
        hot = lane == idx
        work = jnp.where(hot, -jnp.inf, work)
        vals.append(m)
        hots.append(hot)
        rec = jnp.where(lane == REC_IDX + k, idx.astype(_F32), rec)
    exps = [jnp.exp(v - vals[0]) for v in vals]
    den = exps[0]
    for e in exps[1:]:
        den = den + e
    for k in range(TOP_K):
        rec = jnp.where(lane == REC_GATE + k, exps[k] / den, rec)

    chosen = jnp.zeros((tm, LANES), _F32)
    for hot in hots:
        chosen = jnp.where(hot, 1.0, chosen)
    r_i = lax.broadcasted_iota(_I32, (tm, tm), 0)
    c_i = lax.broadcasted_iota(_I32, (tm, tm), 1)
    before = _dot(jnp.where(c_i < r_i, 1.0, 0.0).astype(_BF16), chosen.astype(_BF16)) + run_ref[...]
    for k in range(TOP_K):
        rank = jnp.sum(jnp.where(hots[k], before, 0.0), axis=-1, keepdims=True)
        rec = jnp.where(lane == REC_RANK + k, rank, rec)
    rec_ref[...] = rec
    run_ref[...] = run_ref[...] + jnp.sum(chosen, axis=0, keepdims=True)
    cnt_out_ref[...] = run_ref[...]


def _out_proj(x2d, attn, conv, mods4, g_ffn, wo_a, wo_c, wr_hi, wr_lo, br_pad, counts, tm, n_experts):
    t, d = x2d.shape
    n_tiles = t // tm
    tiles_per_group = n_tiles // mods4.shape[1]
    row = lambda w: pl.BlockSpec((tm, w), lambda i: (i, 0))
    const = lambda a, b: pl.BlockSpec((a, b), lambda i: (0, 0))
    return pl.pallas_call(
        functools.partial(_out_proj_body, n_experts=n_experts),
        grid=(n_tiles,),
        in_specs=[row(d), row(ATTN_WIDTH), row(CONV_CH), _mod_spec(mods4, tiles_per_group), const(1, d),
                  const(ATTN_WIDTH, d), const(CONV_CH, d), const(d, LANES), const(d, LANES), const(1, LANES),
                  const(1, LANES)],
        out_specs=[row(d), row(d // 2), row(LANES), const(1, LANES)],
        out_shape=[
            jax.ShapeDtypeStruct((t, d), _F32),
            jax.ShapeDtypeStruct((t, d // 2), _U32),
            jax.ShapeDtypeStruct((t, LANES), _F32),
            jax.ShapeDtypeStruct((1, LANES), _F32),
        ],
        scratch_shapes=[pltpu.VMEM((1, LANES), _F32)],
        compiler_params=_params(("arbitrary",), 56),
        name="out_proj_router",
    )(x2d, attn, conv, mods4, g_ffn.reshape(1, d), wo_a, wo_c, wr_hi, wr_lo, br_pad, counts)


def _row_copy(src, dst, sem):
    return pltpu.make_async_copy(src, dst, sem)


def _rows(ref, row0, n):
    return ref.at[pl.ds(row0, n), 0]


def _dispatch_body(pends_ref, padded_ref, nu_ref, dest_ref, hp_ref, hs_ref, xs_ref, stage_ref, zeros_ref, sems, zsem, *,
                   tt, bm, n_experts, prompt_steps):
    i = pl.program_id(0)
    n = pl.num_programs(0)
    n_blk = xs_ref.shape[0] // bm
    slot = i % 2

    def zero_block(row0):
        return _row_copy(zeros_ref, _rows(xs_ref, pl.multiple_of(row0, bm), bm), zsem)

    @pl.when(i == 0)
    def _():
        zeros_ref[...] = jnp.zeros(zeros_ref.shape, zeros_ref.dtype)
        for e in range(n_experts):
            @pl.when(padded_ref[e] > 0)
            def _():
                zero_block(pends_ref[e] - bm).start()

        def tail_start(b, carry):
            zero_block(b * bm).start()
            return carry

        def tail_wait(b, carry):
            zero_block(b * bm).wait()
            return carry

        lax.fori_loop(nu_ref[0], n_blk, tail_start, 0)
        for e in range(n_experts):
            @pl.when(padded_ref[e] > 0)
            def _():
                zero_block(pends_ref[e] - bm).wait()
        lax.fori_loop(nu_ref[0], n_blk, tail_wait, 0)

    @pl.when(i < prompt_steps)
    def _():
        stage_ref[slot] = hp_ref[...]

    @pl.when(i >= prompt_steps)
    def _():
        stage_ref[slot] = hs_ref[...]

    for r in range(tt):
        src = stage_ref.at[slot, pl.ds(r, 1)]
        for k in range(TOP_K):
            _row_copy(src, xs_ref.at[dest_ref[0, 0, r * TOP_K + k]], sems.at[slot]).start(priority=k % 2)

    def drain(s):
        for _ in range(TOP_K):
            _row_copy(stage_ref.at[s], _rows(xs_ref, 0, tt), sems.at[s]).wait()

    @pl.when(i > 0)
    def _():
        drain(1 - slot)

    @pl.when(i == n - 1)
    def _():
        drain(slot)


def _dispatch(h2_p, h2_s, dest, pends, padded, n_used, rows, tt, bm):
    w = h2_p.shape[1]
    tp, ts = h2_p.shape[0], h2_s.shape[0]
    n_experts = pends.shape[0]
    prompt_steps = tp // tt
    grid_spec = pltpu.PrefetchScalarGridSpec(
        num_scalar_prefetch=3,
        grid=((tp + ts) // tt,),
        in_specs=[
            pl.BlockSpec((1, 1, tt * TOP_K), lambda i, *_: (i, 0, 0), memory_space=pltpu.SMEM),
            pl.BlockSpec((tt, w), lambda i, *_: (jnp.minimum(i, prompt_steps - 1), 0)),
            pl.BlockSpec((tt, w), lambda i, *_: (jnp.maximum(i - prompt_steps, 0), 0)),
        ],
        out_specs=pl.BlockSpec(memory_space=pl.ANY),
        scratch_shapes=[pltpu.VMEM((2, tt, w), _U32), pltpu.VMEM((bm, w), _U32),
                        pltpu.SemaphoreType.DMA((2,)), pltpu.SemaphoreType.DMA(())],
    )
    return pl.pallas_call(
        functools.partial(_dispatch_body, tt=tt, bm=bm, n_experts=n_experts, prompt_steps=prompt_steps),
        grid_spec=grid_spec,
        out_shape=jax.ShapeDtypeStruct((rows, 1, w), _U32),
        compiler_params=_params(("arbitrary",), 24),
        name="moe_dispatch",
    )(pends, padded, n_used, dest.reshape((tp + ts) // tt, 1, tt * TOP_K), h2_p, h2_s)


def _ffn_body(be_ref, nu_ref, valid_ref, bgu_ref, bd_ref, xs_hbm, wg_hbm, wu_hbm, wd_hbm, o_hbm, wg_ref, wu_ref, wd_ref,
              xbuf_ref, obuf_ref, xb_ref, a_ref, wsems, xsems, osems, *, bm, n_blk, chunk):
    i = pl.program_id(0)
    n_used = nu_ref[0]
    d_ff = wd_ref.shape[0]
    half = xbuf_ref.shape[2]
    slot = i % 2

    def fetch(b, s):
        return _row_copy(_rows(xs_hbm, pl.multiple_of(b * bm, bm), bm), xbuf_ref.at[s], xsems.at[s])

    def put(b, s):
        return _row_copy(obuf_ref.at[s], _rows(o_hbm, pl.multiple_of(b * bm, bm), bm), osems.at[s])

    class gu_copy:
        def __init__(self, e):
            self.copies = (_row_copy(wg_hbm.at[e], wg_ref, wsems.at[0]), _row_copy(wu_hbm.at[e], wu_ref, wsems.at[1]))

        def start(self):
            for c in self.copies:
                c.start()

        def wait(self):
            for c in self.copies:
                c.wait()

    def down_copy(e):
        return _row_copy(wd_hbm.at[e], wd_ref, wsems.at[2])

    @pl.when(i == 0)
    def _():
        fetch(0, 0).start()

    @pl.when(i + 1 < n_used)
    def _():
        fetch(i + 1, 1 - slot).start()

    @pl.when(i >= 2)
    def _():
        put(i - 2, slot).wait()

    @pl.when(i < n_used)
    def _():
        e = be_ref[i]
        first = (i == 0) | (e != be_ref[jnp.maximum(i - 1, 0)])
        e_next = be_ref[jnp.minimum(i + 1, n_used - 1)]

        @pl.when(i == 0)
        def _():
            gu_copy(e).start()

        @pl.when(first)
        def _():
            down_copy(e).start()
            gu_copy(e).wait()

        fetch(i, slot).wait()

        def ffn(m):
            lo, hi = _unpack_halves(xbuf_ref[slot, :m])
            xb_ref[:m, :half] = lo.astype(_BF16)
            xb_ref[:m, half:] = hi.astype(_BF16)
            xb = xb_ref[:m]
            for c in range(d_ff // chunk):
                cs = slice(c * chunk, (c + 1) * chunk)
                us = slice(d_ff + c * chunk, d_ff + (c + 1) * chunk)
                g = jnp.minimum(_dot(xb, wg_ref[:, cs]) + bgu_ref[:, cs], SWIGLU_LIMIT)
                u = jnp.clip(_dot(xb, wu_ref[:, cs]) + bgu_ref[:, us], -SWIGLU_LIMIT, SWIGLU_LIMIT)
                a_ref[:m, cs] = (g * jax.nn.sigmoid(SWIGLU_ALPHA * g) * (u + 1.0)).astype(_BF16)

            @pl.when(e_next != e)
            def _():
                gu_copy(e_next).start()

            @pl.when(first)
            def _():
                down_copy(e).wait()

            a = a_ref[:m]
            for c in range(half // chunk):
                cs = slice(c * chunk, (c + 1) * chunk)
                hs = slice(half + c * chunk, half + (c + 1) * chunk)
                o_lo = _dot(a, wd_ref[:, cs]) + bd_ref[:, cs]
                o_hi = _dot(a, wd_ref[:, hs]) + bd_ref[:, hs]
                obuf_ref[slot, :m, cs] = _pack_pair(o_lo, o_hi)
            if m < bm:
                obuf_ref[slot, m:] = jnp.zeros((bm - m, half), obuf_ref.dtype)

        short = valid_ref[i] <= bm // 2

        @pl.when(jnp.logical_not(short))
        def _():
            ffn(bm)

        @pl.when(short)
        def _():
            ffn(bm // 2)

    @pl.when(i >= n_used)
    def _():
        obuf_ref[slot] = jnp.zeros(obuf_ref.shape[1:], obuf_ref.dtype)

    put(i, slot).start()

    @pl.when(i == n_blk - 1)
    def _():
        put(i, slot).wait()
        if n_blk > 1:
            put(i - 1, 1 - slot).wait()


def _expert_ffn(xs, block_e, n_used, valid, w_g_b, w_u_b, b_gu, w_down_b, b_down, bm, chunk):
    rows, _, half = xs.shape
    d = 2 * half
    n_experts, _, d_ff = w_g_b.shape
    two_ff = 2 * d_ff
    n_blk = rows // bm

    def blk(i, nu):
        return jnp.minimum(i, nu[0] - 1)

    hbm = pl.BlockSpec(memory_space=pl.ANY)
    grid_spec = pltpu.PrefetchScalarGridSpec(
        num_scalar_prefetch=3,
        grid=(n_blk,),
        in_specs=[
            pl.BlockSpec((None, 1, two_ff), lambda i, be, nu, vr: (be[blk(i, nu)], 0, 0)),
            pl.BlockSpec((None, 1, d), lambda i, be, nu, vr: (be[blk(i, nu)], 0, 0)),
            hbm, hbm, hbm, hbm,
        ],
        out_specs=hbm,
        scratch_shapes=[pltpu.VMEM((d, d_ff), _BF16), pltpu.VMEM((d, d_ff), _BF16), pltpu.VMEM((d_ff, d), _BF16),
                        pltpu.VMEM((2, bm, half), _U32), pltpu.VMEM((2, bm, half), _U32),
                        pltpu.VMEM((bm, d), _BF16), pltpu.VMEM((bm, d_ff), _BF16),
                        pltpu.SemaphoreType.DMA((3,)), pltpu.SemaphoreType.DMA((2,)), pltpu.SemaphoreType.DMA((2,))],
    )
    return pl.pallas_call(
        functools.partial(_ffn_body, bm=bm, n_blk=n_blk, chunk=chunk),
        grid_spec=grid_spec,
        out_shape=jax.ShapeDtypeStruct((rows, 1, half), _U32),
        compiler_params=_params(("arbitrary",), 56),
        name="expert_ffn",
    )(block_e, n_used, valid, b_gu.reshape(n_experts, 1, two_ff), b_down.reshape(n_experts, 1, d), xs, w_g_b, w_u_b,
      w_down_b)


def _combine_body(dcur_ref, dnext_ref, x1_ref, rec_ref, mod_ref, gf_ref, out_ref, y_ref, gbuf_ref, sems, *, final_norm):
    i = pl.program_id(0)
    n = pl.num_programs(0)
    tm, d = x1_ref.shape
    w = d // 2
    groups = tm // SUBLANES
    slot = i % 2

    def start_group(dest_ref, s, g):
        for j in range(SUBLANES):
            for k in range(TOP_K):
                _row_copy(out_ref.at[dest_ref[0, 0, (g * SUBLANES + j) * TOP_K + k]], gbuf_ref.at[s, k, g, pl.ds(j, 1)],
                          sems.at[s]).start(priority=k % 2)

    def wait_tile(s):
        for k in range(TOP_K):
            for g in range(groups):
                _row_copy(_rows(out_ref, 0, SUBLANES), gbuf_ref.at[s, k, g], sems.at[s]).wait()

    @pl.when(i == 0)
    def _():
        def issue(g, carry):
            start_group(dcur_ref, 0, g)
            return carry
        lax.fori_loop(0, groups, issue, 0)

    wait_tile(slot)
    rec = rec_ref[...]
    gate2 = mod_ref[5, 0]
    for g in range(groups):
        start_group(dnext_ref, 1 - slot, g)
        rows = slice(g * SUBLANES, (g + 1) * SUBLANES)
        lo = hi = None
        for k in range(TOP_K):
            l, h = _unpack_halves(gbuf_ref[slot, k, g])
            gk = rec[rows, REC_GATE + k:REC_GATE + k + 1]
            lo = l * gk if lo is None else lo + l * gk
            hi = h * gk if hi is None else hi + h * gk
        g2 = gate2 if gate2.shape[0] == 1 else gate2[rows]
        x2l = x1_ref[rows, :w] + g2[:, :w] * lo
        x2h = x1_ref[rows, w:] + g2[:, w:] * hi
        if final_norm:
            ms = (jnp.sum(x2l * x2l, axis=-1, keepdims=True) + jnp.sum(x2h * x2h, axis=-1, keepdims=True)) * (1.0 / d)
            inv = lax.rsqrt(ms + EPS)
            x2l = x2l * inv * gf_ref[:, :w]
            x2h = x2h * inv * gf_ref[:, w:]
        y_ref[rows, :w] = x2l
        y_ref[rows, w:] = x2h

    @pl.when(i == n - 1)
    def _():
        wait_tile(1 - slot)


def _combine(x1, out_sorted, dest, rec, mods4, g_final, tm, final_norm):
    t, d = x1.shape
    n_tiles = t // tm
    tiles_per_group = n_tiles // mods4.shape[1]
    dest3 = dest.reshape(n_tiles, 1, tm * TOP_K)
    dspec = lambda nxt: pl.BlockSpec((1, 1, tm * TOP_K), lambda i: (jnp.minimum(i + nxt, n_tiles - 1), 0, 0),
                                     memory_space=pltpu.SMEM)
    return pl.pallas_call(
        functools.partial(_combine_body, final_norm=final_norm),
        grid=(n_tiles,),
        in_specs=[
            dspec(0), dspec(1),
            pl.BlockSpec((tm, d), lambda i: (i, 0)),
            pl.BlockSpec((tm, LANES), lambda i: (i, 0)),
            _mod_spec(mods4, tiles_per_group),
            pl.BlockSpec((1, d), lambda i: (0, 0)),
            pl.BlockSpec(memory_space=pl.ANY),
        ],
        out_specs=pl.BlockSpec((tm, d), lambda i: (i, 0)),
        out_shape=jax.ShapeDtypeStruct((t, d), _F32),
        scratch_shapes=[pltpu.VMEM((2, TOP_K, tm // SUBLANES, SUBLANES, d // 2), _U32),
                        pltpu.SemaphoreType.DMA((2,))],
        compiler_params=_params(("arbitrary",), 40),
        name="combine_final_norm",
    )(dest3, dest3, x1, rec, mods4, g_final.reshape(1, d), out_sorted)


def _tile(n, pref):
    t = pref
    while n % t:
        t //= 2
    return t


def _moe(h2_p, h2_s, rec_all, counts, w_g_b, w_u_b, b_gu, w_down_b, b_down, tt, bm, chunk):
    t_all = rec_all.shape[0]
    n_experts = w_g_b.shape[0]
    n_blk = -(-(t_all * TOP_K) // bm) + n_experts
    cnt = counts[0, :n_experts].astype(_I32)
    padded = (cnt + bm - 1) // bm * bm
    pends = jnp.cumsum(padded).astype(_I32)
    pstarts = pends - padded
    idx = rec_all[:, REC_IDX:REC_IDX + TOP_K].astype(_I32)
    rank = rec_all[:, REC_RANK:REC_RANK + TOP_K].astype(_I32)
    experts = jnp.arange(n_experts, dtype=_I32)
    dest = rank + jnp.sum(jnp.where(idx[:, :, None] == experts, pstarts, 0), axis=-1)
    blk_row0 = jnp.arange(n_blk, dtype=_I32) * bm
    block_e = jnp.minimum(jnp.sum((pends[None, :] <= blk_row0[:, None]).astype(_I32), axis=1), n_experts - 1)
    n_used = (pends[-1:] // bm).astype(_I32)
    row_end = jnp.sum(jnp.where(block_e[:, None] == experts, pstarts + cnt, 0), axis=-1)
    valid = jnp.clip(row_end - blk_row0, 0, bm).astype(_I32)
    xs = _dispatch(h2_p, h2_s, dest, pends, padded, n_used, n_blk * bm, tt, bm)
    return _expert_ffn(xs, block_e, n_used, valid, w_g_b, w_u_b, b_gu, w_down_b, b_down, bm, chunk), dest


def kernel(x_prompt, x_sample, c_prompt, c_sample, cache_k, cache_v, state_conv, w_ada, b_ada, g_mix, w_in, attn_sink, conv_w, conv_b, conv_ln_g, conv_ln_b, w_out, g_ffn, w_router, b_router, w_gu, b_gu, w_down, b_down, g_final):
    bp, sp, d = x_prompt.shape
    bs, ss, _ = x_sample.shape
    depth = w_ada.shape[0]
    n_experts = w_router.shape[-1]
    tp, ts = bp * sp, bs * ss
    t_all = tp + ts

    tm = _tile(sp, 256)
    to = _tile(sp, 512)
    tq = _tile(sp, 256)
    tc = _tile(sp, 256)
    tt = _tile(ts, 128)
    bm = 512 if tp >= 8192 else 64
    chunk = 512
    assert tp % ts == 0 and tp % tt == 0 and sp % CHUNK == 0 and tq % WINDOW == 0 and tc % CONV_HALO == 0

    xp = x_prompt.reshape(tp, d)
    xsm = x_sample.reshape(ts, d)
    tabs_p = _rope_tables(np.arange(sp))
    tabs_s = _rope_tables(np.tile(PAST_LEN + np.arange(ss), bs))

    new_kp, new_vp, new_cp, new_ks, new_vs, new_cs = [], [], [], [], [], []
    for l in range(depth):
        w_in_b = w_in[l].astype(_BF16)
        wo_a = w_out[l, :ATTN_WIDTH].astype(_BF16)
        wo_c = w_out[l, ATTN_WIDTH:].astype(_BF16)
        wr_pad = jnp.pad(w_router[l], ((0, 0), (0, LANES - n_experts)))
        wr_hi = wr_pad.astype(_BF16)
        wr_lo = (wr_pad - wr_hi.astype(_F32)).astype(_BF16)
        br_pad = jnp.pad(b_router[l], (0, LANES - n_experts)).reshape(1, LANES)

        mods = _modulation(jnp.concatenate([c_prompt, c_sample], axis=0), w_ada[l], b_ada[l])
        mods_p = mods[:, :bp].reshape(6, bp, 1, d)
        mods_s = jnp.repeat(mods[:, bp:], ss, axis=1).reshape(6, 1, ts, d)

        d_ff = w_down.shape[2]
        w_gu2d = w_gu[l].reshape(n_experts * d, 2 * d_ff)

        qp, kp, vp, up, w_u_b = _in_proj(xp, mods_p, g_mix[l], w_in_b, tabs_p, tm, (w_gu2d, d_ff, 1))
        qs, ks, vs, us = _in_proj(xsm, mods_s, g_mix[l], w_in_b, tabs_s, ts)

        att_p, w_g_b = _band_attention(qp, kp, vp, attn_sink[l], bp, sp, tq, (w_gu2d, d_ff, 0))
        w_g_b = w_g_b.reshape(n_experts, d, d_ff)
        w_u_b = w_u_b.reshape(n_experts, d, d_ff)
        win = cache_k.shape[2]
        att_s = _cached_attention(qs, ks, vs, cache_k[l].reshape(bs, win, KV_COLS), cache_v[l].reshape(bs, win, KV_COLS),
                                  attn_sink[l], bs, ss)

        cv_p, w_down_b = _conv_module(up, None, conv_w[l], conv_b[l], conv_ln_g[l], conv_ln_b[l], bp, sp, tc,
                                      (w_down[l].reshape(n_experts * d_ff, d), d, 0))
        w_down_b = w_down_b.reshape(n_experts, d_ff, d)
        hist = jnp.pad(state_conv[l], ((0, 0), (CONV_HALO - (CONV_WIDTH - 1), 0), (0, 0))).reshape(bs * CONV_HALO, CONV_CH)
        cv_s = _conv_module(us, hist, conv_w[l], conv_b[l], conv_ln_g[l], conv_ln_b[l], bs, ss, ss)

        zero_counts = jnp.zeros((1, LANES), _F32)
        x1p, h2p, rec_p, counts = _out_proj(xp, att_p, cv_p, mods_p, g_ffn[l], wo_a, wo_c, wr_hi, wr_lo, br_pad,
                                            zero_counts, to, n_experts)
        x1s, h2s, rec_s, counts = _out_proj(xsm, att_s, cv_s, mods_s, g_ffn[l], wo_a, wo_c, wr_hi, wr_lo, br_pad,
                                            counts, ts, n_experts)

        out_sorted, dest = _moe(h2p, h2s, jnp.concatenate([rec_p, rec_s], axis=0), counts, w_g_b, w_u_b, b_gu[l],
                                w_down_b, b_down[l], tt, bm, chunk)

        last = l == depth - 1
        xp = _combine(x1p, out_sorted, dest[:tp], rec_p, mods_p, g_final, tm, last)
        xsm = _combine(x1s, out_sorted, dest[tp:], rec_s, mods_s, g_final, ts, last)

        keep = min(WINDOW, sp)
        new_kp.append(kp.reshape(bp, sp, N_KV_HEADS, HEAD_DIM)[:, -keep:])
        new_vp.append(vp.reshape(bp, sp, N_KV_HEADS, HEAD_DIM)[:, -keep:])
        new_cp.append(up.reshape(bp, sp, CONV_CH)[:, -(CONV_WIDTH - 1):])
        new_ks.append(ks.reshape(bs, ss, N_KV_HEADS, HEAD_DIM))
        new_vs.append(vs.reshape(bs, ss, N_KV_HEADS, HEAD_DIM))
        new_cs.append(jnp.concatenate([state_conv[l], us.reshape(bs, ss, CONV_CH)], axis=1)[:, -(CONV_WIDTH - 1):])

    return (xp.reshape(bp, sp, d), xsm.reshape(bs, ss, d), jnp.stack(new_kp), jnp.stack(new_vp), jnp.stack(new_cp),
            jnp.stack(new_ks), jnp.stack(new_vs), jnp.stack(new_cs))
```

```python
import functools

import jax
import jax.numpy as jnp
import numpy as np
from jax import lax
from jax.experimental import pallas as pl
from jax.experimental.pallas import tpu as pltpu

D_MODEL = 2048
CHUNK = 64
HEAD_DIM = 64
ATTN_WIDTH = D_MODEL // 2
CONV_CH = D_MODEL - ATTN_WIDTH
N_HEADS = ATTN_WIDTH // HEAD_DIM
N_KV_HEADS = N_HEADS // 4
GQA_GROUP = N_HEADS // N_KV_HEADS
ROT_DIM = HEAD_DIM // 4
ROPE_THETA = 500000.0
WINDOW = 128
WINDOW_CHUNKS = WINDOW // CHUNK
CONV_WIDTH = 31
TOP_K = 4
SWIGLU_LIMIT = 7.0
SWIGLU_ALPHA = 1.702
EPS = 1e-5
NEG_INF = -1e30
PAST_LEN = 2048
Q_COLS = N_HEADS * HEAD_DIM
KV_COLS = N_KV_HEADS * HEAD_DIM
IN_COLS = Q_COLS + 2 * KV_COLS + 2 * CONV_CH
ATTN_SCALE = HEAD_DIM ** -0.5
assert ATTN_SCALE == 0.125

LANES = 128
SUBLANES = 8
CONV_HALO = 32
MIB = 1024 * 1024

_BF16 = jnp.bfloat16
_F32 = jnp.float32
_I32 = jnp.int32
_U32 = jnp.uint32


def _params(semantics, vmem_mib):
    return pltpu.CompilerParams(dimension_semantics=semantics, vmem_limit_bytes=vmem_mib * MIB)


def _dot(a, b):
    return jnp.dot(a, b, preferred_element_type=_F32)


def _pack_pair(lo, hi):
    lo = lax.bitcast_convert_type(lo.astype(_BF16).astype(_F32), _U32)
    hi = lax.bitcast_convert_type(hi.astype(_BF16).astype(_F32), _U32)
    return (lo >> 16) | (hi & jnp.uint32(0xFFFF0000))


def _pack_halves(x):
    w = x.shape[1] // 2
    return _pack_pair(x[:, :w], x[:, w:])


def _unpack_halves(p):
    return (lax.bitcast_convert_type(p << 16, _F32), lax.bitcast_convert_type(p & jnp.uint32(0xFFFF0000), _F32))


def _rmsnorm_mod(x, g, scale, shift):
    ms = jnp.mean(x * x, axis=-1, keepdims=True)
    return (x * lax.rsqrt(ms + EPS) * g) * (1.0 + scale) + shift


def _ada_body(c_ref, w_ref, b_ref, o_ref):
    c = c_ref[...]
    s = (c * jax.nn.sigmoid(c)).astype(_BF16)
    o_ref[...] = _dot(s, w_ref[...].astype(_BF16)) + b_ref[...]


def _modulation(c, w_ada, b_ada):
    n, d = c.shape
    cols = w_ada.shape[1]
    tn = 1024
    per = d // tn
    return pl.pallas_call(
        _ada_body,
        grid=(cols // tn,),
        in_specs=[
            pl.BlockSpec((n, d), lambda j: (0, 0)),
            pl.BlockSpec((d, tn), lambda j: (0, j)),
            pl.BlockSpec((1, tn), lambda j: (0, j)),
        ],
        out_specs=pl.BlockSpec((None, n, tn), lambda j: (j // per, 0, j % per)),
        out_shape=jax.ShapeDtypeStruct((cols // d, n, d), _F32),
        compiler_params=_params(("arbitrary",), 40),
        name="modulation",
    )(c, w_ada, b_ada.reshape(1, cols))


def _rope_tables(pos):
    half = ROT_DIM // 2
    inv_freq = np.power(np.float32(ROPE_THETA), -np.arange(0, ROT_DIM, 2, dtype=np.float32) / np.float32(ROT_DIM))
    ang = (pos.astype(np.float32)[:, None] * inv_freq[None, :].astype(np.float32)).astype(np.float64)
    cos, sin = np.cos(ang), np.sin(ang)
    d = np.arange(LANES) % HEAD_DIM
    cos_l = np.where(d < ROT_DIM, cos[:, d % half], 1.0)
    sa = np.where(d < half, -sin[:, d % half], 0.0)
    sb = np.where((d >= half) & (d < ROT_DIM), sin[:, d % half], 0.0)
    return tuple(jnp.asarray(t.astype(np.float32)) for t in (cos_l, sa, sb))


def _rotate(z, cos, sa, sb):
    half = ROT_DIM // 2
    parts = []
    for j in range(z.shape[1] // LANES):
        zj = z[:, j * LANES:(j + 1) * LANES]
        parts.append(zj * cos + pltpu.roll(zj, LANES - half, 1) * sa + pltpu.roll(zj, half, 1) * sb)
    return parts[0] if len(parts) == 1 else jnp.concatenate(parts, axis=1)


def _in_proj_body(x_ref, mod_ref, g_ref, w_ref, cos_ref, sa_ref, sb_ref, *rest):
    q_ref, k_ref, v_ref, u_ref = rest[-4:] if len(rest) == 4 else rest[1:5]
    if len(rest) == 6:
        rest[5][...] = rest[0][...].astype(_BF16)
    h = _rmsnorm_mod(x_ref[...], g_ref[...], mod_ref[1, 0], mod_ref[0, 0]).astype(_BF16)
    cos, sa, sb = cos_ref[...], sa_ref[...], sb_ref[...]
    cw = 512
    for c in range(Q_COLS // cw):
        z = _dot(h, w_ref[:, c * cw:(c + 1) * cw])
        q_ref[:, c * cw:(c + 1) * cw] = (_rotate(z, cos, sa, sb) * ATTN_SCALE).astype(_BF16)
    z = _dot(h, w_ref[:, Q_COLS:Q_COLS + 2 * KV_COLS])
    k_ref[...] = _rotate(z[:, :KV_COLS], cos, sa, sb)
    v_ref[...] = z[:, KV_COLS:]
    o = Q_COLS + 2 * KV_COLS
    for c in range(CONV_CH // cw):
        zv = _dot(h, w_ref[:, o + c * cw:o + (c + 1) * cw])
        zg = _dot(h, w_ref[:, o + CONV_CH + c * cw:o + CONV_CH + (c + 1) * cw])
        u_ref[:, c * cw:(c + 1) * cw] = zv * jax.nn.sigmoid(zg)


def _mod_spec(mods4, tiles_per_group):
    _, _, r, d = mods4.shape
    return pl.BlockSpec((6, 1, r, d), lambda i: (0, i // tiles_per_group, 0, 0))


def _in_proj(x2d, mods4, g_mix, w_in_b, tabs, tm, rider=None):
    t, d = x2d.shape
    n_tiles = t // tm
    tiles_per_group = n_tiles // mods4.shape[1]
    pos_tiles = tabs[0].shape[0] // tm
    tab_spec = pl.BlockSpec((tm, LANES), lambda i: (i % pos_tiles, 0))
    row = lambda w: pl.BlockSpec((tm, w), lambda i: (i, 0))
    in_specs = [
        row(d),
        _mod_spec(mods4, tiles_per_group),
        pl.BlockSpec((1, d), lambda i: (0, 0)),
        pl.BlockSpec((d, IN_COLS), lambda i: (0, 0), pipeline_mode=pl.Buffered(1)),
        tab_spec, tab_spec, tab_spec,
    ]
    out_specs = [row(Q_COLS), row(KV_COLS), row(KV_COLS), row(CONV_CH)]
    out_shape = [
        jax.ShapeDtypeStruct((t, Q_COLS), _BF16),
        jax.ShapeDtypeStruct((t, KV_COLS), _F32),
        jax.ShapeDtypeStruct((t, KV_COLS), _F32),
        jax.ShapeDtypeStruct((t, CONV_CH), _F32),
    ]
    args = [x2d, mods4, g_mix.reshape(1, d), w_in_b, *tabs]
    ride = None if rider is None else _rider_specs(rider, n_tiles, lambda i: i)
    if ride is not None:
        in_specs.append(ride[0])
        out_specs.append(ride[1])
        out_shape.append(ride[2])
        args.append(rider[0])
    res = pl.pallas_call(
        _in_proj_body,
        grid=(n_tiles,),
        in_specs=in_specs,
        out_specs=out_specs,
        out_shape=out_shape,
        compiler_params=_params(("arbitrary",), 56),
        name="in_proj",
    )(*args)
    if rider is not None and ride is None:
        res = list(res) + [_rider_fallback(rider)]
    return res


def _attn_group(qg, kk, vv, sink_col, valid):
    s = lax.dot_general(qg, kk, (((1,), (1,)), ((), ())), preferred_element_type=_F32)
    if valid is not None:
        s = jnp.where(valid, s, NEG_INF)
    m = jnp.maximum(jnp.max(s, axis=-1, keepdims=True), sink_col)
    p = jnp.exp(s - m)
    den = jnp.sum(p, axis=-1, keepdims=True) + jnp.exp(sink_col - m)
    return _dot(p.astype(_BF16), vv) / den


def _sink_column(sink_ref, kh, rows_per_head):
    row = lax.broadcasted_iota(_I32, (GQA_GROUP * rows_per_head, 1), 0)
    col = jnp.full(row.shape, sink_ref[kh * GQA_GROUP + GQA_GROUP - 1], _F32)
    for g in range(GQA_GROUP - 2, -1, -1):
        col = jnp.where(row < (g + 1) * rows_per_head, sink_ref[kh * GQA_GROUP + g], col)
    return col


def _store_heads(o_ref, r0, rows, kh, o):
    for pair in range(GQA_GROUP // 2):
        both = jnp.concatenate([o[(2 * pair) * rows:(2 * pair + 1) * rows],
                                o[(2 * pair + 1) * rows:(2 * pair + 2) * rows]], axis=1)
        c0 = (kh * GQA_GROUP + 2 * pair) * HEAD_DIM
        o_ref[r0:r0 + rows, c0:c0 + 2 * HEAD_DIM] = both.astype(o_ref.dtype)


def _stack_heads(q, r0, rows, kh):
    return jnp.concatenate(
        [q[r0:r0 + rows, (kh * GQA_GROUP + g) * HEAD_DIM:(kh * GQA_GROUP + g + 1) * HEAD_DIM] for g in range(GQA_GROUP)],
        axis=0)


def _rider_specs(rider, n_steps, step_index):
    arr, width, colblk = rider
    rows = arr.shape[0]
    per = rows // n_steps
    if rows % n_steps or per % 16 or per * width * 4 > 8 * MIB:
        return None
    return (pl.BlockSpec((per, width), lambda *ids: (step_index(*ids), colblk)),
            pl.BlockSpec((per, width), lambda *ids: (step_index(*ids), 0)),
            jax.ShapeDtypeStruct((rows, width), _BF16))


def _rider_fallback(rider):
    arr, width, colblk = rider
    return arr[:, colblk * width:(colblk + 1) * width].astype(_BF16)


def _band_attn_body(sink_ref, q_ref, kc_ref, kh_ref, vc_ref, vh_ref, *rest, tq):
    o_ref = rest[-1] if len(rest) == 1 else rest[1]
    if len(rest) == 3:
        rest[2][...] = rest[0][...].astype(_BF16)
    j = pl.program_id(1)
    q = q_ref[...]
    kall = jnp.concatenate([kh_ref[...], kc_ref[...]], axis=0)
    vall = jnp.concatenate([vh_ref[...], vc_ref[...]], axis=0)
    band = (WINDOW_CHUNKS + 1) * CHUNK
    key_chunk = lax.broadcasted_iota(_I32, (1, band), 1) // CHUNK
    for kh in range(N_KV_HEADS):
        sink_col = _sink_column(sink_ref, kh, CHUNK)
        kk = kall[:, kh * HEAD_DIM:(kh + 1) * HEAD_DIM].astype(_BF16)
        vv = vall[:, kh * HEAD_DIM:(kh + 1) * HEAD_DIM].astype(_BF16)
        for ci in range(tq // CHUNK):
            qg = _stack_heads(q, ci * CHUNK, CHUNK, kh)
            valid = None if ci >= WINDOW_CHUNKS else (j * (tq // CHUNK) + ci - WINDOW_CHUNKS + key_chunk) >= 0
            o = _attn_group(qg, kk[ci * CHUNK:ci * CHUNK + band], vv[ci * CHUNK:ci * CHUNK + band], sink_col, valid)
            _store_heads(o_ref, ci * CHUNK, CHUNK, kh, o)


def _band_attention(q, k, v, sink, batch, seq, tq, rider):
    nq = seq // tq
    r = tq // WINDOW
    cur = lambda w: pl.BlockSpec((tq, w), lambda b, j: (b * nq + j, 0))
    halo = pl.BlockSpec((WINDOW, KV_COLS), lambda b, j: (b * nq * r + jnp.maximum(j * r - 1, 0), 0))
    in_specs = [pl.BlockSpec(memory_space=pltpu.SMEM), cur(Q_COLS), cur(KV_COLS), halo, cur(KV_COLS), halo]
    out_specs = [cur(Q_COLS)]
    out_shape = [jax.ShapeDtypeStruct((batch * seq, Q_COLS), _BF16)]
    args = [sink, q, k, k, v, v]
    ride = _rider_specs(rider, batch * nq, lambda b, j: b * nq + j)
    if ride is not None:
        in_specs.append(ride[0])
        out_specs.append(ride[1])
        out_shape.append(ride[2])
        args.append(rider[0])
    res = pl.pallas_call(
        functools.partial(_band_attn_body, tq=tq),
        grid=(batch, nq),
        in_specs=in_specs,
        out_specs=out_specs,
        out_shape=out_shape,
        compiler_params=_params(("arbitrary", "arbitrary"), 48),
        name="band_attention",
    )(*args)
    return (res[0], res[1]) if ride is not None else (res[0], _rider_fallback(rider))


def _cached_attn_body(sink_ref, q_ref, kn_ref, kc_ref, vn_ref, vc_ref, o_ref, *, rows):
    q = q_ref[...]
    kall = jnp.concatenate([kc_ref[0], kn_ref[...]], axis=0)
    vall = jnp.concatenate([vc_ref[0], vn_ref[...]], axis=0)
    for kh in range(N_KV_HEADS):
        o = _attn_group(_stack_heads(q, 0, rows, kh), kall[:, kh * HEAD_DIM:(kh + 1) * HEAD_DIM].astype(_BF16),
                        vall[:, kh * HEAD_DIM:(kh + 1) * HEAD_DIM].astype(_BF16), _sink_column(sink_ref, kh, rows), None)
        _store_heads(o_ref, 0, rows, kh, o)


def _cached_attention(q, k, v, cache_k, cache_v, sink, batch, rows):
    win = cache_k.shape[1]
    new = lambda w: pl.BlockSpec((rows, w), lambda b: (b, 0))
    cache = pl.BlockSpec((1, win, KV_COLS), lambda b: (b, 0, 0))
    return pl.pallas_call(
        functools.partial(_cached_attn_body, rows=rows),
        grid=(batch,),
        in_specs=[pl.BlockSpec(memory_space=pltpu.SMEM), new(Q_COLS), new(KV_COLS), cache, new(KV_COLS), cache],
        out_specs=new(Q_COLS),
        out_shape=jax.ShapeDtypeStruct((batch * rows, Q_COLS), _BF16),
        compiler_params=_params(("arbitrary",), 40),
        name="cached_attention",
    )(sink, q, k, cache_k, v, cache_v)


def _conv_body(uc_ref, uh_ref, w_ref, b_ref, g_ref, beta_ref, *rest, tt, zero_first_halo):
    win_ref, y_ref = rest[-2:]
    o_ref = rest[0] if len(rest) == 3 else rest[1]
    if len(rest) == 5:
        rest[2][...] = rest[0][...].astype(_BF16)
    halo = uh_ref[...]
    if zero_first_halo:
        halo = jnp.where(pl.program_id(1) == 0, 0.0, halo)
    win_ref[0:CONV_HALO, :] = halo
    win_ref[CONV_HALO:, :] = uc_ref[...]
    lead = CONV_HALO - (CONV_WIDTH - 1)
    by_shift = {}
    for tap in range(CONV_WIDTH):
        by_shift.setdefault((lead + tap) % SUBLANES, []).append(tap)
    rb = min(tt, 128)
    for s in range(CONV_CH // LANES):
        cs = slice(s * LANES, (s + 1) * LANES)
        for r0 in range(0, tt, rb):
            y = b_ref[:, cs]
            for sh, taps in sorted(by_shift.items()):
                ext = SUBLANES if sh else 0
                q = None
                for tap in taps:
                    base = r0 + lead + tap - sh
                    term = w_ref[tap:tap + 1, cs] * win_ref[base:base + rb + ext, cs]
                    q = term if q is None else q + term
                y = y + (q[sh:sh + rb] if sh else q)
            y_ref[r0:r0 + rb, cs] = y
    y = y_ref[...]
    mu = jnp.mean(y, axis=-1, keepdims=True)
    var = jnp.mean(jnp.square(y - mu), axis=-1, keepdims=True)
    z = (y - mu) * lax.rsqrt(var + EPS) * g_ref[...] + beta_ref[...]
    o_ref[...] = (z * jax.nn.sigmoid(z)).astype(o_ref.dtype)


def _conv_module(u, hist, conv_w, conv_b, ln_g, ln_b, batch, seq, tt, rider=None):
    nt = seq // tt
    r = tt // CONV_HALO
    cur = pl.BlockSpec((tt, CONV_CH), lambda b, j: (b * nt + j, 0))
    if hist is None:
        hist_arr = u
        halo = pl.BlockSpec((CONV_HALO, CONV_CH), lambda b, j: (b * nt * r + jnp.maximum(j * r - 1, 0), 0))
    else:
        assert nt == 1
        hist_arr = hist
        halo = pl.BlockSpec((CONV_HALO, CONV_CH), lambda b, j: (b, 0))
    vec = pl.BlockSpec((1, CONV_CH), lambda b, j: (0, 0))
    wpad = jnp.pad(conv_w.reshape(CONV_WIDTH, CONV_CH), ((0, 1), (0, 0)))
    in_specs = [cur, halo, pl.BlockSpec((CONV_WIDTH + 1, CONV_CH), lambda b, j: (0, 0)), vec, vec, vec]
    out_specs = [cur]
    out_shape = [jax.ShapeDtypeStruct((batch * seq, CONV_CH), _BF16)]
    args = [u, hist_arr, wpad, conv_b.reshape(1, -1), ln_g.reshape(1, -1), ln_b.reshape(1, -1)]
    ride = None if rider is None else _rider_specs(rider, batch * nt, lambda b, j: b * nt + j)
    if ride is not None:
        in_specs.append(ride[0])
        out_specs.append(ride[1])
        out_shape.append(ride[2])
        args.append(rider[0])
    res = pl.pallas_call(
        functools.partial(_conv_body, tt=tt, zero_first_halo=hist is None),
        grid=(batch, nt),
        in_specs=in_specs,
        out_specs=out_specs,
        out_shape=out_shape,
        scratch_shapes=[pltpu.VMEM((tt + CONV_HALO, CONV_CH), _F32), pltpu.VMEM((tt, CONV_CH), _F32)],
        compiler_params=_params(("arbitrary", "arbitrary"), 40),
        name="conv_module",
    )(*args)
    if rider is None:
        return res[0]
    return (res[0], res[1]) if ride is not None else (res[0], _rider_fallback(rider))


REC_IDX, REC_GATE, REC_RANK = 0, TOP_K, 2 * TOP_K


def _out_proj_body(x_ref, a_ref, c_ref, mod_ref, g_ref, wa_ref, wc_ref, wrc_ref, br_ref, cnt_ref,
                   x1_ref, h2_ref, rec_ref, cnt_out_ref, run_ref, *, n_experts):
    tm = x_ref.shape[0]

    @pl.when(pl.program_id(0) == 0)
    def _():
        run_ref[...] = cnt_ref[...]

    o = _dot(a_ref[...], wa_ref[...]) + _dot(c_ref[...], wc_ref[...])
    x1 = x_ref[...] + mod_ref[2, 0] * o
    x1_ref[...] = x1
    h2 = _rmsnorm_mod(x1, g_ref[...], mod_ref[4, 0], mod_ref[3, 0])
    h2_ref[...] = _pack_halves(h2)

    h2_hi = h2.astype(_BF16)
    h2_lo = (h2 - h2_hi.astype(_F32)).astype(_BF16)
    both = _dot(h2_hi, wrc_ref[...])
    logits = both[:, :LANES] + (_dot(h2_lo, wrc_ref[:, :LANES]) + both[:, LANES:]) + br_ref[...]
    lane = lax.broadcasted_iota(_I32, (tm, LANES), 1)
    work = jnp.where(lane < n_experts, logits, -jnp.inf)
    vals, hots = [], []
    rec = jnp.zeros((tm, LANES), _F32)
    for k in range(TOP_K):
        m = jnp.max(work, axis=-1, keepdims=True)
        idx = jnp.min(jnp.where(work == m, lane, LANES), axis=-1, keepdims=True)
        hot = lane == idx
        work = jnp.where(hot, -jnp.inf, work)
        vals.append(m)
        hots.append(hot)
        rec = jnp.where(lane == REC_IDX + k, idx.astype(_F32), rec)
    exps = [jnp.exp(v - vals[0]) for v in vals]
    den = exps[0]
    for e in exps[1:]:
        den = den + e
    for k in range(TOP_K):
        rec = jnp.where(lane == REC_GATE + k, exps[k] / den, rec)

    chosen = jnp.zeros((tm, LANES), _F32)
    for hot in hots:
        chosen = jnp.where(hot, 1.0, chosen)
    r_i = lax.broadcasted_iota(_I32, (tm, tm), 0)
    c_i = lax.broadcasted_iota(_I32, (tm, tm), 1)
    before = _dot(jnp.where(c_i < r_i, 1.0, 0.0).astype(_BF16), chosen.astype(_BF16)) + run_ref[...]
    for k in range(TOP_K):
        rank = jnp.sum(jnp.where(hots[k], before, 0.0), axis=-1, keepdims=True)
        rec = jnp.where(lane == REC_RANK + k, rank, rec)
    rec_ref[...] = rec
    run_ref[...] = run_ref[...] + jnp.sum(chosen, axis=0, keepdims=True)
    cnt_out_ref[...] = run_ref[...]


def _out_proj(x2d, attn, conv, mods4, g_ffn, wo_a, wo_c, wr_cat, br_pad, counts, tm, n_experts):
    t, d = x2d.shape
    n_tiles = t // tm
    tiles_per_group = n_tiles // mods4.shape[1]
    row = lambda w: pl.BlockSpec((tm, w), lambda i: (i, 0))
    const = lambda a, b: pl.BlockSpec((a, b), lambda i: (0, 0))
    return pl.pallas_call(
        functools.partial(_out_proj_body, n_experts=n_experts),
        grid=(n_tiles,),
        in_specs=[row(d), row(ATTN_WIDTH), row(CONV_CH), _mod_spec(mods4, tiles_per_group), const(1, d),
                  const(ATTN_WIDTH, d), const(CONV_CH, d), const(d, 2 * LANES), const(1, LANES), const(1, LANES)],
        out_specs=[row(d), row(d // 2), row(LANES), const(1, LANES)],
        out_shape=[
            jax.ShapeDtypeStruct((t, d), _F32),
            jax.ShapeDtypeStruct((t, d // 2), _U32),
            jax.ShapeDtypeStruct((t, LANES), _F32),
            jax.ShapeDtypeStruct((1, LANES), _F32),
        ],
        scratch_shapes=[pltpu.VMEM((1, LANES), _F32)],
        compiler_params=_params(("arbitrary",), 56),
        name="out_proj_router",
    )(x2d, attn, conv, mods4, g_ffn.reshape(1, d), wo_a, wo_c, wr_cat, br_pad, counts)


def _row_copy(src, dst, sem):
    return pltpu.make_async_copy(src, dst, sem)


def _rows(ref, row0, n):
    return ref.at[pl.ds(row0, n), 0]


def _dispatch_body(pends_ref, padded_ref, nu_ref, dest_ref, hp_ref, hs_ref, xs_ref, stage_ref, zeros_ref, sems, zsem, *,
                   tt, bm, n_experts, prompt_steps):
    i = pl.program_id(0)
    n = pl.num_programs(0)
    n_blk = xs_ref.shape[0] // bm
    slot = i % 2

    def zero_block(row0):
        return _row_copy(zeros_ref, _rows(xs_ref, pl.multiple_of(row0, bm), bm), zsem)

    @pl.when(i == 0)
    def _():
        zeros_ref[...] = jnp.zeros(zeros_ref.shape, zeros_ref.dtype)
        for e in range(n_experts):
            @pl.when(padded_ref[e] > 0)
            def _():
                zero_block(pends_ref[e] - bm).start()

        def tail_start(b, carry):
            zero_block(b * bm).start()
            return carry

        def tail_wait(b, carry):
            zero_block(b * bm).wait()
            return carry

        lax.fori_loop(nu_ref[0], n_blk, tail_start, 0)
        for e in range(n_experts):
            @pl.when(padded_ref[e] > 0)
            def _():
                zero_block(pends_ref[e] - bm).wait()
        lax.fori_loop(nu_ref[0], n_blk, tail_wait, 0)

    @pl.when(i < prompt_steps)
    def _():
        stage_ref[slot] = hp_ref[...]

    @pl.when(i >= prompt_steps)
    def _():
        stage_ref[slot] = hs_ref[...]

    for r in range(tt):
        src = stage_ref.at[slot, pl.ds(r, 1)]
        for k in range(TOP_K):
            _row_copy(src, xs_ref.at[dest_ref[0, 0, r * TOP_K + k]], sems.at[slot]).start(priority=k % 2)

    def drain(s):
        for _ in range(TOP_K):
            _row_copy(stage_ref.at[s], _rows(xs_ref, 0, tt), sems.at[s]).wait()

    @pl.when(i > 0)
    def _():
        drain(1 - slot)

    @pl.when(i == n - 1)
    def _():
        drain(slot)


def _dispatch(h2_p, h2_s, dest, pends, padded, n_used, rows, tt, bm):
    w = h2_p.shape[1]
    tp, ts = h2_p.shape[0], h2_s.shape[0]
    n_experts = pends.shape[0]
    prompt_steps = tp // tt
    grid_spec = pltpu.PrefetchScalarGridSpec(
        num_scalar_prefetch=3,
        grid=((tp + ts) // tt,),
        in_specs=[
            pl.BlockSpec((1, 1, tt * TOP_K), lambda i, *_: (i, 0, 0), memory_space=pltpu.SMEM),
            pl.BlockSpec((tt, w), lambda i, *_: (jnp.minimum(i, prompt_steps - 1), 0)),
            pl.BlockSpec((tt, w), lambda i, *_: (jnp.maximum(i - prompt_steps, 0), 0)),
        ],
        out_specs=pl.BlockSpec(memory_space=pl.ANY),
        scratch_shapes=[pltpu.VMEM((2, tt, w), _U32), pltpu.VMEM((bm, w), _U32),
                        pltpu.SemaphoreType.DMA((2,)), pltpu.SemaphoreType.DMA(())],
    )
    return pl.pallas_call(
        functools.partial(_dispatch_body, tt=tt, bm=bm, n_experts=n_experts, prompt_steps=prompt_steps),
        grid_spec=grid_spec,
        out_shape=jax.ShapeDtypeStruct((rows, 1, w), _U32),
        compiler_params=_params(("arbitrary",), 24),
        name="moe_dispatch",
    )(pends, padded, n_used, dest.reshape((tp + ts) // tt, 1, tt * TOP_K), h2_p, h2_s)


def _ffn_body(be_ref, nu_ref, valid_ref, bgu_ref, bd_ref, xs_hbm, wg_hbm, wu_hbm, wd_hbm, o_hbm, wg_ref, wu_ref, wd_ref,
              xbuf_ref, obuf_ref, xb_ref, a_ref, wsems, xsems, osems, *, bm, n_blk, chunk):
    i = pl.program_id(0)
    n_used = nu_ref[0]
    d_ff = wd_ref.shape[0]
    half = xbuf_ref.shape[2]
    slot = i % 2

    def fetch(b, s):
        return _row_copy(_rows(xs_hbm, pl.multiple_of(b * bm, bm), bm), xbuf_ref.at[s], xsems.at[s])

    def put(b, s):
        return _row_copy(obuf_ref.at[s], _rows(o_hbm, pl.multiple_of(b * bm, bm), bm), osems.at[s])

    class gu_copy:
        def __init__(self, e):
            self.copies = (_row_copy(wg_hbm.at[e], wg_ref, wsems.at[0]), _row_copy(wu_hbm.at[e], wu_ref, wsems.at[1]))

        def start(self):
            for c in self.copies:
                c.start()

        def wait(self):
            for c in self.copies:
                c.wait()

    def down_copy(e):
        return _row_copy(wd_hbm.at[e], wd_ref, wsems.at[2])

    @pl.when(i == 0)
    def _():
        fetch(0, 0).start()

    @pl.when(i + 1 < n_used)
    def _():
        fetch(i + 1, 1 - slot).start()

    @pl.when(i >= 2)
    def _():
        put(i - 2, slot).wait()

    @pl.when(i < n_used)
    def _():
        e = be_ref[i]
        first = (i == 0) | (e != be_ref[jnp.maximum(i - 1, 0)])
        e_next = be_ref[jnp.minimum(i + 1, n_used - 1)]

        @pl.when(i == 0)
        def _():
            gu_copy(e).start()

        @pl.when(first)
        def _():
            down_copy(e).start()
            gu_copy(e).wait()

        fetch(i, slot).wait()

        def ffn(m):
            lo, hi = _unpack_halves(xbuf_ref[slot, :m])
            xb_ref[:m, :half] = lo.astype(_BF16)
            xb_ref[:m, half:] = hi.astype(_BF16)
            xb = xb_ref[:m]
            for c in range(d_ff // chunk):
                cs = slice(c * chunk, (c + 1) * chunk)
                us = slice(d_ff + c * chunk, d_ff + (c + 1) * chunk)
                g = jnp.minimum(_dot(xb, wg_ref[:, cs]) + bgu_ref[:, cs], SWIGLU_LIMIT)
                u = jnp.clip(_dot(xb, wu_ref[:, cs]) + bgu_ref[:, us], -SWIGLU_LIMIT, SWIGLU_LIMIT)
                a_ref[:m, cs] = (g * jax.nn.sigmoid(SWIGLU_ALPHA * g) * (u + 1.0)).astype(_BF16)

            @pl.when(e_next != e)
            def _():
                gu_copy(e_next).start()

            @pl.when(first)
            def _():
                down_copy(e).wait()

            a = a_ref[:m]
            for c in range(half // chunk):
                cs = slice(c * chunk, (c + 1) * chunk)
                hs = slice(half + c * chunk, half + (c + 1) * chunk)
                o_lo = _dot(a, wd_ref[:, cs]) + bd_ref[:, cs]
                o_hi = _dot(a, wd_ref[:, hs]) + bd_ref[:, hs]
                obuf_ref[slot, :m, cs] = _pack_pair(o_lo, o_hi)
            if m < bm:
                obuf_ref[slot, m:] = jnp.zeros((bm - m, half), obuf_ref.dtype)

        short = valid_ref[i] <= bm // 2

        @pl.when(jnp.logical_not(short))
        def _():
            ffn(bm)

        @pl.when(short)
        def _():
            ffn(bm // 2)

    @pl.when(i >= n_used)
    def _():
        obuf_ref[slot] = jnp.zeros(obuf_ref.shape[1:], obuf_ref.dtype)

    put(i, slot).start()

    @pl.when(i == n_blk - 1)
    def _():
        put(i, slot).wait()
        if n_blk > 1:
            put(i - 1, 1 - slot).wait()


def _expert_ffn(xs, block_e, n_used, valid, w_g_b, w_u_b, b_gu, w_down_b, b_down, bm, chunk):
    rows, _, half = xs.shape
    d = 2 * half
    n_experts, _, d_ff = w_g_b.shape
    two_ff = 2 * d_ff
    n_blk = rows // bm

    def blk(i, nu):
        return jnp.minimum(i, nu[0] - 1)

    hbm = pl.BlockSpec(memory_space=pl.ANY)
    grid_spec = pltpu.PrefetchScalarGridSpec(
        num_scalar_prefetch=3,
        grid=(n_blk,),
        in_specs=[
            pl.BlockSpec((None, 1, two_ff), lambda i, be, nu, vr: (be[blk(i, nu)], 0, 0)),
            pl.BlockSpec((None, 1, d), lambda i, be, nu, vr: (be[blk(i, nu)], 0, 0)),
            hbm, hbm, hbm, hbm,
        ],
        out_specs=hbm,
        scratch_shapes=[pltpu.VMEM((d, d_ff), _BF16), pltpu.VMEM((d, d_ff), _BF16), pltpu.VMEM((d_ff, d), _BF16),
                        pltpu.VMEM((2, bm, half), _U32), pltpu.VMEM((2, bm, half), _U32),
                        pltpu.VMEM((bm, d), _BF16), pltpu.VMEM((bm, d_ff), _BF16),
                        pltpu.SemaphoreType.DMA((3,)), pltpu.SemaphoreType.DMA((2,)), pltpu.SemaphoreType.DMA((2,))],
    )
    return pl.pallas_call(
        functools.partial(_ffn_body, bm=bm, n_blk=n_blk, chunk=chunk),
        grid_spec=grid_spec,
        out_shape=jax.ShapeDtypeStruct((rows, 1, half), _U32),
        compiler_params=_params(("arbitrary",), 56),
        name="expert_ffn",
    )(block_e, n_used, valid, b_gu.reshape(n_experts, 1, two_ff), b_down.reshape(n_experts, 1, d), xs, w_g_b, w_u_b,
      w_down_b)


def _combine_body(dcur_ref, dnext_ref, x1_ref, rec_ref, mod_ref, gf_ref, out_ref, y_ref, gbuf_ref, sems, *, final_norm):
    i = pl.program_id(0)
    n = pl.num_programs(0)
    tm, d = x1_ref.shape
    w = d // 2
    groups = tm // SUBLANES
    slot = i % 2

    def start_group(dest_ref, s, g):
        for j in range(SUBLANES):
            for k in range(TOP_K):
                _row_copy(out_ref.at[dest_ref[0, 0, (g * SUBLANES + j) * TOP_K + k]], gbuf_ref.at[s, k, g, pl.ds(j, 1)],
                          sems.at[s]).start(priority=k % 2)

    def wait_tile(s):
        for k in range(TOP_K):
            for g in range(groups):
                _row_copy(_rows(out_ref, 0, SUBLANES), gbuf_ref.at[s, k, g], sems.at[s]).wait()

    @pl.when(i == 0)
    def _():
        def issue(g, carry):
            start_group(dcur_ref, 0, g)
            return carry
        lax.fori_loop(0, groups, issue, 0)

    wait_tile(slot)
    rec = rec_ref[...]
    gate2 = mod_ref[5, 0]
    for g in range(groups):
        start_group(dnext_ref, 1 - slot, g)
        rows = slice(g * SUBLANES, (g + 1) * SUBLANES)
        lo = hi = None
        for k in range(TOP_K):
            l, h = _unpack_halves(gbuf_ref[slot, k, g])
            gk = rec[rows, REC_GATE + k:REC_GATE + k + 1]
            lo = l * gk if lo is None else lo + l * gk
            hi = h * gk if hi is None else hi + h * gk
        g2 = gate2 if gate2.shape[0] == 1 else gate2[rows]
        x2l = x1_ref[rows, :w] + g2[:, :w] * lo
        x2h = x1_ref[rows, w:] + g2[:, w:] * hi
        if final_norm:
            ms = (jnp.sum(x2l * x2l, axis=-1, keepdims=True) + jnp.sum(x2h * x2h, axis=-1, keepdims=True)) * (1.0 / d)
            inv = lax.rsqrt(ms + EPS)
            x2l = x2l * inv * gf_ref[:, :w]
            x2h = x2h * inv * gf_ref[:, w:]
        y_ref[rows, :w] = x2l
        y_ref[rows, w:] = x2h

    @pl.when(i == n - 1)
    def _():
        wait_tile(1 - slot)


def _combine(x1, out_sorted, dest, rec, mods4, g_final, tm, final_norm):
    t, d = x1.shape
    n_tiles = t // tm
    tiles_per_group = n_tiles // mods4.shape[1]
    dest3 = dest.reshape(n_tiles, 1, tm * TOP_K)
    dspec = lambda nxt: pl.BlockSpec((1, 1, tm * TOP_K), lambda i: (jnp.minimum(i + nxt, n_tiles - 1), 0, 0),
                                     memory_space=pltpu.SMEM)
    return pl.pallas_call(
        functools.partial(_combine_body, final_norm=final_norm),
        grid=(n_tiles,),
        in_specs=[
            dspec(0), dspec(1),
            pl.BlockSpec((tm, d), lambda i: (i, 0)),
            pl.BlockSpec((tm, LANES), lambda i: (i, 0)),
            _mod_spec(mods4, tiles_per_group),
            pl.BlockSpec((1, d), lambda i: (0, 0)),
            pl.BlockSpec(memory_space=pl.ANY),
        ],
        out_specs=pl.BlockSpec((tm, d), lambda i: (i, 0)),
        out_shape=jax.ShapeDtypeStruct((t, d), _F32),
        scratch_shapes=[pltpu.VMEM((2, TOP_K, tm // SUBLANES, SUBLANES, d // 2), _U32),
                        pltpu.SemaphoreType.DMA((2,))],
        compiler_params=_params(("arbitrary",), 40),
        name="combine_final_norm",
    )(dest3, dest3, x1, rec, mods4, g_final.reshape(1, d), out_sorted)


def _tile(n, pref):
    t = pref
    while n % t:
        t //= 2
    return t


def _moe(h2_p, h2_s, rec_all, counts, w_g_b, w_u_b, b_gu, w_down_b, b_down, tt, bm, chunk):
    t_all = rec_all.shape[0]
    n_experts = w_g_b.shape[0]
    n_blk = -(-(t_all * TOP_K) // bm) + n_experts
    cnt = counts[0, :n_experts].astype(_I32)
    padded = (cnt + bm - 1) // bm * bm
    pends = jnp.cumsum(padded).astype(_I32)
    pstarts = pends - padded
    idx = rec_all[:, REC_IDX:REC_IDX + TOP_K].astype(_I32)
    rank = rec_all[:, REC_RANK:REC_RANK + TOP_K].astype(_I32)
    experts = jnp.arange(n_experts, dtype=_I32)
    dest = rank + jnp.sum(jnp.where(idx[:, :, None] == experts, pstarts, 0), axis=-1)
    blk_row0 = jnp.arange(n_blk, dtype=_I32) * bm
    block_e = jnp.minimum(jnp.sum((pends[None, :] <= blk_row0[:, None]).astype(_I32), axis=1), n_experts - 1)
    n_used = (pends[-1:] // bm).astype(_I32)
    row_end = jnp.sum(jnp.where(block_e[:, None] == experts, pstarts + cnt, 0), axis=-1)
    valid = jnp.clip(row_end - blk_row0, 0, bm).astype(_I32)
    xs = _dispatch(h2_p, h2_s, dest, pends, padded, n_used, n_blk * bm, tt, bm)
    return _expert_ffn(xs, block_e, n_used, valid, w_g_b, w_u_b, b_gu, w_down_b, b_down, bm, chunk), dest


def kernel(x_prompt, x_sample, c_prompt, c_sample, cache_k, cache_v, state_conv, w_ada, b_ada, g_mix, w_in, attn_sink, conv_w, conv_b, conv_ln_g, conv_ln_b, w_out, g_ffn, w_router, b_router, w_gu, b_gu, w_down, b_down, g_final):
    bp, sp, d = x_prompt.shape
    bs, ss, _ = x_sample.shape
    depth = w_ada.shape[0]
    n_experts = w_router.shape[-1]
    tp, ts = bp * sp, bs * ss
    t_all = tp + ts

    tm = _tile(sp, 256)
    to = _tile(sp, 512)
    tq = _tile(sp, 256)
    tc = _tile(sp, 256)
    tt = _tile(ts, 128)
    bm = 512 if tp >= 8192 else 64
    chunk = 512
    assert tp % ts == 0 and tp % tt == 0 and sp % CHUNK == 0 and tq % WINDOW == 0 and tc % CONV_HALO == 0

    xp = x_prompt.reshape(tp, d)
    xsm = x_sample.reshape(ts, d)
    tabs_p = _rope_tables(np.arange(sp))
    tabs_s = _rope_tables(np.tile(PAST_LEN + np.arange(ss), bs))

    new_kp, new_vp, new_cp, new_ks, new_vs, new_cs = [], [], [], [], [], []
    for l in range(depth):
        w_in_b = w_in[l].astype(_BF16)
        wo_a = w_out[l, :ATTN_WIDTH].astype(_BF16)
        wo_c = w_out[l, ATTN_WIDTH:].astype(_BF16)
        wr_pad = jnp.pad(w_router[l], ((0, 0), (0, LANES - n_experts)))
        wr_hi = wr_pad.astype(_BF16)
        wr_cat = jnp.concatenate([wr_hi, (wr_pad - wr_hi.astype(_F32)).astype(_BF16)], axis=1)
        br_pad = jnp.pad(b_router[l], (0, LANES - n_experts)).reshape(1, LANES)

        mods = _modulation(jnp.concatenate([c_prompt, c_sample], axis=0), w_ada[l], b_ada[l])
        mods_p = mods[:, :bp].reshape(6, bp, 1, d)
        mods_s = jnp.repeat(mods[:, bp:], ss, axis=1).reshape(6, 1, ts, d)

        d_ff = w_down.shape[2]
        w_gu2d = w_gu[l].reshape(n_experts * d, 2 * d_ff)

        qp, kp, vp, up, w_u_b = _in_proj(xp, mods_p, g_mix[l], w_in_b, tabs_p, tm, (w_gu2d, d_ff, 1))
        qs, ks, vs, us = _in_proj(xsm, mods_s, g_mix[l], w_in_b, tabs_s, ts)

        att_p, w_g_b = _band_attention(qp, kp, vp, attn_sink[l], bp, sp, tq, (w_gu2d, d_ff, 0))
        w_g_b = w_g_b.reshape(n_experts, d, d_ff)
        w_u_b = w_u_b.reshape(n_experts, d, d_ff)
        win = cache_k.shape[2]
        att_s = _cached_attention(qs, ks, vs, cache_k[l].reshape(bs, win, KV_COLS), cache_v[l].reshape(bs, win, KV_COLS),
                                  attn_sink[l], bs, ss)

        cv_p, w_down_b = _conv_module(up, None, conv_w[l], conv_b[l], conv_ln_g[l], conv_ln_b[l], bp, sp, tc,
                                      (w_down[l].reshape(n_experts * d_ff, d), d, 0))
        w_down_b = w_down_b.reshape(n_experts, d_ff, d)
        hist = jnp.pad(state_conv[l], ((0, 0), (CONV_HALO - (CONV_WIDTH - 1), 0), (0, 0))).reshape(bs * CONV_HALO, CONV_CH)
        cv_s = _conv_module(us, hist, conv_w[l], conv_b[l], conv_ln_g[l], conv_ln_b[l], bs, ss, ss)

        zero_counts = jnp.zeros((1, LANES), _F32)
        x1p, h2p, rec_p, counts = _out_proj(xp, att_p, cv_p, mods_p, g_ffn[l], wo_a, wo_c, wr_cat, br_pad,
                                            zero_counts, to, n_experts)
        x1s, h2s, rec_s, counts = _out_proj(xsm, att_s, cv_s, mods_s, g_ffn[l], wo_a, wo_c, wr_cat, br_pad,
                                            counts, ts, n_experts)

        out_sorted, dest = _moe(h2p, h2s, jnp.concatenate([rec_p, rec_s], axis=0), counts, w_g_b, w_u_b, b_gu[l],
                                w_down_b, b_down[l], tt, bm, chunk)

        last = l == depth - 1
        xp = _combine(x1p, out_sorted, dest[:tp], rec_p, mods_p, g_final, tm, last)
        xsm = _combine(x1s, out_sorted, dest[tp:], rec_s, mods_s, g_final, ts, last)

        keep = min(WINDOW, sp)
        new_kp.append(kp.reshape(bp, sp, KV_COLS)[:, sp - keep:].reshape(bp, keep, N_KV_HEADS, HEAD_DIM))
        new_vp.append(vp.reshape(bp, sp, KV_COLS)[:, sp - keep:].reshape(bp, keep, N_KV_HEADS, HEAD_DIM))
        new_cp.append(up.reshape(bp, sp, CONV_CH)[:, -(CONV_WIDTH - 1):])
        new_ks.append(ks.reshape(bs, ss, N_KV_HEADS, HEAD_DIM))
        new_vs.append(vs.reshape(bs, ss, N_KV_HEADS, HEAD_DIM))
        new_cs.append(jnp.concatenate([state_conv[l], us.reshape(bs, ss, CONV_CH)], axis=1)[:, -(CONV_WIDTH - 1):])

    return (xp.reshape(bp, sp, d), xsm.reshape(bs, ss, d), jnp.stack(new_kp), jnp.stack(new_vp), jnp.stack(new_cp),
            jnp.stack(new_ks), jnp.stack(new_vs), jnp.stack(new_cs))
```

```python
import functools

import jax
import jax.numpy as jnp
import numpy as np
from jax import lax
from jax.experimental import pallas as pl
from jax.experimental.pallas import tpu as pltpu

D_MODEL = 2048
CHUNK = 64
HEAD_DIM = 64
ATTN_WIDTH = D_MODEL // 2
CONV_CH = D_MODEL - ATTN_WIDTH
N_HEADS = ATTN_WIDTH // HEAD_DIM
N_KV_HEADS = N_HEADS // 4
GQA_GROUP = N_HEADS // N_KV_HEADS
ROT_DIM = HEAD_DIM // 4
ROPE_THETA = 500000.0
WINDOW = 128
WINDOW_CHUNKS = WINDOW // CHUNK
CONV_WIDTH = 31
TOP_K = 4
SWIGLU_LIMIT = 7.0
SWIGLU_ALPHA = 1.702
EPS = 1e-5
NEG_INF = -1e30
PAST_LEN = 2048
Q_COLS = N_HEADS * HEAD_DIM
KV_COLS = N_KV_HEADS * HEAD_DIM
IN_COLS = Q_COLS + 2 * KV_COLS + 2 * CONV_CH
ATTN_SCALE = HEAD_DIM ** -0.5
assert ATTN_SCALE == 0.125

LANES = 128
SUBLANES = 8
CONV_HALO = 32
MIB = 1024 * 1024

_BF16 = jnp.bfloat16
_F32 = jnp.float32
_I32 = jnp.int32
_U32 = jnp.uint32


def _params(semantics, vmem_mib):
    return pltpu.CompilerParams(dimension_semantics=semantics, vmem_limit_bytes=vmem_mib * MIB)


def _dot(a, b):
    return jnp.dot(a, b, preferred_element_type=_F32)


def _pack_pair(lo, hi):
    lo = lax.bitcast_convert_type(lo.astype(_BF16).astype(_F32), _U32)
    hi = lax.bitcast_convert_type(hi.astype(_BF16).astype(_F32), _U32)
    return (lo >> 16) | (hi & jnp.uint32(0xFFFF0000))


def _pack_halves(x):
    w = x.shape[1] // 2
    return _pack_pair(x[:, :w], x[:, w:])


def _unpack_halves(p):
    return (lax.bitcast_convert_type(p << 16, _F32), lax.bitcast_convert_type(p & jnp.uint32(0xFFFF0000), _F32))


def _rmsnorm_mod(x, g, scale, shift):
    ms = jnp.mean(x * x, axis=-1, keepdims=True)
    return (x * lax.rsqrt(ms + EPS) * g) * (1.0 + scale) + shift


def _ada_body(c_ref, w_ref, b_ref, o_ref):
    c = c_ref[...]
    s = (c * jax.nn.sigmoid(c)).astype(_BF16)
    o_ref[...] = _dot(s, w_ref[...].astype(_BF16)) + b_ref[...]


def _modulation(c, w_ada, b_ada):
    n, d = c.shape
    cols = w_ada.shape[1]
    tn = 1024
    per = d // tn
    return pl.pallas_call(
        _ada_body,
        grid=(cols // tn,),
        in_specs=[
            pl.BlockSpec((n, d), lambda j: (0, 0)),
            pl.BlockSpec((d, tn), lambda j: (0, j)),
            pl.BlockSpec((1, tn), lambda j: (0, j)),
        ],
        out_specs=pl.BlockSpec((None, n, tn), lambda j: (j // per, 0, j % per)),
        out_shape=jax.ShapeDtypeStruct((cols // d, n, d), _F32),
        compiler_params=_params(("arbitrary",), 40),
        name="modulation",
    )(c, w_ada, b_ada.reshape(1, cols))


def _rope_tables(pos):
    half = ROT_DIM // 2
    inv_freq = np.power(np.float32(ROPE_THETA), -np.arange(0, ROT_DIM, 2, dtype=np.float32) / np.float32(ROT_DIM))
    ang = (pos.astype(np.float32)[:, None] * inv_freq[None, :].astype(np.float32)).astype(np.float64)
    cos, sin = np.cos(ang), np.sin(ang)
    d = np.arange(LANES) % HEAD_DIM
    cos_l = np.where(d < ROT_DIM, cos[:, d % half], 1.0)
    sa = np.where(d < half, -sin[:, d % half], 0.0)
    sb = np.where((d >= half) & (d < ROT_DIM), sin[:, d % half], 0.0)
    return tuple(jnp.asarray(t.astype(np.float32)) for t in (cos_l, sa, sb))


def _rotate(z, cos, sa, sb):
    half = ROT_DIM // 2
    parts = []
    for j in range(z.shape[1] // LANES):
        zj = z[:, j * LANES:(j + 1) * LANES]
        parts.append(zj * cos + pltpu.roll(zj, LANES - half, 1) * sa + pltpu.roll(zj, half, 1) * sb)
    return parts[0] if len(parts) == 1 else jnp.concatenate(parts, axis=1)


def _in_proj_body(x_ref, mod_ref, g_ref, w_ref, cos_ref, sa_ref, sb_ref, *rest):
    q_ref, k_ref, v_ref, u_ref = rest[-4:] if len(rest) == 4 else rest[1:5]
    if len(rest) == 6:
        rest[5][...] = rest[0][...].astype(_BF16)
    h = _rmsnorm_mod(x_ref[...], g_ref[...], mod_ref[1, 0], mod_ref[0, 0]).astype(_BF16)
    cos, sa, sb = cos_ref[...], sa_ref[...], sb_ref[...]
    cw = 512
    for c in range(Q_COLS // cw):
        z = _dot(h, w_ref[:, c * cw:(c + 1) * cw])
        q_ref[:, c * cw:(c + 1) * cw] = (_rotate(z, cos, sa, sb) * ATTN_SCALE).astype(_BF16)
    z = _dot(h, w_ref[:, Q_COLS:Q_COLS + 2 * KV_COLS])
    k_ref[...] = _rotate(z[:, :KV_COLS], cos, sa, sb)
    v_ref[...] = z[:, KV_COLS:]
    o = Q_COLS + 2 * KV_COLS
    for c in range(CONV_CH // cw):
        zv = _dot(h, w_ref[:, o + c * cw:o + (c + 1) * cw])
        zg = _dot(h, w_ref[:, o + CONV_CH + c * cw:o + CONV_CH + (c + 1) * cw])
        u_ref[:, c * cw:(c + 1) * cw] = zv * jax.nn.sigmoid(zg)


def _mod_spec(mods4, tiles_per_group):
    _, _, r, d = mods4.shape
    return pl.BlockSpec((6, 1, r, d), lambda i: (0, i // tiles_per_group, 0, 0))


def _in_proj(x2d, mods4, g_mix, w_in_b, tabs, tm, rider=None):
    t, d = x2d.shape
    n_tiles = t // tm
    tiles_per_group = n_tiles // mods4.shape[1]
    pos_tiles = tabs[0].shape[0] // tm
    tab_spec = pl.BlockSpec((tm, LANES), lambda i: (i % pos_tiles, 0))
    row = lambda w: pl.BlockSpec((tm, w), lambda i: (i, 0))
    in_specs = [
        row(d),
        _mod_spec(mods4, tiles_per_group),
        pl.BlockSpec((1, d), lambda i: (0, 0)),
        pl.BlockSpec((d, IN_COLS), lambda i: (0, 0), pipeline_mode=pl.Buffered(1)),
        tab_spec, tab_spec, tab_spec,
    ]
    out_specs = [row(Q_COLS), row(KV_COLS), row(KV_COLS), row(CONV_CH)]
    out_shape = [
        jax.ShapeDtypeStruct((t, Q_COLS), _BF16),
        jax.ShapeDtypeStruct((t, KV_COLS), _F32),
        jax.ShapeDtypeStruct((t, KV_COLS), _F32),
        jax.ShapeDtypeStruct((t, CONV_CH), _F32),
    ]
    args = [x2d, mods4, g_mix.reshape(1, d), w_in_b, *tabs]
    ride = None if rider is None else _rider_specs(rider, n_tiles, lambda i: i)
    if ride is not None:
        in_specs.append(ride[0])
        out_specs.append(ride[1])
        out_shape.append(ride[2])
        args.append(rider[0])
    res = pl.pallas_call(
        _in_proj_body,
        grid=(n_tiles,),
        in_specs=in_specs,
        out_specs=out_specs,
        out_shape=out_shape,
        compiler_params=_params(("arbitrary",), 56),
        name="in_proj",
    )(*args)
    if rider is not None and ride is None:
        res = list(res) + [_rider_fallback(rider)]
    return res


def _attn_group(qg, kk, vv, sink_col, valid):
    s = lax.dot_general(qg, kk, (((1,), (1,)), ((), ())), preferred_element_type=_F32)
    if valid is not None:
        s = jnp.where(valid, s, NEG_INF)
    m = jnp.maximum(jnp.max(s, axis=-1, keepdims=True), sink_col)
    p = jnp.exp(s - m)
    den = jnp.sum(p, axis=-1, keepdims=True) + jnp.exp(sink_col - m)
    return _dot(p.astype(_BF16), vv) / den


def _sink_column(sink_ref, kh, rows_per_head):
    row = lax.broadcasted_iota(_I32, (GQA_GROUP * rows_per_head, 1), 0)
    col = jnp.full(row.shape, sink_ref[kh * GQA_GROUP + GQA_GROUP - 1], _F32)
    for g in range(GQA_GROUP - 2, -1, -1):
        col = jnp.where(row < (g + 1) * rows_per_head, sink_ref[kh * GQA_GROUP + g], col)
    return col


def _store_heads(o_ref, r0, rows, kh, o):
    for pair in range(GQA_GROUP // 2):
        both = jnp.concatenate([o[(2 * pair) * rows:(2 * pair + 1) * rows],
                                o[(2 * pair + 1) * rows:(2 * pair + 2) * rows]], axis=1)
        c0 = (kh * GQA_GROUP + 2 * pair) * HEAD_DIM
        o_ref[r0:r0 + rows, c0:c0 + 2 * HEAD_DIM] = both.astype(o_ref.dtype)


def _stack_heads(q, r0, rows, kh):
    return jnp.concatenate(
        [q[r0:r0 + rows, (kh * GQA_GROUP + g) * HEAD_DIM:(kh * GQA_GROUP + g + 1) * HEAD_DIM] for g in range(GQA_GROUP)],
        axis=0)


def _rider_specs(rider, n_steps, step_index):
    arr, width, colblk = rider
    rows = arr.shape[0]
    per = rows // n_steps
    if rows % n_steps or per % 16 or per * width * 4 > 8 * MIB:
        return None
    return (pl.BlockSpec((per, width), lambda *ids: (step_index(*ids), colblk)),
            pl.BlockSpec((per, width), lambda *ids: (step_index(*ids), 0)),
            jax.ShapeDtypeStruct((rows, width), _BF16))


def _rider_fallback(rider):
    arr, width, colblk = rider
    return arr[:, colblk * width:(colblk + 1) * width].astype(_BF16)


def _band_attn_body(*refs, tq):
    for stage in _band_attn_stages(*refs, tq=tq):
        stage()


def _band_attn_stages(sink_ref, q_ref, kc_ref, kh_ref, vc_ref, vh_ref, *rest, tq):
    o_ref = rest[-1] if len(rest) == 1 else rest[1]
    if len(rest) == 3:
        rest[2][...] = rest[0][...].astype(_BF16)
    j = pl.program_id(1)
    q = q_ref[...]
    kall = jnp.concatenate([kh_ref[...], kc_ref[...]], axis=0)
    vall = jnp.concatenate([vh_ref[...], vc_ref[...]], axis=0)
    band = (WINDOW_CHUNKS + 1) * CHUNK
    key_chunk = lax.broadcasted_iota(_I32, (1, band), 1) // CHUNK

    def head(kh):
        sink_col = _sink_column(sink_ref, kh, CHUNK)
        kk = kall[:, kh * HEAD_DIM:(kh + 1) * HEAD_DIM].astype(_BF16)
        vv = vall[:, kh * HEAD_DIM:(kh + 1) * HEAD_DIM].astype(_BF16)
        for ci in range(tq // CHUNK):
            qg = _stack_heads(q, ci * CHUNK, CHUNK, kh)
            valid = None if ci >= WINDOW_CHUNKS else (j * (tq // CHUNK) + ci - WINDOW_CHUNKS + key_chunk) >= 0
            o = _attn_group(qg, kk[ci * CHUNK:ci * CHUNK + band], vv[ci * CHUNK:ci * CHUNK + band], sink_col, valid)
            _store_heads(o_ref, ci * CHUNK, CHUNK, kh, o)

    return [functools.partial(head, kh) for kh in range(N_KV_HEADS)]


def _band_attention(q, k, v, sink, batch, seq, tq, rider):
    nq = seq // tq
    r = tq // WINDOW
    cur = lambda w: pl.BlockSpec((tq, w), lambda b, j: (b * nq + j, 0))
    halo = pl.BlockSpec((WINDOW, KV_COLS), lambda b, j: (b * nq * r + jnp.maximum(j * r - 1, 0), 0))
    in_specs = [pl.BlockSpec(memory_space=pltpu.SMEM), cur(Q_COLS), cur(KV_COLS), halo, cur(KV_COLS), halo]
    out_specs = [cur(Q_COLS)]
    out_shape = [jax.ShapeDtypeStruct((batch * seq, Q_COLS), _BF16)]
    args = [sink, q, k, k, v, v]
    ride = _rider_specs(rider, batch * nq, lambda b, j: b * nq + j)
    if ride is not None:
        in_specs.append(ride[0])
        out_specs.append(ride[1])
        out_shape.append(ride[2])
        args.append(rider[0])
    res = pl.pallas_call(
        functools.partial(_band_attn_body, tq=tq),
        grid=(batch, nq),
        in_specs=in_specs,
        out_specs=out_specs,
        out_shape=out_shape,
        compiler_params=_params(("arbitrary", "arbitrary"), 48),
        name="band_attention",
    )(*args)
    return (res[0], res[1]) if ride is not None else (res[0], _rider_fallback(rider))


def _cached_attn_body(sink_ref, q_ref, kn_ref, kc_ref, vn_ref, vc_ref, o_ref, *, rows):
    q = q_ref[...]
    kall = jnp.concatenate([kc_ref[0], kn_ref[...]], axis=0)
    vall = jnp.concatenate([vc_ref[0], vn_ref[...]], axis=0)
    for kh in range(N_KV_HEADS):
        o = _attn_group(_stack_heads(q, 0, rows, kh), kall[:, kh * HEAD_DIM:(kh + 1) * HEAD_DIM].astype(_BF16),
                        vall[:, kh * HEAD_DIM:(kh + 1) * HEAD_DIM].astype(_BF16), _sink_column(sink_ref, kh, rows), None)
        _store_heads(o_ref, 0, rows, kh, o)


def _cached_attention(q, k, v, cache_k, cache_v, sink, batch, rows):
    win = cache_k.shape[1]
    new = lambda w: pl.BlockSpec((rows, w), lambda b: (b, 0))
    cache = pl.BlockSpec((1, win, KV_COLS), lambda b: (b, 0, 0))
    return pl.pallas_call(
        functools.partial(_cached_attn_body, rows=rows),
        grid=(batch,),
        in_specs=[pl.BlockSpec(memory_space=pltpu.SMEM), new(Q_COLS), new(KV_COLS), cache, new(KV_COLS), cache],
        out_specs=new(Q_COLS),
        out_shape=jax.ShapeDtypeStruct((batch * rows, Q_COLS), _BF16),
        compiler_params=_params(("arbitrary",), 40),
        name="cached_attention",
    )(sink, q, k, cache_k, v, cache_v)


def _conv_body(*refs, tt, zero_first_halo):
    for stage in _conv_stages(*refs, tt=tt, zero_first_halo=zero_first_halo):
        stage()


def _conv_stages(uc_ref, uh_ref, w_ref, b_ref, g_ref, beta_ref, *rest, tt, zero_first_halo):
    win_ref, y_ref = rest[-2:]
    o_ref = rest[0] if len(rest) == 3 else rest[1]
    if len(rest) == 5:
        rest[2][...] = rest[0][...].astype(_BF16)
    halo = uh_ref[...]
    if zero_first_halo:
        halo = jnp.where(pl.program_id(1) == 0, 0.0, halo)
    win_ref[0:CONV_HALO, :] = halo
    win_ref[CONV_HALO:, :] = uc_ref[...]
    lead = CONV_HALO - (CONV_WIDTH - 1)
    by_shift = {}
    for tap in range(CONV_WIDTH):
        by_shift.setdefault((lead + tap) % SUBLANES, []).append(tap)
    rb = min(tt, 128)

    def slab(s):
        cs = slice(s * LANES, (s + 1) * LANES)
        for r0 in range(0, tt, rb):
            y = b_ref[:, cs]
            for sh, taps in sorted(by_shift.items()):
                ext = SUBLANES if sh else 0
                q = None
                for tap in taps:
                    base = r0 + lead + tap - sh
                    term = w_ref[tap:tap + 1, cs] * win_ref[base:base + rb + ext, cs]
                    q = term if q is None else q + term
                y = y + (q[sh:sh + rb] if sh else q)
            y_ref[r0:r0 + rb, cs] = y

    def finish():
        y = y_ref[...]
        mu = jnp.mean(y, axis=-1, keepdims=True)
        var = jnp.mean(jnp.square(y - mu), axis=-1, keepdims=True)
        z = (y - mu) * lax.rsqrt(var + EPS) * g_ref[...] + beta_ref[...]
        o_ref[...] = (z * jax.nn.sigmoid(z)).astype(o_ref.dtype)

    return [functools.partial(slab, s) for s in range(CONV_CH // LANES)] + [finish]


def _conv_module(u, hist, conv_w, conv_b, ln_g, ln_b, batch, seq, tt, rider=None):
    nt = seq // tt
    r = tt // CONV_HALO
    cur = pl.BlockSpec((tt, CONV_CH), lambda b, j: (b * nt + j, 0))
    if hist is None:
        hist_arr = u
        halo = pl.BlockSpec((CONV_HALO, CONV_CH), lambda b, j: (b * nt * r + jnp.maximum(j * r - 1, 0), 0))
    else:
        assert nt == 1
        hist_arr = hist
        halo = pl.BlockSpec((CONV_HALO, CONV_CH), lambda b, j: (b, 0))
    vec = pl.BlockSpec((1, CONV_CH), lambda b, j: (0, 0))
    wpad = jnp.pad(conv_w.reshape(CONV_WIDTH, CONV_CH), ((0, 1), (0, 0)))
    in_specs = [cur, halo, pl.BlockSpec((CONV_WIDTH + 1, CONV_CH), lambda b, j: (0, 0)), vec, vec, vec]
    out_specs = [cur]
    out_shape = [jax.ShapeDtypeStruct((batch * seq, CONV_CH), _BF16)]
    args = [u, hist_arr, wpad, conv_b.reshape(1, -1), ln_g.reshape(1, -1), ln_b.reshape(1, -1)]
    ride = None if rider is None else _rider_specs(rider, batch * nt, lambda b, j: b * nt + j)
    if ride is not None:
        in_specs.append(ride[0])
        out_specs.append(ride[1])
        out_shape.append(ride[2])
        args.append(rider[0])
    res = pl.pallas_call(
        functools.partial(_conv_body, tt=tt, zero_first_halo=hist is None),
        grid=(batch, nt),
        in_specs=in_specs,
        out_specs=out_specs,
        out_shape=out_shape,
        scratch_shapes=[pltpu.VMEM((tt + CONV_HALO, CONV_CH), _F32), pltpu.VMEM((tt, CONV_CH), _F32)],
        compiler_params=_params(("arbitrary", "arbitrary"), 40),
        name="conv_module",
    )(*args)
    if rider is None:
        return res[0]
    return (res[0], res[1]) if ride is not None else (res[0], _rider_fallback(rider))


def _mixers_body(*refs, t, n_attn_in, n_conv_in, n_attn_out, n_conv_out):
    a_in = refs[:n_attn_in]
    c_in = refs[n_attn_in:n_attn_in + n_conv_in]
    outs = refs[n_attn_in + n_conv_in:]
    a_out = outs[:n_attn_out]
    c_out = outs[n_attn_out:n_attn_out + n_conv_out]
    scratch = outs[n_attn_out + n_conv_out:]
    attn = _band_attn_stages(*a_in, *a_out, tq=t)
    conv = _conv_stages(*c_in, *c_out, *scratch, tt=t, zero_first_halo=True)
    per = -(-len(conv) // len(attn))
    for n, stage in enumerate(attn):
        stage()
        for c in conv[n * per:(n + 1) * per]:
            c()


def _mixers(q, k, v, sink, u, conv_w, conv_b, ln_g, ln_b, batch, seq, t, rider_a, rider_c):
    nt = seq // t
    ra, rc = t // WINDOW, t // CONV_HALO
    step = lambda b, j: b * nt + j
    cur = lambda w: pl.BlockSpec((t, w), lambda b, j: (b * nt + j, 0))
    kv_halo = pl.BlockSpec((WINDOW, KV_COLS), lambda b, j: (b * nt * ra + jnp.maximum(j * ra - 1, 0), 0))
    u_halo = pl.BlockSpec((CONV_HALO, CONV_CH), lambda b, j: (b * nt * rc + jnp.maximum(j * rc - 1, 0), 0))
    vec = pl.BlockSpec((1, CONV_CH), lambda b, j: (0, 0))
    wpad = jnp.pad(conv_w.reshape(CONV_WIDTH, CONV_CH), ((0, 1), (0, 0)))
    a_specs = [pl.BlockSpec(memory_space=pltpu.SMEM), cur(Q_COLS), cur(KV_COLS), kv_halo, cur(KV_COLS), kv_halo]
    a_args = [sink, q, k, k, v, v]
    c_specs = [cur(CONV_CH), u_halo, pl.BlockSpec((CONV_WIDTH + 1, CONV_CH), lambda b, j: (0, 0)), vec, vec, vec]
    c_args = [u, u, wpad, conv_b.reshape(1, -1), ln_g.reshape(1, -1), ln_b.reshape(1, -1)]
    a_out_specs, a_out_shape = [cur(Q_COLS)], [jax.ShapeDtypeStruct((batch * seq, Q_COLS), _BF16)]
    c_out_specs, c_out_shape = [cur(CONV_CH)], [jax.ShapeDtypeStruct((batch * seq, CONV_CH), _BF16)]
    ride_a = _rider_specs(rider_a, batch * nt, step)
    ride_c = _rider_specs(rider_c, batch * nt, step)
    if ride_a is not None:
        a_specs.append(ride_a[0]); a_args.append(rider_a[0]); a_out_specs.append(ride_a[1]); a_out_shape.append(ride_a[2])
    if ride_c is not None:
        c_specs.append(ride_c[0]); c_args.append(rider_c[0]); c_out_specs.append(ride_c[1]); c_out_shape.append(ride_c[2])
    res = pl.pallas_call(
        functools.partial(_mixers_body, t=t, n_attn_in=len(a_specs), n_conv_in=len(c_specs),
                          n_attn_out=len(a_out_specs), n_conv_out=len(c_out_specs)),
        grid=(batch, nt),
        in_specs=a_specs + c_specs,
        out_specs=a_out_specs + c_out_specs,
        out_shape=a_out_shape + c_out_shape,
        scratch_shapes=[pltpu.VMEM((t + CONV_HALO, CONV_CH), _F32), pltpu.VMEM((t, CONV_CH), _F32)],
        compiler_params=_params(("arbitrary", "arbitrary"), 56),
        name="prompt_mixers",
    )(*a_args, *c_args)
    na = len(a_out_specs)
    att = res[0]
    wa = res[1] if ride_a is not None else _rider_fallback(rider_a)
    cv = res[na]
    wc = res[na + 1] if ride_c is not None else _rider_fallback(rider_c)
    return att, cv, wa, wc


REC_IDX, REC_GATE, REC_RANK = 0, TOP_K, 2 * TOP_K


def _out_proj_body(x_ref, a_ref, c_ref, mod_ref, g_ref, wa_ref, wc_ref, wrc_ref, br_ref, cnt_ref,
                   x1_ref, h2_ref, rec_ref, cnt_out_ref, run_ref, *, n_experts):
    tm = x_ref.shape[0]

    @pl.when(pl.program_id(0) == 0)
    def _():
        run_ref[...] = cnt_ref[...]

    o = _dot(a_ref[...], wa_ref[...]) + _dot(c_ref[...], wc_ref[...])
    x1 = x_ref[...] + mod_ref[2, 0] * o
    x1_ref[...] = x1
    h2 = _rmsnorm_mod(x1, g_ref[...], mod_ref[4, 0], mod_ref[3, 0])
    h2_ref[...] = _pack_halves(h2)

    h2_hi = h2.astype(_BF16)
    h2_lo = (h2 - h2_hi.astype(_F32)).astype(_BF16)
    both = _dot(h2_hi, wrc_ref[...])
    logits = both[:, :LANES] + (_dot(h2_lo, wrc_ref[:, :LANES]) + both[:, LANES:]) + br_ref[...]
    lane = lax.broadcasted_iota(_I32, (tm, LANES), 1)
    work = jnp.where(lane < n_experts, logits, -jnp.inf)
    vals, hots = [], []
    rec = jnp.zeros((tm, LANES), _F32)
    for k in range(TOP_K):
        m = jnp.max(work, axis=-1, keepdims=True)
        idx = jnp.min(jnp.where(work == m, lane, LANES), axis=-1, keepdims=True)
        hot = lane == idx
        work = jnp.where(hot, -jnp.inf, work)
        vals.append(m)
        hots.append(hot)
        rec = jnp.where(lane == REC_IDX + k, idx.astype(_F32), rec)
    exps = [jnp.exp(v - vals[0]) for v in vals]
    den = exps[0]
    for e in exps[1:]:
        den = den + e
    for k in range(TOP_K):
        rec = jnp.where(lane == REC_GATE + k, exps[k] / den, rec)

    chosen = jnp.zeros((tm, LANES), _F32)
    for hot in hots:
        chosen = jnp.where(hot, 1.0, chosen)
    r_i = lax.broadcasted_iota(_I32, (tm, tm), 0)
    c_i = lax.broadcasted_iota(_I32, (tm, tm), 1)
    before = _dot(jnp.where(c_i < r_i, 1.0, 0.0).astype(_BF16), chosen.astype(_BF16)) + run_ref[...]
    for k in range(TOP_K):
        rank = jnp.sum(jnp.where(hots[k], before, 0.0), axis=-1, keepdims=True)
        rec = jnp.where(lane == REC_RANK + k, rank, rec)
    rec_ref[...] = rec
    run_ref[...] = run_ref[...] + jnp.sum(chosen, axis=0, keepdims=True)
    cnt_out_ref[...] = run_ref[...]


def _out_proj(x2d, attn, conv, mods4, g_ffn, wo_a, wo_c, wr_cat, br_pad, counts, tm, n_experts):
    t, d = x2d.shape
    n_tiles = t // tm
    tiles_per_group = n_tiles // mods4.shape[1]
    row = lambda w: pl.BlockSpec((tm, w), lambda i: (i, 0))
    const = lambda a, b: pl.BlockSpec((a, b), lambda i: (0, 0))
    return pl.pallas_call(
        functools.partial(_out_proj_body, n_experts=n_experts),
        grid=(n_tiles,),
        in_specs=[row(d), row(ATTN_WIDTH), row(CONV_CH), _mod_spec(mods4, tiles_per_group), const(1, d),
                  const(ATTN_WIDTH, d), const(CONV_CH, d), const(d, 2 * LANES), const(1, LANES), const(1, LANES)],
        out_specs=[row(d), row(d // 2), row(LANES), const(1, LANES)],
        out_shape=[
            jax.ShapeDtypeStruct((t, d), _F32),
            jax.ShapeDtypeStruct((t, d // 2), _U32),
            jax.ShapeDtypeStruct((t, LANES), _F32),
            jax.ShapeDtypeStruct((1, LANES), _F32),
        ],
        scratch_shapes=[pltpu.VMEM((1, LANES), _F32)],
        compiler_params=_params(("arbitrary",), 56),
        name="out_proj_router",
    )(x2d, attn, conv, mods4, g_ffn.reshape(1, d), wo_a, wo_c, wr_cat, br_pad, counts)


def _row_copy(src, dst, sem):
    return pltpu.make_async_copy(src, dst, sem)


def _rows(ref, row0, n):
    return ref.at[pl.ds(row0, n), 0]


def _dispatch_body(pends_ref, padded_ref, nu_ref, dest_ref, hp_ref, hs_ref, xs_ref, stage_ref, zeros_ref, sems, zsem, *,
                   tt, bm, n_experts, prompt_steps):
    i = pl.program_id(0)
    n = pl.num_programs(0)
    n_blk = xs_ref.shape[0] // bm
    slot = i % 2

    def zero_block(row0):
        return _row_copy(zeros_ref, _rows(xs_ref, pl.multiple_of(row0, bm), bm), zsem)

    @pl.when(i == 0)
    def _():
        zeros_ref[...] = jnp.zeros(zeros_ref.shape, zeros_ref.dtype)
        for e in range(n_experts):
            @pl.when(padded_ref[e] > 0)
            def _():
                zero_block(pends_ref[e] - bm).start()

        def tail_start(b, carry):
            zero_block(b * bm).start()
            return carry

        def tail_wait(b, carry):
            zero_block(b * bm).wait()
            return carry

        lax.fori_loop(nu_ref[0], n_blk, tail_start, 0)
        for e in range(n_experts):
            @pl.when(padded_ref[e] > 0)
            def _():
                zero_block(pends_ref[e] - bm).wait()
        lax.fori_loop(nu_ref[0], n_blk, tail_wait, 0)

    @pl.when(i < prompt_steps)
    def _():
        stage_ref[slot] = hp_ref[...]

    @pl.when(i >= prompt_steps)
    def _():
        stage_ref[slot] = hs_ref[...]

    for r in range(tt):
        src = stage_ref.at[slot, pl.ds(r, 1)]
        for k in range(TOP_K):
            _row_copy(src, xs_ref.at[dest_ref[0, 0, r * TOP_K + k]], sems.at[slot]).start(priority=k % 2)

    def drain(s):
        for _ in range(TOP_K):
            _row_copy(stage_ref.at[s], _rows(xs_ref, 0, tt), sems.at[s]).wait()

    @pl.when(i > 0)
    def _():
        drain(1 - slot)

    @pl.when(i == n - 1)
    def _():
        drain(slot)


def _dispatch(h2_p, h2_s, dest, pends, padded, n_used, rows, tt, bm):
    w = h2_p.shape[1]
    tp, ts = h2_p.shape[0], h2_s.shape[0]
    n_experts = pends.shape[0]
    prompt_steps = tp // tt
    grid_spec = pltpu.PrefetchScalarGridSpec(
        num_scalar_prefetch=3,
        grid=((tp + ts) // tt,),
        in_specs=[
            pl.BlockSpec((1, 1, tt * TOP_K), lambda i, *_: (i, 0, 0), memory_space=pltpu.SMEM),
            pl.BlockSpec((tt, w), lambda i, *_: (jnp.minimum(i, prompt_steps - 1), 0)),
            pl.BlockSpec((tt, w), lambda i, *_: (jnp.maximum(i - prompt_steps, 0), 0)),
        ],
        out_specs=pl.BlockSpec(memory_space=pl.ANY),
        scratch_shapes=[pltpu.VMEM((2, tt, w), _U32), pltpu.VMEM((bm, w), _U32),
                        pltpu.SemaphoreType.DMA((2,)), pltpu.SemaphoreType.DMA(())],
    )
    return pl.pallas_call(
        functools.partial(_dispatch_body, tt=tt, bm=bm, n_experts=n_experts, prompt_steps=prompt_steps),
        grid_spec=grid_spec,
        out_shape=jax.ShapeDtypeStruct((rows, 1, w), _U32),
        compiler_params=_params(("arbitrary",), 24),
        name="moe_dispatch",
    )(pends, padded, n_used, dest.reshape((tp + ts) // tt, 1, tt * TOP_K), h2_p, h2_s)


def _ffn_body(be_ref, nu_ref, valid_ref, bgu_ref, bd_ref, xs_hbm, wg_hbm, wu_hbm, wd_hbm, o_hbm, wg_ref, wu_ref, wd_ref,
              xbuf_ref, obuf_ref, xb_ref, a_ref, wsems, xsems, osems, *, bm, n_blk, chunk):
    i = pl.program_id(0)
    n_used = nu_ref[0]
    d_ff = wd_ref.shape[0]
    half = xbuf_ref.shape[2]
    slot = i % 2

    def fetch(b, s):
        return _row_copy(_rows(xs_hbm, pl.multiple_of(b * bm, bm), bm), xbuf_ref.at[s], xsems.at[s])

    def put(b, s):
        return _row_copy(obuf_ref.at[s], _rows(o_hbm, pl.multiple_of(b * bm, bm), bm), osems.at[s])

    class gu_copy:
        def __init__(self, e):
            self.copies = (_row_copy(wg_hbm.at[e], wg_ref, wsems.at[0]), _row_copy(wu_hbm.at[e], wu_ref, wsems.at[1]))

        def start(self):
            for c in self.copies:
                c.start()

        def wait(self):
            for c in self.copies:
                c.wait()

    def down_copy(e):
        return _row_copy(wd_hbm.at[e], wd_ref, wsems.at[2])

    @pl.when(i == 0)
    def _():
        fetch(0, 0).start()

    @pl.when(i + 1 < n_used)
    def _():
        fetch(i + 1, 1 - slot).start()

    @pl.when(i >= 2)
    def _():
        put(i - 2, slot).wait()

    @pl.when(i < n_used)
    def _():
        e = be_ref[i]
        first = (i == 0) | (e != be_ref[jnp.maximum(i - 1, 0)])
        e_next = be_ref[jnp.minimum(i + 1, n_used - 1)]

        @pl.when(i == 0)
        def _():
            gu_copy(e).start()

        @pl.when(first)
        def _():
            down_copy(e).start()
            gu_copy(e).wait()

        fetch(i, slot).wait()

        def ffn(m):
            lo, hi = _unpack_halves(xbuf_ref[slot, :m])
            xb_ref[:m, :half] = lo.astype(_BF16)
            xb_ref[:m, half:] = hi.astype(_BF16)
            xb = xb_ref[:m]
            for c in range(d_ff // chunk):
                cs = slice(c * chunk, (c + 1) * chunk)
                us = slice(d_ff + c * chunk, d_ff + (c + 1) * chunk)
                g = jnp.minimum(_dot(xb, wg_ref[:, cs]) + bgu_ref[:, cs], SWIGLU_LIMIT)
                u = jnp.clip(_dot(xb, wu_ref[:, cs]) + bgu_ref[:, us], -SWIGLU_LIMIT, SWIGLU_LIMIT)
                a_ref[:m, cs] = (g * jax.nn.sigmoid(SWIGLU_ALPHA * g) * (u + 1.0)).astype(_BF16)

            @pl.when(e_next != e)
            def _():
                gu_copy(e_next).start()

            @pl.when(first)
            def _():
                down_copy(e).wait()

            a = a_ref[:m]
            for c in range(half // chunk):
                cs = slice(c * chunk, (c + 1) * chunk)
                hs = slice(half + c * chunk, half + (c + 1) * chunk)
                o_lo = _dot(a, wd_ref[:, cs]) + bd_ref[:, cs]
                o_hi = _dot(a, wd_ref[:, hs]) + bd_ref[:, hs]
                obuf_ref[slot, :m, cs] = _pack_pair(o_lo, o_hi)
            if m < bm:
                obuf_ref[slot, m:] = jnp.zeros((bm - m, half), obuf_ref.dtype)

        short = valid_ref[i] <= bm // 2

        @pl.when(jnp.logical_not(short))
        def _():
            ffn(bm)

        @pl.when(short)
        def _():
            ffn(bm // 2)

    @pl.when(i >= n_used)
    def _():
        obuf_ref[slot] = jnp.zeros(obuf_ref.shape[1:], obuf_ref.dtype)

    put(i, slot).start()

    @pl.when(i == n_blk - 1)
    def _():
        put(i, slot).wait()
        if n_blk > 1:
            put(i - 1, 1 - slot).wait()


def _expert_ffn(xs, block_e, n_used, valid, w_g_b, w_u_b, b_gu, w_down_b, b_down, bm, chunk):
    rows, _, half = xs.shape
    d = 2 * half
    n_experts, _, d_ff = w_g_b.shape
    two_ff = 2 * d_ff
    n_blk = rows // bm

    def blk(i, nu):
        return jnp.minimum(i, nu[0] - 1)

    hbm = pl.BlockSpec(memory_space=pl.ANY)
    grid_spec = pltpu.PrefetchScalarGridSpec(
        num_scalar_prefetch=3,
        grid=(n_blk,),
        in_specs=[
            pl.BlockSpec((None, 1, two_ff), lambda i, be, nu, vr: (be[blk(i, nu)], 0, 0)),
            pl.BlockSpec((None, 1, d), lambda i, be, nu, vr: (be[blk(i, nu)], 0, 0)),
            hbm, hbm, hbm, hbm,
        ],
        out_specs=hbm,
        scratch_shapes=[pltpu.VMEM((d, d_ff), _BF16), pltpu.VMEM((d, d_ff), _BF16), pltpu.VMEM((d_ff, d), _BF16),
                        pltpu.VMEM((2, bm, half), _U32), pltpu.VMEM((2, bm, half), _U32),
                        pltpu.VMEM((bm, d), _BF16), pltpu.VMEM((bm, d_ff), _BF16),
                        pltpu.SemaphoreType.DMA((3,)), pltpu.SemaphoreType.DMA((2,)), pltpu.SemaphoreType.DMA((2,))],
    )
    return pl.pallas_call(
        functools.partial(_ffn_body, bm=bm, n_blk=n_blk, chunk=chunk),
        grid_spec=grid_spec,
        out_shape=jax.ShapeDtypeStruct((rows, 1, half), _U32),
        compiler_params=_params(("arbitrary",), 56),
        name="expert_ffn",
    )(block_e, n_used, valid, b_gu.reshape(n_experts, 1, two_ff), b_down.reshape(n_experts, 1, d), xs, w_g_b, w_u_b,
      w_down_b)


def _combine_body(dcur_ref, dnext_ref, x1_ref, rec_ref, mod_ref, gf_ref, out_ref, y_ref, gbuf_ref, sems, *, final_norm):
    i = pl.program_id(0)
    n = pl.num_programs(0)
    tm, d = x1_ref.shape
    w = d // 2
    groups = tm // SUBLANES
    slot = i % 2

    def start_group(dest_ref, s, g):
        for j in range(SUBLANES):
            for k in range(TOP_K):
                _row_copy(out_ref.at[dest_ref[0, 0, (g * SUBLANES + j) * TOP_K + k]], gbuf_ref.at[s, k, g, pl.ds(j, 1)],
                          sems.at[s]).start(priority=k % 2)

    def wait_tile(s):
        for k in range(TOP_K):
            for g in range(groups):
                _row_copy(_rows(out_ref, 0, SUBLANES), gbuf_ref.at[s, k, g], sems.at[s]).wait()

    @pl.when(i == 0)
    def _():
        def issue(g, carry):
            start_group(dcur_ref, 0, g)
            return carry
        lax.fori_loop(0, groups, issue, 0)

    wait_tile(slot)
    rec = rec_ref[...]
    gate2 = mod_ref[5, 0]
    for g in range(groups):
        start_group(dnext_ref, 1 - slot, g)
        rows = slice(g * SUBLANES, (g + 1) * SUBLANES)
        lo = hi = None
        for k in range(TOP_K):
            l, h = _unpack_halves(gbuf_ref[slot, k, g])
            gk = rec[rows, REC_GATE + k:REC_GATE + k + 1]
            lo = l * gk if lo is None else lo + l * gk
            hi = h * gk if hi is None else hi + h * gk
        g2 = gate2 if gate2.shape[0] == 1 else gate2[rows]
        x2l = x1_ref[rows, :w] + g2[:, :w] * lo
        x2h = x1_ref[rows, w:] + g2[:, w:] * hi
        if final_norm:
            ms = (jnp.sum(x2l * x2l, axis=-1, keepdims=True) + jnp.sum(x2h * x2h, axis=-1, keepdims=True)) * (1.0 / d)
            inv = lax.rsqrt(ms + EPS)
            x2l = x2l * inv * gf_ref[:, :w]
            x2h = x2h * inv * gf_ref[:, w:]
        y_ref[rows, :w] = x2l
        y_ref[rows, w:] = x2h

    @pl.when(i == n - 1)
    def _():
        wait_tile(1 - slot)


def _combine(x1, out_sorted, dest, rec, mods4, g_final, tm, final_norm):
    t, d = x1.shape
    n_tiles = t // tm
    tiles_per_group = n_tiles // mods4.shape[1]
    dest3 = dest.reshape(n_tiles, 1, tm * TOP_K)
    dspec = lambda nxt: pl.BlockSpec((1, 1, tm * TOP_K), lambda i: (jnp.minimum(i + nxt, n_tiles - 1), 0, 0),
                                     memory_space=pltpu.SMEM)
    return pl.pallas_call(
        functools.partial(_combine_body, final_norm=final_norm),
        grid=(n_tiles,),
        in_specs=[
            dspec(0), dspec(1),
            pl.BlockSpec((tm, d), lambda i: (i, 0)),
            pl.BlockSpec((tm, LANES), lambda i: (i, 0)),
            _mod_spec(mods4, tiles_per_group),
            pl.BlockSpec((1, d), lambda i: (0, 0)),
            pl.BlockSpec(memory_space=pl.ANY),
        ],
        out_specs=pl.BlockSpec((tm, d), lambda i: (i, 0)),
        out_shape=jax.ShapeDtypeStruct((t, d), _F32),
        scratch_shapes=[pltpu.VMEM((2, TOP_K, tm // SUBLANES, SUBLANES, d // 2), _U32),
                        pltpu.SemaphoreType.DMA((2,))],
        compiler_params=_params(("arbitrary",), 40),
        name="combine_final_norm",
    )(dest3, dest3, x1, rec, mods4, g_final.reshape(1, d), out_sorted)


def _tile(n, pref):
    t = pref
    while n % t:
        t //= 2
    return t


def _moe(h2_p, h2_s, rec_all, counts, w_g_b, w_u_b, b_gu, w_down_b, b_down, tt, bm, chunk):
    t_all = rec_all.shape[0]
    n_experts = w_g_b.shape[0]
    n_blk = -(-(t_all * TOP_K) // bm) + n_experts
    cnt = counts[0, :n_experts].astype(_I32)
    padded = (cnt + bm - 1) // bm * bm
    pends = jnp.cumsum(padded).astype(_I32)
    pstarts = pends - padded
    idx = rec_all[:, REC_IDX:REC_IDX + TOP_K].astype(_I32)
    rank = rec_all[:, REC_RANK:REC_RANK + TOP_K].astype(_I32)
    experts = jnp.arange(n_experts, dtype=_I32)
    dest = rank + jnp.sum(jnp.where(idx[:, :, None] == experts, pstarts, 0), axis=-1)
    blk_row0 = jnp.arange(n_blk, dtype=_I32) * bm
    block_e = jnp.minimum(jnp.sum((pends[None, :] <= blk_row0[:, None]).astype(_I32), axis=1), n_experts - 1)
    n_used = (pends[-1:] // bm).astype(_I32)
    row_end = jnp.sum(jnp.where(block_e[:, None] == experts, pstarts + cnt, 0), axis=-1)
    valid = jnp.clip(row_end - blk_row0, 0, bm).astype(_I32)
    xs = _dispatch(h2_p, h2_s, dest, pends, padded, n_used, n_blk * bm, tt, bm)
    return _expert_ffn(xs, block_e, n_used, valid, w_g_b, w_u_b, b_gu, w_down_b, b_down, bm, chunk), dest


def kernel(x_prompt, x_sample, c_prompt, c_sample, cache_k, cache_v, state_conv, w_ada, b_ada, g_mix, w_in, attn_sink, conv_w, conv_b, conv_ln_g, conv_ln_b, w_out, g_ffn, w_router, b_router, w_gu, b_gu, w_down, b_down, g_final):
    bp, sp, d = x_prompt.shape
    bs, ss, _ = x_sample.shape
    depth = w_ada.shape[0]
    n_experts = w_router.shape[-1]
    tp, ts = bp * sp, bs * ss
    t_all = tp + ts

    tm = _tile(sp, 256)
    to = _tile(sp, 512)
    tq = _tile(sp, 256)
    tc = _tile(sp, 256)
    tt = _tile(ts, 128)
    bm = 512 if tp >= 8192 else 64
    chunk = 512
    assert tp % ts == 0 and tp % tt == 0 and sp % CHUNK == 0 and tq % WINDOW == 0 and tc % CONV_HALO == 0

    xp = x_prompt.reshape(tp, d)
    xsm = x_sample.reshape(ts, d)
    tabs_p = _rope_tables(np.arange(sp))
    tabs_s = _rope_tables(np.tile(PAST_LEN + np.arange(ss), bs))

    new_kp, new_vp, new_cp, new_ks, new_vs, new_cs = [], [], [], [], [], []
    for l in range(depth):
        w_in_b = w_in[l].astype(_BF16)
        wo_a = w_out[l, :ATTN_WIDTH].astype(_BF16)
        wo_c = w_out[l, ATTN_WIDTH:].astype(_BF16)
        wr_pad = jnp.pad(w_router[l], ((0, 0), (0, LANES - n_experts)))
        wr_hi = wr_pad.astype(_BF16)
        wr_cat = jnp.concatenate([wr_hi, (wr_pad - wr_hi.astype(_F32)).astype(_BF16)], axis=1)
        br_pad = jnp.pad(b_router[l], (0, LANES - n_experts)).reshape(1, LANES)

        mods = _modulation(jnp.concatenate([c_prompt, c_sample], axis=0), w_ada[l], b_ada[l])
        mods_p = mods[:, :bp].reshape(6, bp, 1, d)
        mods_s = jnp.repeat(mods[:, bp:], ss, axis=1).reshape(6, 1, ts, d)

        d_ff = w_down.shape[2]
        w_gu2d = w_gu[l].reshape(n_experts * d, 2 * d_ff)

        qp, kp, vp, up, w_u_b = _in_proj(xp, mods_p, g_mix[l], w_in_b, tabs_p, tm, (w_gu2d, d_ff, 1))
        qs, ks, vs, us = _in_proj(xsm, mods_s, g_mix[l], w_in_b, tabs_s, ts)

        att_p, cv_p, w_g_b, w_down_b = _mixers(qp, kp, vp, attn_sink[l], up, conv_w[l], conv_b[l], conv_ln_g[l],
                                               conv_ln_b[l], bp, sp, tq, (w_gu2d, d_ff, 0),
                                               (w_down[l].reshape(n_experts * d_ff, d), d, 0))
        w_g_b = w_g_b.reshape(n_experts, d, d_ff)
        w_u_b = w_u_b.reshape(n_experts, d, d_ff)
        win = cache_k.shape[2]
        att_s = _cached_attention(qs, ks, vs, cache_k[l].reshape(bs, win, KV_COLS), cache_v[l].reshape(bs, win, KV_COLS),
                                  attn_sink[l], bs, ss)

        w_down_b = w_down_b.reshape(n_experts, d_ff, d)
        hist = jnp.pad(state_conv[l], ((0, 0), (CONV_HALO - (CONV_WIDTH - 1), 0), (0, 0))).reshape(bs * CONV_HALO, CONV_CH)
        cv_s = _conv_module(us, hist, conv_w[l], conv_b[l], conv_ln_g[l], conv_ln_b[l], bs, ss, ss)

        zero_counts = jnp.zeros((1, LANES), _F32)
        x1p, h2p, rec_p, counts = _out_proj(xp, att_p, cv_p, mods_p, g_ffn[l], wo_a, wo_c, wr_cat, br_pad,
                                            zero_counts, to, n_experts)
        x1s, h2s, rec_s, counts = _out_proj(xsm, att_s, cv_s, mods_s, g_ffn[l], wo_a, wo_c, wr_cat, br_pad,
                                            counts, ts, n_experts)

        out_sorted, dest = _moe(h2p, h2s, jnp.concatenate([rec_p, rec_s], axis=0), counts, w_g_b, w_u_b, b_gu[l],
                                w_down_b, b_down[l], tt, bm, chunk)

        last = l == depth - 1
        xp = _combine(x1p, out_sorted, dest[:tp], rec_p, mods_p, g_final, tm, last)
        xsm = _combine(x1s, out_sorted, dest[tp:], rec_s, mods_s, g_final, ts, last)

        keep = min(WINDOW, sp)
        new_kp.append(kp.reshape(bp, sp, KV_COLS)[:, sp - keep:].reshape(bp, keep, N_KV_HEADS, HEAD_DIM))
        new_vp.append(vp.reshape(bp, sp, KV_COLS)[:, sp - keep:].reshape(bp, keep, N_KV_HEADS, HEAD_DIM))
        new_cp.append(up.reshape(bp, sp, CONV_CH)[:, -(CONV_WIDTH - 1):])
        new_ks.append(ks.reshape(bs, ss, N_KV_HEADS, HEAD_DIM))
        new_vs.append(vs.reshape(bs, ss, N_KV_HEADS, HEAD_DIM))
        new_cs.append(jnp.concatenate([state_conv[l], us.reshape(bs, ss, CONV_CH)], axis=1)[:, -(CONV_WIDTH - 1):])

    return (xp.reshape(bp, sp, d), xsm.reshape(bs, ss, d), jnp.stack(new_kp), jnp.stack(new_vp), jnp.stack(new_cp),
            jnp.stack(new_ks), jnp.stack(new_vs), jnp.stack(new_cs))
```
